```python
import math
import jax, jax.numpy as jnp
from jax import lax
import numpy as np

D_MODEL = 1024
BATCH = 8
SEQ = 2048
DEPTH = 1
DEC_BATCH = 2
DEC_SEQ = 16384
PAST_LEN = 128

D_A = D_MODEL // 2
H_A = 4
DH_A = D_A // H_A
D_B = D_MODEL // 2
G_B = 4
DG_B = D_B // G_B
SGU_CHUNK = 128
MLSTM_CHUNK = 128
CONV_K = 3
N_GATES = 4 * H_A
D_IN = 4 * D_A + N_GATES + 2 * D_B
N_EXPERTS = 16
CAPACITY_FACTOR = 2
D_EXPERT = 2048
EPS = 1e-6

kernel_name = "hymba_mlstm_sgu_ecmoe_encoder"


def rms_norm(x, w):
    xf = x.astype(jnp.float32)
    y = xf * lax.rsqrt(jnp.mean(xf * xf, axis=-1, keepdims=True) + EPS)
    return (y * w.astype(jnp.float32)).astype(x.dtype)


def centred_dwconv(x, w, b):
    K = w.shape[0]
    pad = K // 2
    T = x.shape[1]
    xp = jnp.pad(x, ((0, 0), (pad, pad), (0, 0)))
    out = b + w[0] * xp[:, 0:T]
    for j in range(1, K):
        out = out + w[j] * xp[:, j:j + T]
    return out


def mlstm_scan(q, k, v, log_i, log_f):
    B, H, T, d = q.shape
    L = MLSTM_CHUNK
    NC = T // L

    def chunks(a):
        return jnp.moveaxis(a.reshape((B, H, NC, L) + a.shape[3:]), 2, 0)

    tril = jnp.tril(jnp.ones((L, L), dtype=bool))

    def step(carry, inp):
        C, n, m = carry
        qc, kc, vc, ic, fc = inp
        b = jnp.cumsum(fc, axis=-1)
        D = b[..., :, None] - b[..., None, :] + ic[..., None, :]
        D = jnp.where(tril, D, -jnp.inf)
        inter = b + m[..., None]
        m_t = jnp.maximum(jnp.max(D, axis=-1), inter)
        w_intra = jnp.exp(D - m_t[..., None])
        w_inter = jnp.exp(inter - m_t)
        s = jnp.einsum('bhtd,bhsd->bhts', qc, kc) * w_intra
        num = jnp.einsum('bhts,bhsd->bhtd', s, vc) + \
            w_inter[..., None] * jnp.einsum('bhvk,bhtk->bhtv', C, qc)
        den = jnp.sum(s, axis=-1) + w_inter * jnp.einsum('bhk,bhtk->bht', n, qc)
        h = num / jnp.maximum(jnp.abs(den), jnp.exp(-m_t))[..., None]
        bL = b[..., -1]
        g = bL[..., None] - b + ic
        m_new = jnp.maximum(bL + m, jnp.max(g, axis=-1))
        wk = jnp.exp(g - m_new[..., None])
        decay = jnp.exp(bL + m - m_new)
        C_new = decay[..., None, None] * C + jnp.einsum('bhs,bhsv,bhsk->bhvk', wk, vc, kc)
        n_new = decay[..., None] * n + jnp.einsum('bhs,bhsk->bhk', wk, kc)
        return (C_new, n_new, m_new), h

    init = (jnp.zeros((B, H, d, d), jnp.float32),
            jnp.zeros((B, H, d), jnp.float32),
            jnp.zeros((B, H), jnp.float32))
    _, hs = lax.scan(step, init, (chunks(q), chunks(k), chunks(v), chunks(log_i), chunks(log_f)))
    return jnp.moveaxis(hs, 0, 2).reshape(B, H, T, d)


def mlstm_mixer(zq, zk, zv, zo, zg, conv_w, conv_b, gate_b, norm_w):
    B, T, _ = zq.shape
    qk = jax.nn.silu(centred_dwconv(jnp.concatenate([zq, zk], axis=-1), conv_w, conv_b))
    q, k = qk[..., :D_A], qk[..., D_A:]

    def heads(a):
        return a.reshape(B, T, H_A, DH_A).transpose(0, 2, 1, 3).astype(jnp.float32)

    q, k, v = heads(q), heads(k) * (1.0 / math.sqrt(DH_A)), heads(zv)
    g = (zg + gate_b).astype(jnp.float32).transpose(0, 2, 1)
    i_f, i_b = g[:, 0:H_A], g[:, H_A:2 * H_A]
    lf_f = jax.nn.log_sigmoid(g[:, 2 * H_A:3 * H_A])
    lf_b = jax.nn.log_sigmoid(g[:, 3 * H_A:4 * H_A])
    h_fwd = mlstm_scan(q, k, v, i_f, lf_f)
    flip = lambda a: jnp.flip(a, axis=2)
    h_bwd = flip(mlstm_scan(flip(q), flip(k), flip(v), flip(i_b), flip(lf_b)))
    h = h_fwd + h_bwd
    h = h * lax.rsqrt(jnp.mean(h * h, axis=-1, keepdims=True) + EPS)
    h = h * norm_w.astype(jnp.float32).reshape(H_A, 1, DH_A)
    h = h.transpose(0, 2, 1, 3).reshape(B, T, D_A)
    return (jax.nn.sigmoid(zo.astype(jnp.float32)) * h).astype(zq.dtype)


def sgu_mixer(zu, zv, norm_w, w_s, b_s):
    B, T, _ = zu.shape
    NC = T // SGU_CHUNK
    u = jax.nn.gelu(zu, approximate=False)
    v = jax.nn.gelu(zv, approximate=False).reshape(B, T, G_B, DG_B)
    v = rms_norm(v, norm_w.reshape(G_B, DG_B))
    v = v.reshape(B, NC, SGU_CHUNK, G_B, DG_B)
    gate = jnp.einsum('gts,bcsgd->bctgd', w_s, v) + b_s.T[None, None, :, :, None]
    return u * gate.reshape(B, T, D_B)


def ec_moe(x, router_w, w_gate, w_up, w_down):
    B, T, D = x.shape
    N = B * T
    xf = x.reshape(N, D)
    affinity = jax.nn.softmax(xf.astype(jnp.float32) @ router_w.astype(jnp.float32), axis=-1)
    cap = (N * CAPACITY_FACTOR) // N_EXPERTS
    vals, idx = lax.top_k(affinity.T, cap)
    xg = xf[idx]
    hid = jax.nn.silu(jnp.einsum('ecd,edf->ecf', xg, w_gate)) * jnp.einsum('ecd,edf->ecf', xg, w_up)
    y = jnp.einsum('ecf,efd->ecd', hid, w_down) * vals[..., None]
    out = jnp.zeros_like(xf).at[idx.reshape(-1)].add(y.reshape(-1, D).astype(xf.dtype))
    return out.reshape(B, T, D)


def encoder(x, norm_mix_w, w_in, conv_w, conv_b, gate_b, mlstm_norm_w, sgu_norm_w,
            sgu_w, sgu_b, w_out, norm_ffn_w, router_w, w_gate, w_up, w_down, norm_final_w):
    splits = [D_A, 2 * D_A, 3 * D_A, 4 * D_A, 4 * D_A + N_GATES, 4 * D_A + N_GATES + D_B]
    for l in range(DEPTH):
        h = rms_norm(x, norm_mix_w[l])
        z = h @ w_in[l]
        zq, zk, zv, zo, zg, zu, zs = jnp.split(z, splits, axis=-1)
        a_out = mlstm_mixer(zq, zk, zv, zo, zg, conv_w[l], conv_b[l], gate_b[l], mlstm_norm_w[l])
        b_out = sgu_mixer(zu, zs, sgu_norm_w[l], sgu_w[l], sgu_b[l])
        x = x + jnp.concatenate([a_out, b_out], axis=-1) @ w_out[l]
        x = x + ec_moe(rms_norm(x, norm_ffn_w[l]), router_w[l], w_gate[l], w_up[l], w_down[l])
    return rms_norm(x, norm_final_w)


def setup_inputs(seed: int = 0) -> dict:
    key = jax.random.key(seed)
    ks = jax.random.split(key, 20)
    f32 = jnp.float32
    nrm = lambda k, shape, scale: scale * jax.random.normal(k, shape, f32)
    gate_b = jnp.concatenate(
        [nrm(ks[5], (DEPTH, 2 * H_A), 0.1),
         jnp.tile(jnp.linspace(3.0, 6.0, H_A, dtype=f32), 2)[None] + nrm(ks[6], (DEPTH, 2 * H_A), 0.1)],
        axis=-1)
    return {
        "x_prompt": nrm(ks[0], (BATCH, SEQ, D_MODEL), 1.0),
        "x_sample": nrm(ks[1], (DEC_BATCH, DEC_SEQ, D_MODEL), 1.0),
        "norm_mix_w": 1.0 + nrm(ks[2], (DEPTH, D_MODEL), 0.05),
        "w_in": nrm(ks[3], (DEPTH, D_MODEL, D_IN), D_MODEL ** -0.5),
        "conv_w": nrm(ks[4], (DEPTH, CONV_K, 2 * D_A), CONV_K ** -0.5),
        "conv_b": nrm(ks[7], (DEPTH, 2 * D_A), 0.02),
        "gate_b": gate_b,
        "mlstm_norm_w": 1.0 + nrm(ks[8], (DEPTH, D_A), 0.05),
        "sgu_norm_w": 1.0 + nrm(ks[9], (DEPTH, D_B), 0.05),
        "sgu_w": nrm(ks[10], (DEPTH, G_B, SGU_CHUNK, SGU_CHUNK), SGU_CHUNK ** -0.5),
        "sgu_b": 1.0 + nrm(ks[11], (DEPTH, G_B, SGU_CHUNK), 0.1),
        "w_out": nrm(ks[12], (DEPTH, D_A + D_B, D_MODEL), (D_A + D_B) ** -0.5),
        "norm_ffn_w": 1.0 + nrm(ks[13], (DEPTH, D_MODEL), 0.05),
        "router_w": nrm(ks[14], (DEPTH, D_MODEL, N_EXPERTS), D_MODEL ** -0.5),
        "w_gate": nrm(ks[15], (DEPTH, N_EXPERTS, D_MODEL, D_EXPERT), D_MODEL ** -0.5),
        "w_up": nrm(ks[16], (DEPTH, N_EXPERTS, D_MODEL, D_EXPERT), D_MODEL ** -0.5),
        "w_down": nrm(ks[17], (DEPTH, N_EXPERTS, D_EXPERT, D_MODEL), D_EXPERT ** -0.5),
        "norm_final_w": 1.0 + nrm(ks[18], (D_MODEL,), 0.05),
    }


def reference(x_prompt, x_sample, norm_mix_w, w_in, conv_w, conv_b, gate_b, mlstm_norm_w,
              sgu_norm_w, sgu_w, sgu_b, w_out, norm_ffn_w, router_w, w_gate, w_up, w_down,
              norm_final_w):
    y_prompt = encoder(x_prompt, norm_mix_w, w_in, conv_w, conv_b, gate_b, mlstm_norm_w,
                       sgu_norm_w, sgu_w, sgu_b, w_out, norm_ffn_w, router_w, w_gate, w_up,
                       w_down, norm_final_w)
    y_sample = encoder(x_sample, norm_mix_w, w_in, conv_w, conv_b, gate_b, mlstm_norm_w,
                       sgu_norm_w, sgu_w, sgu_b, w_out, norm_ffn_w, router_w, w_gate, w_up,
                       w_down, norm_final_w)
    return (y_prompt, y_sample)
```

```python
import functools
import math

import jax
import jax.numpy as jnp
from jax import lax
from jax.experimental import pallas as pl
from jax.experimental.pallas import tpu as pltpu

F32 = jnp.float32
BF16 = jnp.bfloat16
I32 = jnp.int32

EPS = 1e-6
N_HEADS = 4
HEAD_DIM = 128
CHUNK = 128
N_EXPERTS = 16
CAPACITY_FACTOR = 2
LANES = 128
SUBLANES = 8
BF16_ROWS = 16
NEG_BIG = -1e30
VMEM_LIMIT = 48 * 1024 * 1024

_NT = (((1,), (1,)), ((), ()))
_TN = (((0,), (0,)), ((), ()))


def _cparams(sem, vmem=VMEM_LIMIT):
    return pltpu.CompilerParams(dimension_semantics=sem, vmem_limit_bytes=vmem)


def _dot(a, b):
    return jnp.dot(a, b, preferred_element_type=F32)


def _sigmoid(x):
    return 1.0 / (1.0 + jnp.exp(-x))


def _gelu(x):
    return 0.5 * x * (1.0 + lax.erf(x * (1.0 / math.sqrt(2.0))))


def _rms(x, w):
    ms = jnp.mean(x * x, axis=-1, keepdims=True)
    return x * lax.rsqrt(ms + EPS) * w


def _in_proj_kernel(x_ref, xp_ref, xn_ref, nw_ref, wqk_ref, wv_ref, wo_ref, wg_ref, wgt_ref,
                    wu_ref, ws_ref, cw_ref, cb_ref, gb_ref, gbt_ref, snw_ref,
                    q_ref, k_ref, v_ref, og_ref, g_ref, gt_ref, u_ref, s_ref):
    i = pl.program_id(1)
    n_i = pl.num_programs(1)
    tm = x_ref.shape[1]
    d_a = q_ref.shape[2]
    nw = nw_ref[...]
    hb = _rms(x_ref[0], nw).astype(BF16)
    hp = _rms(xp_ref[0], nw).astype(BF16)
    hn = _rms(xn_ref[0], nw).astype(BF16)

    wqk = wqk_ref[...]
    z = _dot(hb, wqk)
    zp = _dot(hp, wqk)[SUBLANES - 1:SUBLANES, :]
    zn = _dot(hn, wqk)[0:1, :]
    zp = jnp.where(i == 0, 0.0, zp)
    zn = jnp.where(i == n_i - 1, 0.0, zn)
    row = lax.broadcasted_iota(I32, z.shape, 0)
    z_prev = jnp.where(row == 0, zp, pltpu.roll(z, 1, axis=0))
    z_next = jnp.where(row == tm - 1, zn, pltpu.roll(z, tm - 1, axis=0))
    cw = cw_ref[...]
    conv = cb_ref[...] + cw[0:1, :] * z_prev + cw[1:2, :] * z + cw[2:3, :] * z_next
    qk = conv * _sigmoid(conv)
    q_ref[0] = qk[:, :d_a].astype(BF16)
    k_ref[0] = (qk[:, d_a:] * (1.0 / math.sqrt(HEAD_DIM))).astype(BF16)

    v_ref[0] = _dot(hb, wv_ref[...]).astype(BF16)
    og_ref[0] = _sigmoid(_dot(hb, wo_ref[...])).astype(BF16)

    zg = _dot(hb, wg_ref[...]) + gb_ref[...]
    zgt = lax.dot_general(wgt_ref[...], hb, _NT, preferred_element_type=F32) + gbt_ref[...]
    col = lax.broadcasted_iota(I32, zg.shape, 1)
    g_ref[0] = jnp.where(col < 2 * N_HEADS, zg, jax.nn.log_sigmoid(zg))
    rowt = lax.broadcasted_iota(I32, zgt.shape, 0)
    gt_ref[0] = jnp.where(rowt < 2 * N_HEADS, zgt, jax.nn.log_sigmoid(zgt))

    u_ref[0] = _gelu(_dot(hb, wu_ref[...])).astype(BF16)
    sv = _gelu(_dot(hb, ws_ref[...]))
    snw = snw_ref[...]
    parts = []
    for g in range(sv.shape[1] // HEAD_DIM):
        sl = slice(g * HEAD_DIM, (g + 1) * HEAD_DIM)
        parts.append(_rms(sv[:, sl], snw[:, sl]))
    s_ref[0] = jnp.concatenate(parts, axis=1).astype(BF16)


def _in_proj(x, nw, wqk, wv, wo, wg, wgt, wu, ws, cw, cb, gb, gbt, snw, tm):
    B, T, D = x.shape
    d_a = wv.shape[1]
    d_b = wu.shape[1]
    ng = wg.shape[1]
    nt = T // tm
    hb8 = tm // SUBLANES
    last8 = T // SUBLANES - 1
    full = lambda a: pl.BlockSpec(a.shape, lambda b, i: (0,) * a.ndim)
    tok = lambda w: pl.BlockSpec((1, tm, w), lambda b, i: (b, i, 0))
    in_specs = [
        pl.BlockSpec((1, tm, D), lambda b, i: (b, i, 0)),
        pl.BlockSpec((1, SUBLANES, D), lambda b, i: (b, jnp.maximum(i * hb8 - 1, 0), 0)),
        pl.BlockSpec((1, SUBLANES, D), lambda b, i: (b, jnp.minimum((i + 1) * hb8, last8), 0)),
    ] + [full(a) for a in (nw, wqk, wv, wo, wg, wgt, wu, ws, cw, cb, gb, gbt, snw)]
    out_shape = (
        jax.ShapeDtypeStruct((B, T, d_a), BF16), jax.ShapeDtypeStruct((B, T, d_a), BF16),
        jax.ShapeDtypeStruct((B, T, d_a), BF16), jax.ShapeDtypeStruct((B, T, d_a), BF16),
        jax.ShapeDtypeStruct((B, T, ng), F32), jax.ShapeDtypeStruct((B, ng, T), F32),
        jax.ShapeDtypeStruct((B, T, d_b), BF16), jax.ShapeDtypeStruct((B, T, d_b), BF16),
    )
    out_specs = (tok(d_a), tok(d_a), tok(d_a), tok(d_a), tok(ng),
                 pl.BlockSpec((1, ng, tm), lambda b, i: (b, 0, i)), tok(d_b), tok(d_b))
    return pl.pallas_call(
        _in_proj_kernel, out_shape=out_shape, grid=(B, nt), in_specs=in_specs,
        out_specs=out_specs, compiler_params=_cparams(("parallel", "arbitrary")),
        name="in_proj")(x, x, x, nw, wqk, wv, wo, wg, wgt, wu, ws, cw, cb, gb, gbt, snw)


def _mlstm_kernel(q_ref, k_ref, v_ref, g_ref, gt_ref, h_ref, c_st, n_st, m_st):
    hd = pl.program_id(1)
    dr = pl.program_id(2)
    c = pl.program_id(3)
    L = q_ref.shape[1]

    @pl.when(c == 0)
    def _():
        c_st[...] = jnp.zeros_like(c_st)
        n_st[...] = jnp.zeros_like(n_st)
        m_st[...] = jnp.zeros_like(m_st)

    qb = q_ref[0]
    kb = k_ref[0]
    vb = v_ref[0]
    g = g_ref[0]
    gt = gt_ref[0]
    ci = dr * N_HEADS + hd
    cf = 2 * N_HEADS + ci
    lane = lax.broadcasted_iota(I32, g.shape, 1)
    i_col = jnp.sum(jnp.where(lane == ci, g, 0.0), axis=1, keepdims=True)
    f_col = jnp.sum(jnp.where(lane == cf, g, 0.0), axis=1, keepdims=True)
    sub = lax.broadcasted_iota(I32, gt.shape, 0)
    i_row = jnp.sum(jnp.where(sub == ci, gt, 0.0), axis=0, keepdims=True)
    f_row = jnp.sum(jnp.where(sub == cf, gt, 0.0), axis=0, keepdims=True)

    tt = lax.broadcasted_iota(I32, (L, L), 0)
    ss = lax.broadcasted_iota(I32, (L, L), 1)
    sgn = 1 - 2 * dr
    mask = (ss - tt) * sgn <= 0
    b_col = jnp.sum(jnp.where(mask, f_row, 0.0), axis=1, keepdims=True)
    mask_t = (tt - ss) * sgn <= 0
    b_row = jnp.sum(jnp.where(mask_t, f_col, 0.0), axis=0, keepdims=True)
    b_tot = jnp.sum(f_row, axis=1, keepdims=True)

    m_prev = m_st[...]
    dmat = jnp.where(mask, b_col - b_row + i_row, NEG_BIG)
    inter = b_col + m_prev
    m_t = jnp.maximum(jnp.max(dmat, axis=1, keepdims=True), inter)
    w_intra = jnp.exp(dmat - m_t)
    w_inter = jnp.exp(inter - m_t)

    s = lax.dot_general(qb, kb, _NT, preferred_element_type=F32) * w_intra
    c_prev = c_st[...]
    n_prev = n_st[...]
    num = _dot(s.astype(BF16), vb) + w_inter * lax.dot_general(
        qb, c_prev.astype(BF16), _NT, preferred_element_type=F32)
    qf = qb.astype(F32)
    den = jnp.sum(s, axis=1, keepdims=True) + w_inter * jnp.sum(qf * n_prev, axis=1, keepdims=True)
    h_ref[0, 0] = num / jnp.maximum(jnp.abs(den), jnp.exp(-m_t))

    g_col = b_tot - b_col + i_col
    g_row = b_tot - b_row + i_row
    m_new = jnp.maximum(b_tot + m_prev, jnp.max(g_row, axis=1, keepdims=True))
    wk = jnp.exp(g_col - m_new)
    decay = jnp.exp(b_tot + m_prev - m_new)
    kf = kb.astype(F32)
    vw = (vb.astype(F32) * wk).astype(BF16)
    c_st[...] = decay * c_prev + lax.dot_general(vw, kb, _TN, preferred_element_type=F32)
    n_st[...] = decay * n_prev + jnp.sum(kf * wk, axis=0, keepdims=True)
    m_st[...] = m_new


def _mlstm(q, k, v, g, gt):
    B, T, d_a = q.shape
    L = CHUNK
    nc = T // L
    ng = g.shape[2]
    cidx = lambda c, dr: c + dr * (nc - 1 - 2 * c)
    hspec = pl.BlockSpec((1, L, HEAD_DIM), lambda b, h, dr, c: (b, cidx(c, dr), h))
    in_specs = [hspec, hspec, hspec,
                pl.BlockSpec((1, L, ng), lambda b, h, dr, c: (b, cidx(c, dr), 0)),
                pl.BlockSpec((1, ng, L), lambda b, h, dr, c: (b, 0, cidx(c, dr)))]
    out_spec = pl.BlockSpec((1, 1, L, HEAD_DIM), lambda b, h, dr, c: (dr, b, cidx(c, dr), h))
    return pl.pallas_call(
        _mlstm_kernel, out_shape=jax.ShapeDtypeStruct((2, B, T, d_a), F32),
        grid=(B, N_HEADS, 2, nc), in_specs=in_specs, out_specs=out_spec,
        scratch_shapes=[pltpu.VMEM((HEAD_DIM, HEAD_DIM), F32), pltpu.VMEM((1, HEAD_DIM), F32),
                        pltpu.VMEM((1, 1), F32)],
        compiler_params=_cparams(("parallel", "parallel", "parallel", "arbitrary")),
        name="mlstm")(q, k, v, g, gt)


def _mix_out_kernel(hf_ref, hb_ref, og_ref, u_ref, s_ref, x_ref, mnw_ref, sw_ref, sb_ref,
                    wout_ref, fnw_ref, rwt_ref, x1_ref, xn_ref, aff_ref):
    tm = x_ref.shape[0]
    d_a = og_ref.shape[1]
    h = hf_ref[0] + hb_ref[0]
    mnw = mnw_ref[...]
    parts = []
    for hd in range(d_a // HEAD_DIM):
        sl = slice(hd * HEAD_DIM, (hd + 1) * HEAD_DIM)
        parts.append(_rms(h[:, sl], mnw[:, sl]))
    a_out = (og_ref[...].astype(F32) * jnp.concatenate(parts, axis=1)).astype(BF16)

    sv = s_ref[...]
    sbias = sb_ref[...]
    rows = []
    for cc in range(tm // CHUNK):
        rs = slice(cc * CHUNK, (cc + 1) * CHUNK)
        cols = []
        for g in range(sv.shape[1] // HEAD_DIM):
            cs = slice(g * HEAD_DIM, (g + 1) * HEAD_DIM)
            cols.append(_dot(sw_ref[g], sv[rs, cs]))
        rows.append(jnp.concatenate(cols, axis=1) + sbias)
    gate = jnp.concatenate(rows, axis=0)
    b_out = (u_ref[...].astype(F32) * gate).astype(BF16)

    mix = jnp.concatenate([a_out, b_out], axis=1)
    x1 = x_ref[...] + _dot(mix, wout_ref[...])
    x1_ref[...] = x1
    xn = _rms(x1, fnw_ref[...])
    xn_ref[...] = xn
    logits = lax.dot_general(rwt_ref[...], xn, _NT, precision=lax.Precision.HIGHEST,
                             preferred_element_type=F32)
    mx = jnp.max(logits, axis=0, keepdims=True)
    ex = jnp.exp(logits - mx)
    aff = ex / jnp.sum(ex, axis=0, keepdims=True)
    for j in range(tm // LANES):
        aff_ref[j] = aff[:, j * LANES:(j + 1) * LANES]


def _mix_out(hdir, og, u, s, x, mnw, sw, sbias, wout, fnw, rwt, tm):
    N, D = x.shape
    d_a = og.shape[1]
    d_b = u.shape[1]
    E = rwt.shape[0]
    nt = N // tm
    full = lambda a: pl.BlockSpec(a.shape, lambda i: (0,) * a.ndim)
    tok = lambda w: pl.BlockSpec((tm, w), lambda i: (i, 0))
    in_specs = [pl.BlockSpec((1, tm, d_a), lambda i: (0, i, 0)),
                pl.BlockSpec((1, tm, d_a), lambda i: (1, i, 0)),
                tok(d_a), tok(d_b), tok(d_b), tok(D)] + [full(a) for a in (mnw, sw, sbias, wout, fnw, rwt)]
    out_shape = (jax.ShapeDtypeStruct((N, D), F32), jax.ShapeDtypeStruct((N, D), F32),
                 jax.ShapeDtypeStruct((N // LANES, E, LANES), F32))
    out_specs = (tok(D), tok(D), pl.BlockSpec((tm // LANES, E, LANES), lambda i: (i, 0, 0)))
    return pl.pallas_call(
        _mix_out_kernel, out_shape=out_shape, grid=(nt,), in_specs=in_specs, out_specs=out_specs,
        compiler_params=_cparams(("parallel",)), name="mix_out")(
            hdir, hdir, og, u, s, x, mnw, sw, sbias, wout, fnw, rwt)


def _select_kernel(aff_ref, posm_ref, off_ref, cnt_s, wi_s, *, cap):
    nb, E, _ = aff_ref.shape
    bits = pltpu.bitcast(aff_ref[...], I32)

    def count_ge(cand):
        c = jnp.sum((bits >= cand).astype(I32), axis=0, keepdims=True)
        return jnp.sum(c, axis=2, keepdims=True)

    def bit_step(i, thr):
        cand = thr | jnp.left_shift(jnp.int32(1), 30 - i)
        return jnp.where(count_ge(cand) >= cap, cand, thr)

    thr = lax.fori_loop(0, 31, bit_step, jnp.zeros((1, E, 1), I32))
    gt = bits > thr
    eq = bits == thr
    n_gt = jnp.sum(jnp.sum(gt.astype(I32), axis=0, keepdims=True), axis=2, keepdims=True)
    need = cap - n_gt

    li = lax.broadcasted_iota(I32, (LANES, LANES), 0)
    lj = lax.broadcasted_iota(I32, (LANES, LANES), 1)
    upper = (li < lj).astype(BF16)

    def excl_cumsum(flag):
        fb = flag.astype(BF16).reshape(nb * E, LANES)
        wi_s[...] = _dot(fb, upper).astype(I32).reshape(nb, E, LANES)
        cnt_s[...] = jnp.sum(flag.astype(I32), axis=2, keepdims=True)

        def blk(b, run):
            wi_s[b] = wi_s[b] + run
            return run + cnt_s[b]

        lax.fori_loop(0, nb, blk, jnp.zeros((E, 1), I32))
        return wi_s[...]

    eq_rank = excl_cumsum(eq)
    sel = gt | (eq & (eq_rank < need))
    pos = excl_cumsum(sel)
    posm_ref[...] = jnp.where(sel, pos, -1)
    off_ref[...] = jnp.broadcast_to(pos[:, :, 0:1], off_ref.shape)


def _select(aff3, cap):
    nb, E, _ = aff3.shape
    return pl.pallas_call(
        functools.partial(_select_kernel, cap=cap),
        out_shape=(jax.ShapeDtypeStruct((nb, E, LANES), I32), jax.ShapeDtypeStruct((nb, E, LANES), I32)),
        scratch_shapes=[pltpu.VMEM((nb, E, 1), I32), pltpu.VMEM((nb, E, LANES), I32)],
        compiler_params=_cparams(None), name="select")(aff3)


def _compact_kernel(off_sm, posm_ref, aff_ref, acc_ref):
    nb, E, _ = posm_ref.shape
    acc_ref[...] = jnp.zeros_like(acc_ref)
    srow = lax.broadcasted_iota(I32, (2 * LANES, LANES), 0)
    r8 = lax.broadcasted_iota(I32, (SUBLANES, LANES), 0)
    lane8 = lax.broadcasted_iota(I32, (SUBLANES, LANES), 1)

    def blk(b, carry):
        tok = b * LANES + lane8
        t_hi = jnp.right_shift(tok, 8).astype(F32)
        t_lo = jnp.bitwise_and(tok, 255).astype(F32)
        pm = posm_ref[b]
        af = aff_ref[b]
        for e in range(E):
            off = off_sm[b * E + e]
            j0 = jnp.right_shift(off, 7)
            rel = pm[e:e + 1, :] - j0 * LANES
            onehot = (srow == rel).astype(BF16)
            a = af[e:e + 1, :]
            a0 = a.astype(BF16)
            r1 = a - a0.astype(F32)
            a1 = r1.astype(BF16)
            a2 = (r1 - a1.astype(F32)).astype(BF16)
            lhs = jnp.where(r8 == 0, t_hi, jnp.where(r8 == 1, t_lo, 0.0))
            lhs = jnp.where(r8 == 2, a0.astype(F32), lhs)
            lhs = jnp.where(r8 == 3, a1.astype(F32), lhs)
            lhs = jnp.where(r8 == 4, a2.astype(F32), lhs).astype(BF16)
            out = lax.dot_general(lhs, onehot, _NT, preferred_element_type=F32)
            acc_ref[e, j0] = acc_ref[e, j0] + out[:, :LANES]
            acc_ref[e, j0 + 1] = acc_ref[e, j0 + 1] + out[:, LANES:]
        return carry

    lax.fori_loop(0, nb, blk, 0)


def _compact(off_flat, posm3, aff3, cap):
    nb, E, _ = posm3.shape
    nt = cap // LANES
    gs = pltpu.PrefetchScalarGridSpec(
        num_scalar_prefetch=1, grid=(1,),
        in_specs=[pl.BlockSpec(posm3.shape, lambda i, o: (0, 0, 0)),
                  pl.BlockSpec(aff3.shape, lambda i, o: (0, 0, 0))],
        out_specs=pl.BlockSpec((E, nt + 1, SUBLANES, LANES), lambda i, o: (0, 0, 0, 0)))
    return pl.pallas_call(
        _compact_kernel, out_shape=jax.ShapeDtypeStruct((E, nt + 1, SUBLANES, LANES), F32),
        grid_spec=gs, compiler_params=_cparams(("arbitrary",)), name="compact")(off_flat, posm3, aff3)


def _ffn_kernel(idc_sm, idn_sm, xn_hbm, val_ref, wg_ref, wu_ref, wd_ref, y_ref, xbuf, sem,
                *, n_real, fc):
    s = pl.program_id(0)
    ts = xbuf.shape[1]
    slot = lax.rem(s, 2)

    def gather(idx_sm, dst_slot):
        def body(r, carry):
            pltpu.make_async_copy(xn_hbm.at[pl.ds(idx_sm[0, 0, r], 1), :],
                                  xbuf.at[dst_slot, pl.ds(r, 1), :], sem.at[dst_slot]).start()
            return carry
        lax.fori_loop(0, ts, body, 0, unroll=8)

    @pl.when(s == 0)
    def _():
        gather(idc_sm, 0)

    @pl.when(s + 1 < n_real)
    def _():
        gather(idn_sm, 1 - slot)

    @pl.when(s < n_real)
    def _():
        pltpu.make_async_copy(xn_hbm.at[pl.ds(0, ts), :], xbuf.at[slot], sem.at[slot]).wait()
        x = xbuf[slot].astype(BF16)
        F = wg_ref.shape[2]
        acc = jnp.zeros((ts, wd_ref.shape[2]), F32)
        for c in range(F // fc):
            cs = slice(c * fc, (c + 1) * fc)
            gte = _dot(x, wg_ref[0, :, cs])
            up = _dot(x, wu_ref[0, :, cs])
            hid = (gte * _sigmoid(gte) * up).astype(BF16)
            acc = acc + _dot(hid, wd_ref[0, cs, :])
        y_ref[...] = (acc * val_ref[...]).astype(y_ref.dtype)

    @pl.when(s >= n_real)
    def _():
        y_ref[...] = jnp.zeros_like(y_ref)


def _ffn(idx3, xn, vals2, wg, wu, wd, cap, ts):
    E, D, F = wg.shape
    nts = cap // ts
    n_real = E * nts
    last = n_real - 1
    eidx = lambda s: jnp.minimum(s // nts, E - 1)
    in_specs = [
        pl.BlockSpec((1, 1, ts), lambda s: (jnp.minimum(s, last), 0, 0), memory_space=pltpu.SMEM),
        pl.BlockSpec((1, 1, ts), lambda s: (jnp.minimum(s + 1, last), 0, 0), memory_space=pltpu.SMEM),
        pl.BlockSpec(memory_space=pl.ANY),
        pl.BlockSpec((ts, 1), lambda s: (jnp.minimum(s, last), 0)),
        pl.BlockSpec((1, D, F), lambda s: (eidx(s), 0, 0)),
        pl.BlockSpec((1, D, F), lambda s: (eidx(s), 0, 0)),
        pl.BlockSpec((1, F, D), lambda s: (eidx(s), 0, 0)),
    ]
    return pl.pallas_call(
        functools.partial(_ffn_kernel, n_real=n_real, fc=512),
        out_shape=jax.ShapeDtypeStruct(((n_real + 1) * ts, D), BF16),
        grid=(n_real + 1,), in_specs=in_specs,
        out_specs=pl.BlockSpec((ts, D), lambda s: (s, 0)),
        scratch_shapes=[pltpu.VMEM((2, ts, D), F32), pltpu.SemaphoreType.DMA((2,))],
        compiler_params=_cparams(("arbitrary",)), name="ffn")(idx3, idx3, xn, vals2, wg, wu, wd)


WIN = 64


def _combine_kernel(off_sm, posm_ref, x1_ref, y_hbm, fnw_ref, o_ref, ycat, sem, *, cap, y_rows):
    b = pl.program_id(0)
    nsub, E, _ = posm_ref.shape
    tb = x1_ref.shape[0]
    pm = jnp.concatenate([posm_ref[j] for j in range(nsub)], axis=1)
    wrow = lax.broadcasted_iota(I32, (WIN, tb), 0)

    bases = []
    nrounds = jnp.int32(1)
    for e in range(E):
        off = off_sm[b * E + e]
        end = off_sm[(b + 1) * E + e]
        base = jnp.left_shift(jnp.right_shift(off, 4), 4)
        bases.append(base)
        nrounds = jnp.maximum(nrounds, jnp.right_shift(end - base + (WIN - 1), 6))

    def round_body(r, acc):
        starts = []
        for e in range(E):
            st = jnp.minimum(e * cap + bases[e] + r * WIN, y_rows - WIN)
            st = pl.multiple_of(st, BF16_ROWS)
            starts.append(st)
            pltpu.make_async_copy(y_hbm.at[pl.ds(st, WIN), :], ycat.at[pl.ds(e * WIN, WIN), :], sem).start()
        ps = []
        for e in range(E):
            rel = pm[e:e + 1, :] + (e * cap - starts[e])
            rel = jnp.where(pm[e:e + 1, :] >= 0, rel, -1)
            ps.append((wrow == rel).astype(BF16))
        p = jnp.concatenate(ps, axis=0)
        for e in range(E):
            pltpu.make_async_copy(y_hbm.at[pl.ds(0, WIN), :], ycat.at[pl.ds(e * WIN, WIN), :], sem).wait()
        return acc + lax.dot_general(p, ycat[...], _TN, preferred_element_type=F32)

    acc = lax.fori_loop(0, nrounds, round_body, jnp.zeros((tb, x1_ref.shape[1]), F32))
    o_ref[...] = _rms(x1_ref[...] + acc, fnw_ref[...])


def _combine(off_flat, posm3, x1, y, fnw, cap, tb):
    N, D = x1.shape
    nb, E, _ = posm3.shape
    nsub = tb // LANES
    y_rows = y.shape[0]
    gs = pltpu.PrefetchScalarGridSpec(
        num_scalar_prefetch=1, grid=(N // tb,),
        in_specs=[pl.BlockSpec((nsub, E, LANES), lambda i, o: (i, 0, 0)),
                  pl.BlockSpec((tb, D), lambda i, o: (i, 0)),
                  pl.BlockSpec(memory_space=pl.ANY),
                  pl.BlockSpec(fnw.shape, lambda i, o: (0, 0))],
        out_specs=pl.BlockSpec((tb, D), lambda i, o: (i, 0)),
        scratch_shapes=[pltpu.VMEM((E * WIN, D), BF16), pltpu.SemaphoreType.DMA])
    return pl.pallas_call(
        functools.partial(_combine_kernel, cap=cap, y_rows=y_rows),
        out_shape=jax.ShapeDtypeStruct((N, D), F32), grid_spec=gs,
        compiler_params=_cparams(("arbitrary",)), name="combine")(off_flat, posm3, x1, y, fnw)


def _prep_params(norm_mix_w, w_in, conv_w, conv_b, gate_b, mlstm_norm_w, sgu_norm_w, sgu_w, sgu_b,
                 w_out, norm_ffn_w, router_w, w_gate, w_up, w_down, norm_final_w):
    d_a = mlstm_norm_w.shape[1]
    d_b = sgu_norm_w.shape[1]
    ng = gate_b.shape[1]
    w = w_in[0]
    o0, o1, o2, o3, o4, o5 = 2 * d_a, 3 * d_a, 4 * d_a, 4 * d_a + ng, 4 * d_a + ng + d_b, 4 * d_a + ng + 2 * d_b
    wg = w[:, o2:o3]
    return dict(
        nw=norm_mix_w[0][None, :],
        wqk=w[:, :o0].astype(BF16), wv=w[:, o0:o1].astype(BF16), wo=w[:, o1:o2].astype(BF16),
        wg=wg.astype(BF16), wgt=wg.T.astype(BF16),
        wu=w[:, o3:o4].astype(BF16), ws=w[:, o4:o5].astype(BF16),
        cw=conv_w[0], cb=conv_b[0][None, :], gb=gate_b[0][None, :], gbt=gate_b[0][:, None],
        snw=sgu_norm_w[0][None, :], mnw=mlstm_norm_w[0][None, :],
        sw=sgu_w[0].astype(BF16),
        sbias=jnp.repeat(sgu_b[0].T, HEAD_DIM, axis=1),
        wout=w_out[0].astype(BF16), fnw=norm_ffn_w[0][None, :], rwt=router_w[0].T,
        wgate=w_gate[0].astype(BF16), wup=w_up[0].astype(BF16), wdown=w_down[0].astype(BF16),
        nfw=norm_final_w[None, :],
    )


def _encoder(x, p):
    B, T, D = x.shape
    N = B * T
    E = N_EXPERTS
    cap = (N * CAPACITY_FACTOR) // E
    tm1 = min(512, T)
    q, k, v, og, g, gt, u, s = _in_proj(
        x, p["nw"], p["wqk"], p["wv"], p["wo"], p["wg"], p["wgt"], p["wu"], p["ws"],
        p["cw"], p["cb"], p["gb"], p["gbt"], p["snw"], tm1)
    hdir = _mlstm(q, k, v, g, gt)
    d_a = og.shape[2]
    flat = lambda a: a.reshape(N, a.shape[-1])
    x1, xn, aff3 = _mix_out(hdir.reshape(2, N, d_a), flat(og), flat(u), flat(s), flat(x),
                            p["mnw"], p["sw"], p["sbias"], p["wout"], p["fnw"], p["rwt"], min(256, N))
    posm3, off3 = _select(aff3, cap)
    nb = N // LANES
    off_flat = off3[:, :, 0].reshape(nb * E)
    acc = _compact(off_flat, posm3, aff3, cap)
    nt = cap // LANES
    idx = (acc[:, :nt, 0, :] * 256.0 + acc[:, :nt, 1, :]).astype(I32).reshape(E * cap)
    vals = (acc[:, :nt, 2, :] + acc[:, :nt, 3, :] + acc[:, :nt, 4, :]).reshape(E * cap, 1)
    ts = min(256, cap)
    y = _ffn(idx.reshape(E * cap // ts, 1, ts), xn, vals, p["wgate"], p["wup"], p["wdown"], cap, ts)
    tb = min(256, N)
    sub = tb // LANES
    off_tb = jnp.concatenate([off3[::sub, :, 0], jnp.full((1, E), cap, I32)], axis=0).reshape(-1)
    out = _combine(off_tb, posm3, x1, y, p["nfw"], cap, tb)
    return out.reshape(B, T, D)


def kernel(x_prompt, x_sample, norm_mix_w, w_in, conv_w, conv_b, gate_b, mlstm_norm_w, sgu_norm_w,
           sgu_w, sgu_b, w_out, norm_ffn_w, router_w, w_gate, w_up, w_down, norm_final_w):
    p = _prep_params(norm_mix_w, w_in, conv_w, conv_b, gate_b, mlstm_norm_w, sgu_norm_w, sgu_w,
                     sgu_b, w_out, norm_ffn_w, router_w, w_gate, w_up, w_down, norm_final_w)
    return (_encoder(x_prompt, p), _encoder(x_sample, p))
```

```python
import functools
import math

import jax
import jax.numpy as jnp
from jax import lax
from jax.experimental import pallas as pl
from jax.experimental.pallas import tpu as pltpu

F32 = jnp.float32
BF16 = jnp.bfloat16
I32 = jnp.int32

EPS = 1e-6
N_HEADS = 4
HEAD_DIM = 128
CHUNK = 128
N_EXPERTS = 16
CAPACITY_FACTOR = 2
LANES = 128
SUBLANES = 8
BF16_ROWS = 16
NEG_BIG = -1e30
VMEM_LIMIT = 48 * 1024 * 1024

_NT = (((1,), (1,)), ((), ()))
_TN = (((0,), (0,)), ((), ()))


def _cparams(sem, vmem=VMEM_LIMIT):
    return pltpu.CompilerParams(dimension_semantics=sem, vmem_limit_bytes=vmem)


def _dot(a, b):
    return jnp.dot(a, b, preferred_element_type=F32)


def _sigmoid(x):
    return 1.0 / (1.0 + jnp.exp(-x))


def _gelu(x):
    return 0.5 * x * (1.0 + lax.erf(x * (1.0 / math.sqrt(2.0))))


def _rms(x, w):
    ms = jnp.mean(x * x, axis=-1, keepdims=True)
    return x * lax.rsqrt(ms + EPS) * w


def _in_proj_kernel(x_ref, xp_ref, xn_ref, nw_ref, wqk_ref, wvt_ref, wo_ref, wgt_ref,
                    wu_ref, ws_ref, cw_ref, cb_ref, gbt_ref, snw_ref,
                    q_ref, k_ref, vt_ref, og_ref, gt_ref, u_ref, s_ref):
    i = pl.program_id(1)
    n_i = pl.num_programs(1)
    tm = x_ref.shape[1]
    d_a = q_ref.shape[2]
    nw = nw_ref[...]
    hb = _rms(x_ref[0], nw).astype(BF16)
    hp = _rms(xp_ref[0], nw).astype(BF16)
    hn = _rms(xn_ref[0], nw).astype(BF16)

    wqk = wqk_ref[...]
    z = _dot(hb, wqk)
    zp = _dot(hp, wqk)[SUBLANES - 1:SUBLANES, :]
    zn = _dot(hn, wqk)[0:1, :]
    zp = jnp.where(i == 0, 0.0, zp)
    zn = jnp.where(i == n_i - 1, 0.0, zn)
    row = lax.broadcasted_iota(I32, z.shape, 0)
    z_prev = jnp.where(row == 0, zp, pltpu.roll(z, 1, axis=0))
    z_next = jnp.where(row == tm - 1, zn, pltpu.roll(z, tm - 1, axis=0))
    cw = cw_ref[...]
    conv = cb_ref[...] + cw[0:1, :] * z_prev + cw[1:2, :] * z + cw[2:3, :] * z_next
    qk = conv * _sigmoid(conv)
    q_ref[0] = qk[:, :d_a].astype(BF16)
    k_ref[0] = (qk[:, d_a:] * (1.0 / math.sqrt(HEAD_DIM))).astype(BF16)

    vt_ref[0] = lax.dot_general(wvt_ref[...], hb, _NT, preferred_element_type=F32).astype(BF16)
    og_ref[0] = _sigmoid(_dot(hb, wo_ref[...])).astype(BF16)

    zgt = lax.dot_general(wgt_ref[...], hb, _NT, preferred_element_type=F32) + gbt_ref[...]
    rowt = lax.broadcasted_iota(I32, zgt.shape, 0)
    gt_ref[0] = jnp.where(rowt < 2 * N_HEADS, zgt, jax.nn.log_sigmoid(zgt))

    u_ref[0] = _gelu(_dot(hb, wu_ref[...])).astype(BF16)
    sv = _gelu(_dot(hb, ws_ref[...]))
    snw = snw_ref[...]
    parts = []
    for g in range(sv.shape[1] // HEAD_DIM):
        sl = slice(g * HEAD_DIM, (g + 1) * HEAD_DIM)
        parts.append(_rms(sv[:, sl], snw[:, sl]))
    s_ref[0] = jnp.concatenate(parts, axis=1).astype(BF16)


def _in_proj(x, nw, wqk, wvt, wo, wgt, wu, ws, cw, cb, gbt, snw, tm):
    B, T, D = x.shape
    d_a = wvt.shape[0]
    d_b = wu.shape[1]
    ng = wgt.shape[0]
    nt = T // tm
    hb8 = tm // SUBLANES
    last8 = T // SUBLANES - 1
    full = lambda a: pl.BlockSpec(a.shape, lambda b, i: (0,) * a.ndim)
    tok = lambda w: pl.BlockSpec((1, tm, w), lambda b, i: (b, i, 0))
    in_specs = [
        pl.BlockSpec((1, tm, D), lambda b, i: (b, i, 0)),
        pl.BlockSpec((1, SUBLANES, D), lambda b, i: (b, jnp.maximum(i * hb8 - 1, 0), 0)),
        pl.BlockSpec((1, SUBLANES, D), lambda b, i: (b, jnp.minimum((i + 1) * hb8, last8), 0)),
    ] + [full(a) for a in (nw, wqk, wvt, wo, wgt, wu, ws, cw, cb, gbt, snw)]
    out_shape = (
        jax.ShapeDtypeStruct((B, T, d_a), BF16), jax.ShapeDtypeStruct((B, T, d_a), BF16),
        jax.ShapeDtypeStruct((B, d_a, T), BF16), jax.ShapeDtypeStruct((B, T, d_a), BF16),
        jax.ShapeDtypeStruct((B, ng, T), F32),
        jax.ShapeDtypeStruct((B, T, d_b), BF16), jax.ShapeDtypeStruct((B, T, d_b), BF16),
    )
    out_specs = (tok(d_a), tok(d_a), pl.BlockSpec((1, d_a, tm), lambda b, i: (b, 0, i)), tok(d_a),
                 pl.BlockSpec((1, ng, tm), lambda b, i: (b, 0, i)), tok(d_b), tok(d_b))
    return pl.pallas_call(
        _in_proj_kernel, out_shape=out_shape, grid=(B, nt), in_specs=in_specs,
        out_specs=out_specs, compiler_params=_cparams(("parallel", "arbitrary")),
        name="in_proj")(x, x, x, nw, wqk, wvt, wo, wgt, wu, ws, cw, cb, gbt, snw)


def _split3(x):
    x0 = x.astype(BF16)
    r1 = x - x0.astype(F32)
    x1 = r1.astype(BF16)
    x2 = (r1 - x1.astype(F32)).astype(BF16)
    return x0, x1, x2


def _mlstm_kernel(qf_ref, kf_ref, vtf_ref, gtf_ref, qb_ref, kb_ref, vtb_ref, gtb_ref,
                  hf_ref, hb_ref, c_st, m_st):
    c = pl.program_id(1)
    L = qf_ref.shape[1]
    d = HEAD_DIM

    @pl.when(c == 0)
    def _():
        c_st[...] = jnp.zeros_like(c_st)
        m_st[...] = jnp.zeros_like(m_st)

    r0 = lax.broadcasted_iota(I32, (L, L), 0)
    r1 = lax.broadcasted_iota(I32, (L, L), 1)
    ones8 = jnp.ones((SUBLANES, L), BF16)
    ng = 2 * N_HEADS
    dirs = ((qf_ref, kf_ref, vtf_ref, gtf_ref[0], r0 <= r1, hf_ref),
            (qb_ref, kb_ref, vtb_ref, gtb_ref[0], r0 >= r1, hb_ref))

    chains = []
    for dr, (q_ref, k_ref, vt_ref, gt, mask_st, h_ref) in enumerate(dirs):
        g3 = jnp.concatenate(_split3(gt), axis=0)
        b3 = _dot(g3, mask_st.astype(BF16))
        nr = gt.shape[0]
        br_all = b3[:nr] + b3[nr:2 * nr] + b3[2 * nr:]
        a_rows = gt[:ng] - br_all[ng:]
        a_cols = jnp.concatenate([a_rows, jnp.zeros((L - ng, L), F32)], axis=0).T
        for hd in range(N_HEADS):
            j = dr * N_HEADS + hd
            hs = slice(hd * d, (hd + 1) * d)
            chains.append(dict(
                j=j, hs=hs, h_ref=h_ref, mask=mask_st, qb=q_ref[0, :, hs], kb=k_ref[0, :, hs],
                vt_aug=jnp.concatenate([vt_ref[0, hs, :], ones8], axis=0),
                a_col=a_cols[:, j:j + 1], i_row=gt[j:j + 1, :], b_row=br_all[ng + j:ng + j + 1, :],
                b_tot=jnp.sum(gt[ng + j:ng + j + 1, :], axis=1, keepdims=True),
                caug=c_st[j], m_prev=m_st[j]))

    for ch in chains:
        ch["s_raw"] = lax.dot_general(ch["kb"], ch["qb"], _NT, preferred_element_type=F32)
        ch["ia"] = lax.dot_general(ch["caug"].astype(BF16), ch["qb"], _NT, preferred_element_type=F32)
    for ch in chains:
        dmat = jnp.where(ch["mask"], ch["a_col"] + ch["b_row"], NEG_BIG)
        inter = ch["b_row"] + ch["m_prev"]
        m_t = jnp.maximum(jnp.max(dmat, axis=0, keepdims=True), inter)
        st = ch["s_raw"] * jnp.exp(dmat - m_t)
        w_inter = jnp.exp(inter - m_t)
        den = jnp.sum(st, axis=0, keepdims=True) + w_inter * ch["ia"][d:d + 1]
        ch["st"] = st.astype(BF16)
        ch["w_inter"] = w_inter
        ch["rden"] = 1.0 / jnp.maximum(jnp.abs(den), jnp.exp(-m_t))
    for ch in chains:
        num = _dot(ch["vt_aug"][:d], ch["st"]) + ch["w_inter"] * ch["ia"][:d]
        ch["h_ref"][0, :, ch["hs"]] = (num * ch["rden"]).T
    for ch in chains:
        g_row = ch["b_tot"] - ch["b_row"] + ch["i_row"]
        m_new = jnp.maximum(ch["b_tot"] + ch["m_prev"], jnp.max(g_row, axis=1, keepdims=True))
        wk = jnp.exp(g_row - m_new)
        decay = jnp.exp(ch["b_tot"] + ch["m_prev"] - m_new)
        vw = (ch["vt_aug"].astype(F32) * wk).astype(BF16)
        c_st[ch["j"]] = decay * ch["caug"] + _dot(vw, ch["kb"])
        m_st[ch["j"]] = m_new


def _mlstm(q, k, vt, gt):
    B, T, d_a = q.shape
    L = CHUNK
    nc = T // L
    ng = gt.shape[1]
    fwd = lambda w: pl.BlockSpec((1, L, w), lambda b, c: (b, c, 0))
    bwd = lambda w: pl.BlockSpec((1, L, w), lambda b, c: (b, nc - 1 - c, 0))
    fwd_t = lambda r: pl.BlockSpec((1, r, L), lambda b, c: (b, 0, c))
    bwd_t = lambda r: pl.BlockSpec((1, r, L), lambda b, c: (b, 0, nc - 1 - c))
    in_specs = [fwd(d_a), fwd(d_a), fwd_t(d_a), fwd_t(ng), bwd(d_a), bwd(d_a), bwd_t(d_a), bwd_t(ng)]
    nch = 2 * N_HEADS
    return pl.pallas_call(
        _mlstm_kernel,
        out_shape=(jax.ShapeDtypeStruct((B, T, d_a), F32), jax.ShapeDtypeStruct((B, T, d_a), F32)),
        grid=(B, nc), in_specs=in_specs, out_specs=(fwd(d_a), bwd(d_a)),
        scratch_shapes=[pltpu.VMEM((nch, HEAD_DIM + SUBLANES, HEAD_DIM), F32),
                        pltpu.VMEM((nch, 1, 1), F32)],
        compiler_params=_cparams(("parallel", "arbitrary")),
        name="mlstm")(q, k, vt, gt, q, k, vt, gt)


def _mix_out_kernel(hf_ref, hb_ref, og_ref, u_ref, s_ref, x_ref, mnw_ref, sw_ref, sb_ref,
                    wout_ref, fnw_ref, rwt_ref, x1_ref, xn_ref, aff_ref):
    tm = x_ref.shape[0]
    d_a = og_ref.shape[1]
    h = hf_ref[...] + hb_ref[...]
    mnw = mnw_ref[...]
    parts = []
    for hd in range(d_a // HEAD_DIM):
        sl = slice(hd * HEAD_DIM, (hd + 1) * HEAD_DIM)
        parts.append(_rms(h[:, sl], mnw[:, sl]))
    a_out = (og_ref[...].astype(F32) * jnp.concatenate(parts, axis=1)).astype(BF16)

    sv = s_ref[...]
    sbias = sb_ref[...]
    rows = []
    for cc in range(tm // CHUNK):
        rs = slice(cc * CHUNK, (cc + 1) * CHUNK)
        cols = []
        for g in range(sv.shape[1] // HEAD_DIM):
            cs = slice(g * HEAD_DIM, (g + 1) * HEAD_DIM)
            cols.append(_dot(sw_ref[g], sv[rs, cs]))
        rows.append(jnp.concatenate(cols, axis=1) + sbias)
    gate = jnp.concatenate(rows, axis=0)
    b_out = (u_ref[...].astype(F32) * gate).astype(BF16)

    mix = jnp.concatenate([a_out, b_out], axis=1)
    x1 = x_ref[...] + _dot(mix, wout_ref[...])
    x1_ref[...] = x1
    xn = _rms(x1, fnw_ref[...])
    xn_ref[...] = xn
    logits = lax.dot_general(rwt_ref[...], xn, _NT, precision=lax.Precision.HIGHEST,
                             preferred_element_type=F32)
    mx = jnp.max(logits, axis=0, keepdims=True)
    ex = jnp.exp(logits - mx)
    aff = ex / jnp.sum(ex, axis=0, keepdims=True)
    for j in range(tm // LANES):
        aff_ref[j] = aff[:, j * LANES:(j + 1) * LANES]


def _mix_out(hf, hb, og, u, s, x, mnw, sw, sbias, wout, fnw, rwt, tm):
    N, D = x.shape
    d_a = og.shape[1]
    d_b = u.shape[1]
    E = rwt.shape[0]
    nt = N // tm
    full = lambda a: pl.BlockSpec(a.shape, lambda i: (0,) * a.ndim)
    tok = lambda w: pl.BlockSpec((tm, w), lambda i: (i, 0))
    in_specs = [tok(d_a), tok(d_a), tok(d_a), tok(d_b), tok(d_b), tok(D)] + [
        full(a) for a in (mnw, sw, sbias, wout, fnw, rwt)]
    out_shape = (jax.ShapeDtypeStruct((N, D), F32), jax.ShapeDtypeStruct((N, D), F32),
                 jax.ShapeDtypeStruct((N // LANES, E, LANES), F32))
    out_specs = (tok(D), tok(D), pl.BlockSpec((tm // LANES, E, LANES), lambda i: (i, 0, 0)))
    return pl.pallas_call(
        _mix_out_kernel, out_shape=out_shape, grid=(nt,), in_specs=in_specs, out_specs=out_specs,
        compiler_params=_cparams(("parallel",)), name="mix_out")(
            hf, hb, og, u, s, x, mnw, sw, sbias, wout, fnw, rwt)


def _select_kernel(aff_ref, posm_ref, off_ref, cnt_s, wi_s, *, cap):
    nb, E, _ = aff_ref.shape
    aff = aff_ref[...]

    def count_ge(cand):
        c = jnp.sum((aff >= cand).astype(I32), axis=0, keepdims=True)
        return jnp.sum(c, axis=2, keepdims=True)

    def bit_step(i, thr_bits):
        cand = thr_bits | jnp.left_shift(jnp.int32(1), 30 - i)
        return jnp.where(count_ge(pltpu.bitcast(cand, F32)) >= cap, cand, thr_bits)

    thr = pltpu.bitcast(lax.fori_loop(0, 31, bit_step, jnp.zeros((1, E, 1), I32)), F32)
    gt = aff > thr
    eq = aff == thr
    n_gt = jnp.sum(jnp.sum(gt.astype(I32), axis=0, keepdims=True), axis=2, keepdims=True)
    need = cap - n_gt

    li = lax.broadcasted_iota(I32, (LANES, LANES), 0)
    lj = lax.broadcasted_iota(I32, (LANES, LANES), 1)
    upper = (li < lj).astype(BF16)

    def excl_cumsum(flag):
        fb = flag.astype(BF16).reshape(nb * E, LANES)
        wi_s[...] = _dot(fb, upper).astype(I32).reshape(nb, E, LANES)
        cnt_s[...] = jnp.sum(flag.astype(I32), axis=2, keepdims=True)

        def blk(b, run):
            wi_s[b] = wi_s[b] + run
            return run + cnt_s[b]

        lax.fori_loop(0, nb, blk, jnp.zeros((E, 1), I32))
        return wi_s[...]

    eq_rank = excl_cumsum(eq)
    sel = gt | (eq & (eq_rank < need))
    pos = excl_cumsum(sel)
    posm_ref[...] = jnp.where(sel, pos, -1)
    off_ref[...] = jnp.broadcast_to(pos[:, :, 0:1], off_ref.shape)


def _select(aff3, cap):
    nb, E, _ = aff3.shape
    return pl.pallas_call(
        functools.partial(_select_kernel, cap=cap),
        out_shape=(jax.ShapeDtypeStruct((nb, E, LANES), I32), jax.ShapeDtypeStruct((nb, E, LANES), I32)),
        scratch_shapes=[pltpu.VMEM((nb, E, 1), I32), pltpu.VMEM((nb, E, LANES), I32)],
        compiler_params=_cparams(None), name="select")(aff3)


def _compact_kernel(off_sm, posm_ref, aff_ref, acc_ref):
    nb, E, _ = posm_ref.shape
    acc_ref[...] = jnp.zeros_like(acc_ref)
    srow = lax.broadcasted_iota(I32, (2 * LANES, LANES), 0)
    r8 = lax.broadcasted_iota(I32, (SUBLANES, LANES), 0)
    lane8 = lax.broadcasted_iota(I32, (SUBLANES, LANES), 1)

    def blk(b, carry):
        tok = b * LANES + lane8
        t_hi = jnp.right_shift(tok, 8).astype(F32)
        t_lo = jnp.bitwise_and(tok, 255).astype(F32)
        pm = posm_ref[b]
        af = aff_ref[b]
        for e in range(E):
            off = off_sm[b * E + e]
            j0 = jnp.right_shift(off, 7)
            rel = pm[e:e + 1, :] - j0 * LANES
            onehot = (srow == rel).astype(BF16)
            a = af[e:e + 1, :]
            a0 = a.astype(BF16)
            r1 = a - a0.astype(F32)
            a1 = r1.astype(BF16)
            a2 = (r1 - a1.astype(F32)).astype(BF16)
            lhs = jnp.where(r8 == 0, t_hi, jnp.where(r8 == 1, t_lo, 0.0))
            lhs = jnp.where(r8 == 2, a0.astype(F32), lhs)
            lhs = jnp.where(r8 == 3, a1.astype(F32), lhs)
            lhs = jnp.where(r8 == 4, a2.astype(F32), lhs).astype(BF16)
            out = lax.dot_general(lhs, onehot, _NT, preferred_element_type=F32)
            acc_ref[e, j0] = acc_ref[e, j0] + out[:, :LANES]
            acc_ref[e, j0 + 1] = acc_ref[e, j0 + 1] + out[:, LANES:]
        return carry

    lax.fori_loop(0, nb, blk, 0)


def _compact(off_flat, posm3, aff3, cap):
    nb, E, _ = posm3.shape
    nt_pad = cap // LANES + 2
    gs = pltpu.PrefetchScalarGridSpec(
        num_scalar_prefetch=1, grid=(1,),
        in_specs=[pl.BlockSpec(posm3.shape, lambda i, o: (0, 0, 0)),
                  pl.BlockSpec(aff3.shape, lambda i, o: (0, 0, 0))],
        out_specs=pl.BlockSpec((E, nt_pad, SUBLANES, LANES), lambda i, o: (0, 0, 0, 0)))
    return pl.pallas_call(
        _compact_kernel, out_shape=jax.ShapeDtypeStruct((E, nt_pad, SUBLANES, LANES), F32),
        grid_spec=gs, compiler_params=_cparams(("arbitrary",)), name="compact")(off_flat, posm3, aff3)


def _ffn_kernel(idc_sm, idn_sm, xn_hbm, val_ref, wg_ref, wu_ref, wd_ref, y_ref, xbuf, sem,
                *, n_real, fc):
    s = pl.program_id(0)
    ts = xbuf.shape[1]
    slot = lax.rem(s, 2)

    def row_copy(idx_sm, r, dst_slot):
        return pltpu.make_async_copy(xn_hbm.at[pl.ds(idx_sm[0, 0, r], 1), :],
                                     xbuf.at[dst_slot, pl.ds(r, 1), :], sem.at[dst_slot])

    @pl.when(s == 0)
    def _():
        def body(r, carry):
            row_copy(idc_sm, r, 0).start()
            return carry
        lax.fori_loop(0, ts, body, 0, unroll=8)

    pltpu.make_async_copy(xn_hbm.at[pl.ds(0, ts), :], xbuf.at[slot], sem.at[slot]).wait()

    @pl.when(s < n_real)
    def _():
        x = xbuf[slot].astype(BF16)
        F = wg_ref.shape[2]
        nchunk = F // fc
        rows_per_chunk = ts // nchunk
        acc = jnp.zeros((ts, wd_ref.shape[2]), F32)
        for c in range(nchunk):
            for r in range(c * rows_per_chunk, (c + 1) * rows_per_chunk):
                row_copy(idn_sm, r, 1 - slot).start()
            cs = slice(c * fc, (c + 1) * fc)
            gte = _dot(x, wg_ref[0, :, cs])
            up = _dot(x, wu_ref[0, :, cs])
            hid = (gte * _sigmoid(gte) * up).astype(BF16)
            acc = acc + _dot(hid, wd_ref[0, cs, :])
        y_ref[...] = (acc * val_ref[...]).astype(y_ref.dtype)

    @pl.when(s >= n_real)
    def _():
        y_ref[...] = jnp.zeros_like(y_ref)


def _ffn(idx3, xn, vals2, wg, wu, wd, cap, ts):
    E, D, F = wg.shape
    nts = cap // ts
    n_real = E * nts
    last = n_real - 1
    eidx = lambda s: jnp.minimum(s // nts, E - 1)
    in_specs = [
        pl.BlockSpec((1, 1, ts), lambda s: (jnp.minimum(s, last), 0, 0), memory_space=pltpu.SMEM),
        pl.BlockSpec((1, 1, ts), lambda s: (jnp.minimum(s + 1, last), 0, 0), memory_space=pltpu.SMEM),
        pl.BlockSpec(memory_space=pl.ANY),
        pl.BlockSpec((ts, 1), lambda s: (jnp.minimum(s, last), 0)),
        pl.BlockSpec((1, D, F), lambda s: (eidx(s), 0, 0)),
        pl.BlockSpec((1, D, F), lambda s: (eidx(s), 0, 0)),
        pl.BlockSpec((1, F, D), lambda s: (eidx(s), 0, 0)),
    ]
    return pl.pallas_call(
        functools.partial(_ffn_kernel, n_real=n_real, fc=512),
        out_shape=jax.ShapeDtypeStruct(((n_real + 1) * ts, D), BF16),
        grid=(n_real + 1,), in_specs=in_specs,
        out_specs=pl.BlockSpec((ts, D), lambda s: (s, 0)),
        scratch_shapes=[pltpu.VMEM((2, ts, D), F32), pltpu.SemaphoreType.DMA((2,))],
        compiler_params=_cparams(("arbitrary",)), name="ffn")(idx3, idx3, xn, vals2, wg, wu, wd)


WIN = 64


def _combine_kernel(off_sm, posm_ref, x1_ref, y_hbm, fnw_ref, o_ref, ycat, yext, sem, sem_ext,
                    *, cap, y_rows, nblk):
    b = pl.program_id(0)
    slot = lax.rem(b, 2)
    nsub, E, _ = posm_ref.shape
    tb = x1_ref.shape[0]
    pm = jnp.concatenate([posm_ref[j] for j in range(nsub)], axis=1)
    wrow = lax.broadcasted_iota(I32, (WIN, tb), 0)

    def starts_of(blk, r):
        out = []
        for e in range(E):
            base = jnp.left_shift(jnp.right_shift(off_sm[blk * E + e], 4), 4)
            st = jnp.minimum(e * cap + base + r * WIN, y_rows - WIN)
            out.append(pl.multiple_of(st, BF16_ROWS))
        return out

    def copies(starts, dst, dsem):
        return [pltpu.make_async_copy(y_hbm.at[pl.ds(starts[e], WIN), :], dst.at[pl.ds(e * WIN, WIN), :], dsem)
                for e in range(E)]

    def onehot(starts):
        ps = []
        for e in range(E):
            pe = pm[e:e + 1, :]
            rel = jnp.where(pe >= 0, pe + (e * cap - starts[e]), -1)
            ps.append((wrow == rel).astype(BF16))
        return jnp.concatenate(ps, axis=0)

    @pl.when(b == 0)
    def _():
        for cp in copies(starts_of(b, 0), ycat.at[0], sem.at[0]):
            cp.start()

    @pl.when(b + 1 < nblk)
    def _():
        for cp in copies(starts_of(b + 1, 0), ycat.at[1 - slot], sem.at[1 - slot]):
            cp.start()

    starts0 = starts_of(b, 0)
    p0 = onehot(starts0)
    for cp in copies(starts0, ycat.at[slot], sem.at[slot]):
        cp.wait()
    acc0 = lax.dot_general(p0, ycat[slot], _TN, preferred_element_type=F32)

    nrounds = jnp.int32(1)
    for e in range(E):
        base = jnp.left_shift(jnp.right_shift(off_sm[b * E + e], 4), 4)
        nrounds = jnp.maximum(nrounds, jnp.right_shift(off_sm[(b + 1) * E + e] - base + (WIN - 1), 6))

    def round_body(r, acc):
        starts = starts_of(b, r)
        cps = copies(starts, yext, sem_ext)
        for cp in cps:
            cp.start()
        p = onehot(starts)
        for cp in cps:
            cp.wait()
        return acc + lax.dot_general(p, yext[...], _TN, preferred_element_type=F32)

    acc = lax.fori_loop(1, nrounds, round_body, acc0)
    o_ref[...] = _rms(x1_ref[...] + acc, fnw_ref[...])


def _combine(off_flat, posm3, x1, y, fnw, cap, tb):
    N, D = x1.shape
    nb, E, _ = posm3.shape
    nsub = tb // LANES
    y_rows = y.shape[0]
    gs = pltpu.PrefetchScalarGridSpec(
        num_scalar_prefetch=1, grid=(N // tb,),
        in_specs=[pl.BlockSpec((nsub, E, LANES), lambda i, o: (i, 0, 0)),
                  pl.BlockSpec((tb, D), lambda i, o: (i, 0)),
                  pl.BlockSpec(memory_space=pl.ANY),
                  pl.BlockSpec(fnw.shape, lambda i, o: (0, 0))],
        out_specs=pl.BlockSpec((tb, D), lambda i, o: (i, 0)),
        scratch_shapes=[pltpu.VMEM((2, E * WIN, D), BF16), pltpu.VMEM((E * WIN, D), BF16),
                        pltpu.SemaphoreType.DMA((2,)), pltpu.SemaphoreType.DMA])
    return pl.pallas_call(
        functools.partial(_combine_kernel, cap=cap, y_rows=y_rows, nblk=N // tb),
        out_shape=jax.ShapeDtypeStruct((N, D), F32), grid_spec=gs,
        compiler_params=_cparams(("arbitrary",)), name="combine")(off_flat, posm3, x1, y, fnw)


def _prep_params(norm_mix_w, w_in, conv_w, conv_b, gate_b, mlstm_norm_w, sgu_norm_w, sgu_w, sgu_b,
                 w_out, norm_ffn_w, router_w, w_gate, w_up, w_down, norm_final_w):
    d_a = mlstm_norm_w.shape[1]
    d_b = sgu_norm_w.shape[1]
    ng = gate_b.shape[1]
    w = w_in[0]
    o0, o1, o2, o3, o4, o5 = 2 * d_a, 3 * d_a, 4 * d_a, 4 * d_a + ng, 4 * d_a + ng + d_b, 4 * d_a + ng + 2 * d_b
    wg = w[:, o2:o3]
    return dict(
        nw=norm_mix_w[0][None, :],
        wqk=w[:, :o0].astype(BF16), wvt=w[:, o0:o1].T.astype(BF16), wo=w[:, o1:o2].astype(BF16),
        wgt=wg.T.astype(BF16),
        wu=w[:, o3:o4].astype(BF16), ws=w[:, o4:o5].astype(BF16),
        cw=conv_w[0], cb=conv_b[0][None, :], gbt=gate_b[0][:, None],
        snw=sgu_norm_w[0][None, :], mnw=mlstm_norm_w[0][None, :],
        sw=sgu_w[0].astype(BF16),
        sbias=jnp.repeat(sgu_b[0].T, HEAD_DIM, axis=1),
        wout=w_out[0].astype(BF16), fnw=norm_ffn_w[0][None, :], rwt=router_w[0].T,
        wgate=w_gate[0].astype(BF16), wup=w_up[0].astype(BF16), wdown=w_down[0].astype(BF16),
        nfw=norm_final_w[None, :],
    )


def _encoder(x, p):
    B, T, D = x.shape
    N = B * T
    E = N_EXPERTS
    cap = (N * CAPACITY_FACTOR) // E
    tm1 = min(512, T)
    q, k, vt, og, gt, u, s = _in_proj(
        x, p["nw"], p["wqk"], p["wvt"], p["wo"], p["wgt"], p["wu"], p["ws"],
        p["cw"], p["cb"], p["gbt"], p["snw"], tm1)
    hf, hb = _mlstm(q, k, vt, gt)
    flat = lambda a: a.reshape(N, a.shape[-1])
    x1, xn, aff3 = _mix_out(flat(hf), flat(hb), flat(og), flat(u), flat(s), flat(x),
                            p["mnw"], p["sw"], p["sbias"], p["wout"], p["fnw"], p["rwt"], min(256, N))
    posm3, off3 = _select(aff3, cap)
    nb = N // LANES
    off_flat = off3[:, :, 0].reshape(nb * E)
    acc = _compact(off_flat, posm3, aff3, cap)
    nt = cap // LANES
    idx = (acc[:, :nt, 0, :] * 256.0 + acc[:, :nt, 1, :]).astype(I32).reshape(E * cap)
    vals = (acc[:, :nt, 2, :] + acc[:, :nt, 3, :] + acc[:, :nt, 4, :]).reshape(E * cap, 1)
    ts = min(256, cap)
    y = _ffn(idx.reshape(E * cap // ts, 1, ts), xn, vals, p["wgate"], p["wup"], p["wdown"], cap, ts)
    tb = min(256, N)
    sub = tb // LANES
    off_tb = jnp.concatenate([off3[::sub, :, 0], jnp.full((1, E), cap, I32)], axis=0).reshape(-1)
    out = _combine(off_tb, posm3, x1, y, p["nfw"], cap, tb)
    return out.reshape(B, T, D)


def kernel(x_prompt, x_sample, norm_mix_w, w_in, conv_w, conv_b, gate_b, mlstm_norm_w, sgu_norm_w,
           sgu_w, sgu_b, w_out, norm_ffn_w, router_w, w_gate, w_up, w_down, norm_final_w):
    p = _prep_params(norm_mix_w, w_in, conv_w, conv_b, gate_b, mlstm_norm_w, sgu_norm_w, sgu_w,
                     sgu_b, w_out, norm_ffn_w, router_w, w_gate, w_up, w_down, norm_final_w)
    return (_encoder(x_prompt, p), _encoder(x_sample, p))
```

```python
import functools
import math

import jax
import jax.numpy as jnp
from jax import lax
from jax.experimental import pallas as pl
from jax.experimental.pallas import tpu as pltpu

F32 = jnp.float32
BF16 = jnp.bfloat16
I32 = jnp.int32

EPS = 1e-6
N_HEADS = 4
HEAD_DIM = 128
CHUNK = 128
N_EXPERTS = 16
CAPACITY_FACTOR = 2
LANES = 128
SUBLANES = 8
BF16_ROWS = 16
NEG_BIG = -1e30
VMEM_LIMIT = 48 * 1024 * 1024

_NT = (((1,), (1,)), ((), ()))
_TN = (((0,), (0,)), ((), ()))


def _cparams(sem, vmem=VMEM_LIMIT):
    return pltpu.CompilerParams(dimension_semantics=sem, vmem_limit_bytes=vmem)


def _dot(a, b):
    return jnp.dot(a, b, preferred_element_type=F32)


def _sigmoid(x):
    return 1.0 / (1.0 + jnp.exp(-x))


def _gelu(x):
    return 0.5 * x * (1.0 + lax.erf(x * (1.0 / math.sqrt(2.0))))


def _rms(x, w):
    ms = jnp.mean(x * x, axis=-1, keepdims=True)
    return x * lax.rsqrt(ms + EPS) * w


def _in_proj_kernel(x_ref, xp_ref, xn_ref, nw_ref, wqk_ref, wvt_ref, wo_ref, wgt_ref,
                    wu_ref, ws_ref, cw_ref, cb_ref, gbt_ref, snw_ref,
                    q_ref, k_ref, vt_ref, og_ref, gt_ref, u_ref, s_ref):
    i = pl.program_id(1)
    n_i = pl.num_programs(1)
    tm = x_ref.shape[1]
    d_a = q_ref.shape[2]
    nw = nw_ref[...]
    hb = _rms(x_ref[0], nw).astype(BF16)
    hp = _rms(xp_ref[0], nw).astype(BF16)
    hn = _rms(xn_ref[0], nw).astype(BF16)

    wqk = wqk_ref[...]
    z = _dot(hb, wqk)
    zp = _dot(hp, wqk)[SUBLANES - 1:SUBLANES, :]
    zn = _dot(hn, wqk)[0:1, :]
    zp = jnp.where(i == 0, 0.0, zp)
    zn = jnp.where(i == n_i - 1, 0.0, zn)
    row = lax.broadcasted_iota(I32, z.shape, 0)
    z_prev = jnp.where(row == 0, zp, pltpu.roll(z, 1, axis=0))
    z_next = jnp.where(row == tm - 1, zn, pltpu.roll(z, tm - 1, axis=0))
    cw = cw_ref[...]
    conv = cb_ref[...] + cw[0:1, :] * z_prev + cw[1:2, :] * z + cw[2:3, :] * z_next
    qk = conv * _sigmoid(conv)
    q_ref[0] = qk[:, :d_a].astype(BF16)
    k_ref[0] = (qk[:, d_a:] * (1.0 / math.sqrt(HEAD_DIM))).astype(BF16)

    vt_ref[0] = lax.dot_general(wvt_ref[...], hb, _NT, preferred_element_type=F32).astype(BF16)
    og_ref[0] = _sigmoid(_dot(hb, wo_ref[...])).astype(BF16)

    zgt = lax.dot_general(wgt_ref[...], hb, _NT, preferred_element_type=F32) + gbt_ref[...]
    rowt = lax.broadcasted_iota(I32, zgt.shape, 0)
    gt_ref[0] = jnp.where(rowt < 2 * N_HEADS, zgt, jax.nn.log_sigmoid(zgt))

    u_ref[0] = _gelu(_dot(hb, wu_ref[...])).astype(BF16)
    sv = _gelu(_dot(hb, ws_ref[...]))
    snw = snw_ref[...]
    parts = []
    for g in range(sv.shape[1] // HEAD_DIM):
        sl = slice(g * HEAD_DIM, (g + 1) * HEAD_DIM)
        parts.append(_rms(sv[:, sl], snw[:, sl]))
    s_ref[0] = jnp.concatenate(parts, axis=1).astype(BF16)


def _in_proj(x, nw, wqk, wvt, wo, wgt, wu, ws, cw, cb, gbt, snw, tm):
    B, T, D = x.shape
    d_a = wvt.shape[0]
    d_b = wu.shape[1]
    ng = wgt.shape[0]
    nt = T // tm
    hb8 = tm // SUBLANES
    last8 = T // SUBLANES - 1
    full = lambda a: pl.BlockSpec(a.shape, lambda b, i: (0,) * a.ndim)
    tok = lambda w: pl.BlockSpec((1, tm, w), lambda b, i: (b, i, 0))
    in_specs = [
        pl.BlockSpec((1, tm, D), lambda b, i: (b, i, 0)),
        pl.BlockSpec((1, SUBLANES, D), lambda b, i: (b, jnp.maximum(i * hb8 - 1, 0), 0)),
        pl.BlockSpec((1, SUBLANES, D), lambda b, i: (b, jnp.minimum((i + 1) * hb8, last8), 0)),
    ] + [full(a) for a in (nw, wqk, wvt, wo, wgt, wu, ws, cw, cb, gbt, snw)]
    out_shape = (
        jax.ShapeDtypeStruct((B, T, d_a), BF16), jax.ShapeDtypeStruct((B, T, d_a), BF16),
        jax.ShapeDtypeStruct((B, d_a, T), BF16), jax.ShapeDtypeStruct((B, T, d_a), BF16),
        jax.ShapeDtypeStruct((B, ng, T), F32),
        jax.ShapeDtypeStruct((B, T, d_b), BF16), jax.ShapeDtypeStruct((B, T, d_b), BF16),
    )
    out_specs = (tok(d_a), tok(d_a), pl.BlockSpec((1, d_a, tm), lambda b, i: (b, 0, i)), tok(d_a),
                 pl.BlockSpec((1, ng, tm), lambda b, i: (b, 0, i)), tok(d_b), tok(d_b))
    return pl.pallas_call(
        _in_proj_kernel, out_shape=out_shape, grid=(B, nt), in_specs=in_specs,
        out_specs=out_specs, compiler_params=_cparams(("parallel", "arbitrary")),
        name="in_proj")(x, x, x, nw, wqk, wvt, wo, wgt, wu, ws, cw, cb, gbt, snw)


def _split3(x):
    x0 = x.astype(BF16)
    r1 = x - x0.astype(F32)
    x1 = r1.astype(BF16)
    x2 = (r1 - x1.astype(F32)).astype(BF16)
    return x0, x1, x2


def _mlstm_kernel(qf_ref, kf_ref, vtf_ref, gtf_ref, qb_ref, kb_ref, vtb_ref, gtb_ref,
                  hf_ref, hb_ref, c_st, m_st):
    c = pl.program_id(1)
    L = qf_ref.shape[1]
    d = HEAD_DIM

    @pl.when(c == 0)
    def _():
        c_st[...] = jnp.zeros_like(c_st)
        m_st[...] = jnp.zeros_like(m_st)

    r0 = lax.broadcasted_iota(I32, (L, L), 0)
    r1 = lax.broadcasted_iota(I32, (L, L), 1)
    ones8 = jnp.ones((SUBLANES, L), BF16)
    ng = 2 * N_HEADS
    dirs = ((qf_ref, kf_ref, vtf_ref, gtf_ref[0], r0 <= r1, hf_ref),
            (qb_ref, kb_ref, vtb_ref, gtb_ref[0], r0 >= r1, hb_ref))

    chains = []
    for dr, (q_ref, k_ref, vt_ref, gt, mask_st, h_ref) in enumerate(dirs):
        g3 = jnp.concatenate(_split3(gt), axis=0)
        b3 = _dot(g3, mask_st.astype(BF16))
        nr = gt.shape[0]
        br_all = b3[:nr] + b3[nr:2 * nr] + b3[2 * nr:]
        a_rows = gt[:ng] - br_all[ng:]
        a_cols = jnp.concatenate([a_rows, jnp.zeros((L - ng, L), F32)], axis=0).T
        for hd in range(N_HEADS):
            j = dr * N_HEADS + hd
            hs = slice(hd * d, (hd + 1) * d)
            chains.append(dict(
                j=j, hs=hs, h_ref=h_ref, mask=mask_st, qb=q_ref[0, :, hs], kb=k_ref[0, :, hs],
                vt_aug=jnp.concatenate([vt_ref[0, hs, :], ones8], axis=0),
                a_col=a_cols[:, j:j + 1], i_row=gt[j:j + 1, :], b_row=br_all[ng + j:ng + j + 1, :],
                b_tot=jnp.sum(gt[ng + j:ng + j + 1, :], axis=1, keepdims=True),
                caug=c_st[j], m_prev=m_st[j]))

    for ch in chains:
        ch["s_raw"] = lax.dot_general(ch["kb"], ch["qb"], _NT, preferred_element_type=F32)
        ch["ia"] = lax.dot_general(ch["caug"].astype(BF16), ch["qb"], _NT, preferred_element_type=F32)
    for ch in chains:
        dmat = jnp.where(ch["mask"], ch["a_col"] + ch["b_row"], NEG_BIG)
        inter = ch["b_row"] + ch["m_prev"]
        m_t = jnp.maximum(jnp.max(dmat, axis=0, keepdims=True), inter)
        st = ch["s_raw"] * jnp.exp(dmat - m_t)
        w_inter = jnp.exp(inter - m_t)
        den = jnp.sum(st, axis=0, keepdims=True) + w_inter * ch["ia"][d:d + 1]
        ch["st"] = st.astype(BF16)
        ch["w_inter"] = w_inter
        ch["rden"] = 1.0 / jnp.maximum(jnp.abs(den), jnp.exp(-m_t))
    for ch in chains:
        num = _dot(ch["vt_aug"][:d], ch["st"]) + ch["w_inter"] * ch["ia"][:d]
        ch["h_ref"][0, :, ch["hs"]] = (num * ch["rden"]).T
    for ch in chains:
        g_row = ch["b_tot"] - ch["b_row"] + ch["i_row"]
        m_new = jnp.maximum(ch["b_tot"] + ch["m_prev"], jnp.max(g_row, axis=1, keepdims=True))
        wk = jnp.exp(g_row - m_new)
        decay = jnp.exp(ch["b_tot"] + ch["m_prev"] - m_new)
        vw = (ch["vt_aug"].astype(F32) * wk).astype(BF16)
        c_st[ch["j"]] = decay * ch["caug"] + _dot(vw, ch["kb"])
        m_st[ch["j"]] = m_new


def _mlstm(q, k, vt, gt):
    B, T, d_a = q.shape
    L = CHUNK
    nc = T // L
    ng = gt.shape[1]
    fwd = lambda w: pl.BlockSpec((1, L, w), lambda b, c: (b, c, 0))
    bwd = lambda w: pl.BlockSpec((1, L, w), lambda b, c: (b, nc - 1 - c, 0))
    fwd_t = lambda r: pl.BlockSpec((1, r, L), lambda b, c: (b, 0, c))
    bwd_t = lambda r: pl.BlockSpec((1, r, L), lambda b, c: (b, 0, nc - 1 - c))
    in_specs = [fwd(d_a), fwd(d_a), fwd_t(d_a), fwd_t(ng), bwd(d_a), bwd(d_a), bwd_t(d_a), bwd_t(ng)]
    nch = 2 * N_HEADS
    return pl.pallas_call(
        _mlstm_kernel,
        out_shape=(jax.ShapeDtypeStruct((B, T, d_a), F32), jax.ShapeDtypeStruct((B, T, d_a), F32)),
        grid=(B, nc), in_specs=in_specs, out_specs=(fwd(d_a), bwd(d_a)),
        scratch_shapes=[pltpu.VMEM((nch, HEAD_DIM + SUBLANES, HEAD_DIM), F32),
                        pltpu.VMEM((nch, 1, 1), F32)],
        compiler_params=_cparams(("parallel", "arbitrary")),
        name="mlstm")(q, k, vt, gt, q, k, vt, gt)


def _mix_out_kernel(hf_ref, hb_ref, og_ref, u_ref, s_ref, x_ref, mnw_ref, sw_ref, sb_ref,
                    wout_ref, fnw_ref, rwt_ref, x1_ref, xn_ref, aff_ref):
    tm = x_ref.shape[0]
    d_a = og_ref.shape[1]
    h = hf_ref[...] + hb_ref[...]
    mnw = mnw_ref[...]
    parts = []
    for hd in range(d_a // HEAD_DIM):
        sl = slice(hd * HEAD_DIM, (hd + 1) * HEAD_DIM)
        parts.append(_rms(h[:, sl], mnw[:, sl]))
    a_out = (og_ref[...].astype(F32) * jnp.concatenate(parts, axis=1)).astype(BF16)

    sv = s_ref[...]
    sbias = sb_ref[...]
    rows = []
    for cc in range(tm // CHUNK):
        rs = slice(cc * CHUNK, (cc + 1) * CHUNK)
        cols = []
        for g in range(sv.shape[1] // HEAD_DIM):
            cs = slice(g * HEAD_DIM, (g + 1) * HEAD_DIM)
            cols.append(_dot(sw_ref[g], sv[rs, cs]))
        rows.append(jnp.concatenate(cols, axis=1) + sbias)
    gate = jnp.concatenate(rows, axis=0)
    b_out = (u_ref[...].astype(F32) * gate).astype(BF16)

    mix = jnp.concatenate([a_out, b_out], axis=1)
    x1 = x_ref[...] + _dot(mix, wout_ref[...])
    x1_ref[...] = x1
    xn = _rms(x1, fnw_ref[...])
    chunks = xn.shape[1] // LANES
    for j in range(chunks):
        xn_ref[pl.ds(j, tm, stride=chunks), :] = xn[:, j * LANES:(j + 1) * LANES]
    logits = lax.dot_general(rwt_ref[...], xn, _NT, precision=lax.Precision.HIGHEST,
                             preferred_element_type=F32)
    mx = jnp.max(logits, axis=0, keepdims=True)
    ex = jnp.exp(logits - mx)
    aff = ex / jnp.sum(ex, axis=0, keepdims=True)
    for j in range(tm // LANES):
        aff_ref[j] = aff[:, j * LANES:(j + 1) * LANES]


def _mix_out(hf, hb, og, u, s, x, mnw, sw, sbias, wout, fnw, rwt, tm):
    N, D = x.shape
    d_a = og.shape[1]
    d_b = u.shape[1]
    E = rwt.shape[0]
    nt = N // tm
    full = lambda a: pl.BlockSpec(a.shape, lambda i: (0,) * a.ndim)
    tok = lambda w: pl.BlockSpec((tm, w), lambda i: (i, 0))
    in_specs = [tok(d_a), tok(d_a), tok(d_a), tok(d_b), tok(d_b), tok(D)] + [
        full(a) for a in (mnw, sw, sbias, wout, fnw, rwt)]
    chunks = D // LANES
    out_shape = (jax.ShapeDtypeStruct((N, D), F32), jax.ShapeDtypeStruct((N * chunks, LANES), F32),
                 jax.ShapeDtypeStruct((N // LANES, E, LANES), F32))
    out_specs = (tok(D), pl.BlockSpec((tm * chunks, LANES), lambda i: (i, 0)),
                 pl.BlockSpec((tm // LANES, E, LANES), lambda i: (i, 0, 0)))
    return pl.pallas_call(
        _mix_out_kernel, out_shape=out_shape, grid=(nt,), in_specs=in_specs, out_specs=out_specs,
        compiler_params=_cparams(("parallel",)), name="mix_out")(
            hf, hb, og, u, s, x, mnw, sw, sbias, wout, fnw, rwt)


def _select_kernel(aff_ref, posm_ref, off_ref, cnt_s, wi_s, *, cap):
    nb, E, _ = aff_ref.shape
    aff = aff_ref[...]

    def count_ge(cand):
        c = jnp.sum((aff >= cand).astype(I32), axis=0, keepdims=True)
        return jnp.sum(c, axis=2, keepdims=True)

    def bit_step(i, thr_bits):
        cand = thr_bits | jnp.left_shift(jnp.int32(1), 30 - i)
        return jnp.where(count_ge(pltpu.bitcast(cand, F32)) >= cap, cand, thr_bits)

    thr = pltpu.bitcast(lax.fori_loop(0, 31, bit_step, jnp.zeros((1, E, 1), I32)), F32)
    gt = aff > thr
    eq = aff == thr
    n_gt = jnp.sum(jnp.sum(gt.astype(I32), axis=0, keepdims=True), axis=2, keepdims=True)
    need = cap - n_gt

    li = lax.broadcasted_iota(I32, (LANES, LANES), 0)
    lj = lax.broadcasted_iota(I32, (LANES, LANES), 1)
    upper = (li < lj).astype(BF16)

    def excl_cumsum(flag):
        fb = flag.astype(BF16).reshape(nb * E, LANES)
        wi_s[...] = _dot(fb, upper).astype(I32).reshape(nb, E, LANES)
        cnt_s[...] = jnp.sum(flag.astype(I32), axis=2, keepdims=True)

        def blk(b, run):
            wi_s[b] = wi_s[b] + run
            return run + cnt_s[b]

        lax.fori_loop(0, nb, blk, jnp.zeros((E, 1), I32))
        return wi_s[...]

    eq_rank = excl_cumsum(eq)
    sel = gt | (eq & (eq_rank < need))
    pos = excl_cumsum(sel)
    posm_ref[...] = jnp.where(sel, pos, -1)
    off_ref[...] = jnp.broadcast_to(pos[:, :, 0:1], off_ref.shape)


def _select(aff3, cap):
    nb, E, _ = aff3.shape
    return pl.pallas_call(
        functools.partial(_select_kernel, cap=cap),
        out_shape=(jax.ShapeDtypeStruct((nb, E, LANES), I32), jax.ShapeDtypeStruct((nb, E, LANES), I32)),
        scratch_shapes=[pltpu.VMEM((nb, E, 1), I32), pltpu.VMEM((nb, E, LANES), I32)],
        compiler_params=_cparams(None), name="select")(aff3)


def _compact_kernel(off_sm, posm_ref, aff_ref, acc_ref):
    nb, E, _ = posm_ref.shape
    acc_ref[...] = jnp.zeros_like(acc_ref)
    srow = lax.broadcasted_iota(I32, (2 * LANES, LANES), 0)
    r8 = lax.broadcasted_iota(I32, (SUBLANES, LANES), 0)
    lane8 = lax.broadcasted_iota(I32, (SUBLANES, LANES), 1)

    def blk(b, carry):
        tok = b * LANES + lane8
        t_hi = jnp.right_shift(tok, 8).astype(F32)
        t_lo = jnp.bitwise_and(tok, 255).astype(F32)
        pm = posm_ref[b]
        af = aff_ref[b]
        for e in range(E):
            off = off_sm[b * E + e]
            j0 = jnp.right_shift(off, 7)
            rel = pm[e:e + 1, :] - j0 * LANES
            onehot = (srow == rel).astype(BF16)
            a = af[e:e + 1, :]
            a0 = a.astype(BF16)
            r1 = a - a0.astype(F32)
            a1 = r1.astype(BF16)
            a2 = (r1 - a1.astype(F32)).astype(BF16)
            lhs = jnp.where(r8 == 0, t_hi, jnp.where(r8 == 1, t_lo, 0.0))
            lhs = jnp.where(r8 == 2, a0.astype(F32), lhs)
            lhs = jnp.where(r8 == 3, a1.astype(F32), lhs)
            lhs = jnp.where(r8 == 4, a2.astype(F32), lhs).astype(BF16)
            out = lax.dot_general(lhs, onehot, _NT, preferred_element_type=F32)
            acc_ref[e, j0] = acc_ref[e, j0] + out[:, :LANES]
            acc_ref[e, j0 + 1] = acc_ref[e, j0 + 1] + out[:, LANES:]
        return carry

    lax.fori_loop(0, nb, blk, 0)


def _compact(off_flat, posm3, aff3, cap):
    nb, E, _ = posm3.shape
    nt_pad = cap // LANES + 2
    gs = pltpu.PrefetchScalarGridSpec(
        num_scalar_prefetch=1, grid=(1,),
        in_specs=[pl.BlockSpec(posm3.shape, lambda i, o: (0, 0, 0)),
                  pl.BlockSpec(aff3.shape, lambda i, o: (0, 0, 0))],
        out_specs=pl.BlockSpec((E, nt_pad, SUBLANES, LANES), lambda i, o: (0, 0, 0, 0)))
    return pl.pallas_call(
        _compact_kernel, out_shape=jax.ShapeDtypeStruct((E, nt_pad, SUBLANES, LANES), F32),
        grid_spec=gs, compiler_params=_cparams(("arbitrary",)), name="compact")(off_flat, posm3, aff3)


def _ffn_kernel(idc_sm, idn_sm, xn_hbm, val_ref, wg_ref, wu_ref, wd_ref, y_ref, xbuf, sem,
                *, n_real, fc):
    s = pl.program_id(0)
    ts = y_ref.shape[0]
    chunks = xbuf.shape[1] // ts
    slot = lax.rem(s, 2)

    def row_copy(idx_sm, r, dst_slot):
        src = pl.multiple_of(idx_sm[0, 0, r] * chunks, chunks)
        return pltpu.make_async_copy(xn_hbm.at[pl.ds(src, chunks), :],
                                     xbuf.at[dst_slot, pl.ds(r * chunks, chunks), :], sem.at[dst_slot])

    @pl.when(s == 0)
    def _():
        def body(r, carry):
            row_copy(idc_sm, r, 0).start()
            return carry
        lax.fori_loop(0, ts, body, 0, unroll=8)

    pltpu.make_async_copy(xn_hbm.at[pl.ds(0, ts * chunks), :], xbuf.at[slot], sem.at[slot]).wait()

    @pl.when(s < n_real)
    def _():
        xs = xbuf.at[slot]
        x = jnp.concatenate([xs[pl.ds(j, ts, stride=chunks), :] for j in range(chunks)],
                            axis=1).astype(BF16)
        F = wg_ref.shape[2]
        nchunk = F // fc
        rows_per_chunk = ts // nchunk
        acc = jnp.zeros((ts, wd_ref.shape[2]), F32)
        for c in range(nchunk):
            for r in range(c * rows_per_chunk, (c + 1) * rows_per_chunk):
                row_copy(idn_sm, r, 1 - slot).start()
            cs = slice(c * fc, (c + 1) * fc)
            gte = _dot(x, wg_ref[0, :, cs])
            up = _dot(x, wu_ref[0, :, cs])
            hid = (gte * _sigmoid(gte) * up).astype(BF16)
            acc = acc + _dot(hid, wd_ref[0, cs, :])
        y_ref[...] = (acc * val_ref[...]).astype(y_ref.dtype)

    @pl.when(s >= n_real)
    def _():
        y_ref[...] = jnp.zeros_like(y_ref)


def _ffn(idx3, xn, vals2, wg, wu, wd, cap, ts):
    E, D, F = wg.shape
    nts = cap // ts
    n_real = E * nts
    last = n_real - 1
    eidx = lambda s: jnp.minimum(s // nts, E - 1)
    in_specs = [
        pl.BlockSpec((1, 1, ts), lambda s: (jnp.minimum(s, last), 0, 0), memory_space=pltpu.SMEM),
        pl.BlockSpec((1, 1, ts), lambda s: (jnp.minimum(s + 1, last), 0, 0), memory_space=pltpu.SMEM),
        pl.BlockSpec(memory_space=pl.ANY),
        pl.BlockSpec((ts, 1), lambda s: (jnp.minimum(s, last), 0)),
        pl.BlockSpec((1, D, F), lambda s: (eidx(s), 0, 0)),
        pl.BlockSpec((1, D, F), lambda s: (eidx(s), 0, 0)),
        pl.BlockSpec((1, F, D), lambda s: (eidx(s), 0, 0)),
    ]
    return pl.pallas_call(
        functools.partial(_ffn_kernel, n_real=n_real, fc=512),
        out_shape=jax.ShapeDtypeStruct(((n_real + 1) * ts, D), BF16),
        grid=(n_real + 1,), in_specs=in_specs,
        out_specs=pl.BlockSpec((ts, D), lambda s: (s, 0)),
        scratch_shapes=[pltpu.VMEM((2, ts * (D // LANES), LANES), F32), pltpu.SemaphoreType.DMA((2,))],
        compiler_params=_cparams(("arbitrary",)), name="ffn")(idx3, idx3, xn, vals2, wg, wu, wd)


WIN = 64


def _combine_kernel(off_sm, posm_ref, x1_ref, y_hbm, fnw_ref, o_ref, ycat, yext, sem, sem_ext,
                    *, cap, y_rows, nblk):
    b = pl.program_id(0)
    slot = lax.rem(b, 2)
    nsub, E, _ = posm_ref.shape
    tb = x1_ref.shape[0]
    pm = jnp.concatenate([posm_ref[j] for j in range(nsub)], axis=1)
    wrow = lax.broadcasted_iota(I32, (WIN, tb), 0)

    def starts_of(blk, r):
        out = []
        for e in range(E):
            base = jnp.left_shift(jnp.right_shift(off_sm[blk * E + e], 4), 4)
            st = jnp.minimum(e * cap + base + r * WIN, y_rows - WIN)
            out.append(pl.multiple_of(st, BF16_ROWS))
        return out

    def copies(starts, dst, dsem):
        return [pltpu.make_async_copy(y_hbm.at[pl.ds(starts[e], WIN), :], dst.at[pl.ds(e * WIN, WIN), :], dsem)
                for e in range(E)]

    def onehot(starts):
        ps = []
        for e in range(E):
            pe = pm[e:e + 1, :]
            rel = jnp.where(pe >= 0, pe + (e * cap - starts[e]), -1)
            ps.append((wrow == rel).astype(BF16))
        return jnp.concatenate(ps, axis=0)

    @pl.when(b == 0)
    def _():
        for cp in copies(starts_of(b, 0), ycat.at[0], sem.at[0]):
            cp.start()

    @pl.when(b + 1 < nblk)
    def _():
        for cp in copies(starts_of(b + 1, 0), ycat.at[1 - slot], sem.at[1 - slot]):
            cp.start()

    starts0 = starts_of(b, 0)
    p0 = onehot(starts0)
    for cp in copies(starts0, ycat.at[slot], sem.at[slot]):
        cp.wait()
    acc0 = lax.dot_general(p0, ycat[slot], _TN, preferred_element_type=F32)

    nrounds = jnp.int32(1)
    for e in range(E):
        base = jnp.left_shift(jnp.right_shift(off_sm[b * E + e], 4), 4)
        nrounds = jnp.maximum(nrounds, jnp.right_shift(off_sm[(b + 1) * E + e] - base + (WIN - 1), 6))

    def round_body(r, acc):
        starts = starts_of(b, r)
        cps = copies(starts, yext, sem_ext)
        for cp in cps:
            cp.start()
        p = onehot(starts)
        for cp in cps:
            cp.wait()
        return acc + lax.dot_general(p, yext[...], _TN, preferred_element_type=F32)

    acc = lax.fori_loop(1, nrounds, round_body, acc0)
    o_ref[...] = _rms(x1_ref[...] + acc, fnw_ref[...])


def _combine(off_flat, posm3, x1, y, fnw, cap, tb):
    N, D = x1.shape
    nb, E, _ = posm3.shape
    nsub = tb // LANES
    y_rows = y.shape[0]
    gs = pltpu.PrefetchScalarGridSpec(
        num_scalar_prefetch=1, grid=(N // tb,),
        in_specs=[pl.BlockSpec((nsub, E, LANES), lambda i, o: (i, 0, 0)),
                  pl.BlockSpec((tb, D), lambda i, o: (i, 0)),
                  pl.BlockSpec(memory_space=pl.ANY),
                  pl.BlockSpec(fnw.shape, lambda i, o: (0, 0))],
        out_specs=pl.BlockSpec((tb, D), lambda i, o: (i, 0)),
        scratch_shapes=[pltpu.VMEM((2, E * WIN, D), BF16), pltpu.VMEM((E * WIN, D), BF16),
                        pltpu.SemaphoreType.DMA((2,)), pltpu.SemaphoreType.DMA])
    return pl.pallas_call(
        functools.partial(_combine_kernel, cap=cap, y_rows=y_rows, nblk=N // tb),
        out_shape=jax.ShapeDtypeStruct((N, D), F32), grid_spec=gs,
        compiler_params=_cparams(("arbitrary",)), name="combine")(off_flat, posm3, x1, y, fnw)


def _prep_params(norm_mix_w, w_in, conv_w, conv_b, gate_b, mlstm_norm_w, sgu_norm_w, sgu_w, sgu_b,
                 w_out, norm_ffn_w, router_w, w_gate, w_up, w_down, norm_final_w):
    d_a = mlstm_norm_w.shape[1]
    d_b = sgu_norm_w.shape[1]
    ng = gate_b.shape[1]
    w = w_in[0]
    o0, o1, o2, o3, o4, o5 = 2 * d_a, 3 * d_a, 4 * d_a, 4 * d_a + ng, 4 * d_a + ng + d_b, 4 * d_a + ng + 2 * d_b
    wg = w[:, o2:o3]
    return dict(
        nw=norm_mix_w[0][None, :],
        wqk=w[:, :o0].astype(BF16), wvt=w[:, o0:o1].T.astype(BF16), wo=w[:, o1:o2].astype(BF16),
        wgt=wg.T.astype(BF16),
        wu=w[:, o3:o4].astype(BF16), ws=w[:, o4:o5].astype(BF16),
        cw=conv_w[0], cb=conv_b[0][None, :], gbt=gate_b[0][:, None],
        snw=sgu_norm_w[0][None, :], mnw=mlstm_norm_w[0][None, :],
        sw=sgu_w[0].astype(BF16),
        sbias=jnp.repeat(sgu_b[0].T, HEAD_DIM, axis=1),
        wout=w_out[0].astype(BF16), fnw=norm_ffn_w[0][None, :], rwt=router_w[0].T,
        wgate=w_gate[0].astype(BF16), wup=w_up[0].astype(BF16), wdown=w_down[0].astype(BF16),
        nfw=norm_final_w[None, :],
    )


def _encoder(x, p):
    B, T, D = x.shape
    N = B * T
    E = N_EXPERTS
    cap = (N * CAPACITY_FACTOR) // E
    tm1 = min(512, T)
    q, k, vt, og, gt, u, s = _in_proj(
        x, p["nw"], p["wqk"], p["wvt"], p["wo"], p["wgt"], p["wu"], p["ws"],
        p["cw"], p["cb"], p["gbt"], p["snw"], tm1)
    hf, hb = _mlstm(q, k, vt, gt)
    flat = lambda a: a.reshape(N, a.shape[-1])
    x1, xn, aff3 = _mix_out(flat(hf), flat(hb), flat(og), flat(u), flat(s), flat(x),
                            p["mnw"], p["sw"], p["sbias"], p["wout"], p["fnw"], p["rwt"], min(256, N))
    posm3, off3 = _select(aff3, cap)
    nb = N // LANES
    off_flat = off3[:, :, 0].reshape(nb * E)
    acc = _compact(off_flat, posm3, aff3, cap)
    nt = cap // LANES
    idx = (acc[:, :nt, 0, :] * 256.0 + acc[:, :nt, 1, :]).astype(I32).reshape(E * cap)
    vals = (acc[:, :nt, 2, :] + acc[:, :nt, 3, :] + acc[:, :nt, 4, :]).reshape(E * cap, 1)
    ts = min(512, cap)
    y = _ffn(idx.reshape(E * cap // ts, 1, ts), xn, vals, p["wgate"], p["wup"], p["wdown"], cap, ts)
    tb = min(256, N)
    sub = tb // LANES
    off_tb = jnp.concatenate([off3[::sub, :, 0], jnp.full((1, E), cap, I32)], axis=0).reshape(-1)
    out = _combine(off_tb, posm3, x1, y, p["nfw"], cap, tb)
    return out.reshape(B, T, D)


def kernel(x_prompt, x_sample, norm_mix_w, w_in, conv_w, conv_b, gate_b, mlstm_norm_w, sgu_norm_w,
           sgu_w, sgu_b, w_out, norm_ffn_w, router_w, w_gate, w_up, w_down, norm_final_w):
    p = _prep_params(norm_mix_w, w_in, conv_w, conv_b, gate_b, mlstm_norm_w, sgu_norm_w, sgu_w,
                     sgu_b, w_out, norm_ffn_w, router_w, w_gate, w_up, w_down, norm_final_w)
    return (_encoder(x_prompt, p), _encoder(x_sample, p))
```

```python
import functools
import math

import jax
import jax.numpy as jnp
from jax import lax
from jax.experimental import pallas as pl
from jax.experimental.pallas import tpu as pltpu

F32 = jnp.float32
BF16 = jnp.bfloat16
I32 = jnp.int32

EPS = 1e-6
N_HEADS = 4
HEAD_DIM = 128
CHUNK = 128
N_EXPERTS = 16
CAPACITY_FACTOR = 2
LANES = 128
SUBLANES = 8
BF16_ROWS = 16
NEG_BIG = -1e30
VMEM_LIMIT = 48 * 1024 * 1024

_NT = (((1,), (1,)), ((), ()))
_TN = (((0,), (0,)), ((), ()))


def _cparams(sem, vmem=VMEM_LIMIT):
    return pltpu.CompilerParams(dimension_semantics=sem, vmem_limit_bytes=vmem)


def _dot(a, b):
    return jnp.dot(a, b, preferred_element_type=F32)


def _sigmoid(x):
    return 1.0 / (1.0 + jnp.exp(-x))


def _gelu(x):
    return 0.5 * x * (1.0 + lax.erf(x * (1.0 / math.sqrt(2.0))))


def _rms(x, w):
    ms = jnp.mean(x * x, axis=-1, keepdims=True)
    return x * lax.rsqrt(ms + EPS) * w


PROJ_COLS = 256


def _in_proj_kernel(x_ref, xp_ref, xn_ref, nw_ref, wqk_ref, wvt_ref, wo_ref, wgt_ref,
                    wu_ref, ws_ref, cw_ref, cb_ref, gbt_ref, snw_ref,
                    q_ref, k_ref, vt_ref, og_ref, gt_ref, u_ref, s_ref):
    i = pl.program_id(1)
    n_i = pl.num_programs(1)
    tm = x_ref.shape[1]
    d_a = q_ref.shape[2]
    nw = nw_ref[...]
    hb = _rms(x_ref[0], nw).astype(BF16)
    hp = jnp.where(i == 0, 0.0, _rms(xp_ref[0], nw)).astype(BF16)
    hn = jnp.where(i == n_i - 1, 0.0, _rms(xn_ref[0], nw)).astype(BF16)
    h_ext = jnp.concatenate([hb, hp, hn], axis=0)
    cw = cw_ref[...]
    cb = cb_ref[...]
    snw = snw_ref[...]
    row = lax.broadcasted_iota(I32, (tm, PROJ_COLS), 0)

    def qk_tile(j):
        cs = slice(j * PROJ_COLS, (j + 1) * PROJ_COLS)
        ze = _dot(h_ext, wqk_ref[:, cs])
        z = ze[:tm]
        zp = ze[tm + SUBLANES - 1:tm + SUBLANES]
        zn = ze[tm + SUBLANES:tm + SUBLANES + 1]
        z_prev = jnp.where(row == 0, zp, pltpu.roll(z, 1, axis=0))
        z_next = jnp.where(row == tm - 1, zn, pltpu.roll(z, tm - 1, axis=0))
        conv = cb[:, cs] + cw[0:1, cs] * z_prev + cw[1:2, cs] * z + cw[2:3, cs] * z_next
        qk = conv * _sigmoid(conv)
        if cs.start < d_a:
            q_ref[0, :, cs] = qk.astype(BF16)
        else:
            ks = slice(cs.start - d_a, cs.stop - d_a)
            k_ref[0, :, ks] = (qk * (1.0 / math.sqrt(HEAD_DIM))).astype(BF16)

    def u_tile(j):
        cs = slice(j * PROJ_COLS, (j + 1) * PROJ_COLS)
        u_ref[0, :, cs] = _gelu(_dot(hb, wu_ref[:, cs])).astype(BF16)

    def s_tile(j):
        cs = slice(j * PROJ_COLS, (j + 1) * PROJ_COLS)
        sv = _gelu(_dot(hb, ws_ref[:, cs]))
        for g in range(PROJ_COLS // HEAD_DIM):
            gs = slice(g * HEAD_DIM, (g + 1) * HEAD_DIM)
            og_cols = slice(cs.start + gs.start, cs.start + gs.stop)
            s_ref[0, :, og_cols] = _rms(sv[:, gs], snw[:, og_cols]).astype(BF16)

    def o_tile(j):
        cs = slice(j * PROJ_COLS, (j + 1) * PROJ_COLS)
        og_ref[0, :, cs] = _sigmoid(_dot(hb, wo_ref[:, cs])).astype(BF16)

    def vt_tile(j):
        cs = slice(j * PROJ_COLS, (j + 1) * PROJ_COLS)
        vt_ref[0, cs, :] = lax.dot_general(wvt_ref[cs, :], hb, _NT, preferred_element_type=F32).astype(BF16)

    n_qk = wqk_ref.shape[1] // PROJ_COLS
    n_b = wu_ref.shape[1] // PROJ_COLS
    n_a = wo_ref.shape[1] // PROJ_COLS
    light = [(u_tile, j) for j in range(n_b)] + [(s_tile, j) for j in range(n_b)]
    light = [light[(k // 2) + (k % 2) * n_b] for k in range(2 * n_b)]
    for j in range(n_qk):
        qk_tile(j)
        if j < len(light):
            fn, jj = light[j]
            fn(jj)
    for fn, jj in light[n_qk:]:
        fn(jj)
    for j in range(n_a):
        o_tile(j)
        vt_tile(j)

    zgt = lax.dot_general(wgt_ref[...], hb, _NT, preferred_element_type=F32) + gbt_ref[...]
    rowt = lax.broadcasted_iota(I32, zgt.shape, 0)
    gt_ref[0] = jnp.where(rowt < 2 * N_HEADS, zgt, jax.nn.log_sigmoid(zgt))


def _in_proj(x, nw, wqk, wvt, wo, wgt, wu, ws, cw, cb, gbt, snw, tm):
    B, T, D = x.shape
    d_a = wvt.shape[0]
    d_b = wu.shape[1]
    ng = wgt.shape[0]
    nt = T // tm
    hb8 = tm // SUBLANES
    last8 = T // SUBLANES - 1
    full = lambda a: pl.BlockSpec(a.shape, lambda b, i: (0,) * a.ndim)
    tok = lambda w: pl.BlockSpec((1, tm, w), lambda b, i: (b, i, 0))
    in_specs = [
        pl.BlockSpec((1, tm, D), lambda b, i: (b, i, 0)),
        pl.BlockSpec((1, SUBLANES, D), lambda b, i: (b, jnp.maximum(i * hb8 - 1, 0), 0)),
        pl.BlockSpec((1, SUBLANES, D), lambda b, i: (b, jnp.minimum((i + 1) * hb8, last8), 0)),
    ] + [full(a) for a in (nw, wqk, wvt, wo, wgt, wu, ws, cw, cb, gbt, snw)]
    out_shape = (
        jax.ShapeDtypeStruct((B, T, d_a), BF16), jax.ShapeDtypeStruct((B, T, d_a), BF16),
        jax.ShapeDtypeStruct((B, d_a, T), BF16), jax.ShapeDtypeStruct((B, T, d_a), BF16),
        jax.ShapeDtypeStruct((B, ng, T), F32),
        jax.ShapeDtypeStruct((B, T, d_b), BF16), jax.ShapeDtypeStruct((B, T, d_b), BF16),
    )
    out_specs = (tok(d_a), tok(d_a), pl.BlockSpec((1, d_a, tm), lambda b, i: (b, 0, i)), tok(d_a),
                 pl.BlockSpec((1, ng, tm), lambda b, i: (b, 0, i)), tok(d_b), tok(d_b))
    return pl.pallas_call(
        _in_proj_kernel, out_shape=out_shape, grid=(B, nt), in_specs=in_specs,
        out_specs=out_specs, compiler_params=_cparams(("parallel", "arbitrary")),
        name="in_proj")(x, x, x, nw, wqk, wvt, wo, wgt, wu, ws, cw, cb, gbt, snw)


def _split3(x):
    x0 = x.astype(BF16)
    r1 = x - x0.astype(F32)
    x1 = r1.astype(BF16)
    x2 = (r1 - x1.astype(F32)).astype(BF16)
    return x0, x1, x2


def _mlstm_kernel(qf_ref, kf_ref, vtf_ref, gtf_ref, qb_ref, kb_ref, vtb_ref, gtb_ref,
                  hf_ref, hb_ref, c_st, m_st):
    c = pl.program_id(1)
    L = qf_ref.shape[1]
    d = HEAD_DIM

    @pl.when(c == 0)
    def _():
        c_st[...] = jnp.zeros_like(c_st)
        m_st[...] = jnp.zeros_like(m_st)

    r0 = lax.broadcasted_iota(I32, (L, L), 0)
    r1 = lax.broadcasted_iota(I32, (L, L), 1)
    ones8 = jnp.ones((SUBLANES, L), BF16)
    ng = 2 * N_HEADS
    dirs = ((qf_ref, kf_ref, vtf_ref, gtf_ref[0], r0 <= r1, hf_ref),
            (qb_ref, kb_ref, vtb_ref, gtb_ref[0], r0 >= r1, hb_ref))

    chains = []
    for dr, (q_ref, k_ref, vt_ref, gt, mask_st, h_ref) in enumerate(dirs):
        g3 = jnp.concatenate(_split3(gt), axis=0)
        b3 = _dot(g3, mask_st.astype(BF16))
        nr = gt.shape[0]
        br_all = b3[:nr] + b3[nr:2 * nr] + b3[2 * nr:]
        a_rows = gt[:ng] - br_all[ng:]
        a_cols = jnp.concatenate([a_rows, jnp.zeros((L - ng, L), F32)], axis=0).T
        for hd in range(N_HEADS):
            j = dr * N_HEADS + hd
            hs = slice(hd * d, (hd + 1) * d)
            chains.append(dict(
                j=j, hs=hs, h_ref=h_ref, mask=mask_st, qb=q_ref[0, :, hs], kb=k_ref[0, :, hs],
                vt_aug=jnp.concatenate([vt_ref[0, hs, :], ones8], axis=0),
                a_col=a_cols[:, j:j + 1], i_row=gt[j:j + 1, :], b_row=br_all[ng + j:ng + j + 1, :],
                b_tot=jnp.sum(gt[ng + j:ng + j + 1, :], axis=1, keepdims=True),
                caug=c_st[j], m_prev=m_st[j]))

    for ch in chains:
        ch["s_raw"] = lax.dot_general(ch["kb"], ch["qb"], _NT, preferred_element_type=F32)
        ch["ia"] = lax.dot_general(ch["caug"].astype(BF16), ch["qb"], _NT, preferred_element_type=F32)
    for ch in chains:
        dmat = jnp.where(ch["mask"], ch["a_col"] + ch["b_row"], NEG_BIG)
        inter = ch["b_row"] + ch["m_prev"]
        m_t = jnp.maximum(jnp.max(dmat, axis=0, keepdims=True), inter)
        st = ch["s_raw"] * jnp.exp(dmat - m_t)
        w_inter = jnp.exp(inter - m_t)
        den = jnp.sum(st, axis=0, keepdims=True) + w_inter * ch["ia"][d:d + 1]
        ch["st"] = st.astype(BF16)
        ch["w_inter"] = w_inter
        ch["rden"] = 1.0 / jnp.maximum(jnp.abs(den), jnp.exp(-m_t))
    for ch in chains:
        num = _dot(ch["vt_aug"][:d], ch["st"]) + ch["w_inter"] * ch["ia"][:d]
        ch["h_ref"][0, :, ch["hs"]] = (num * ch["rden"]).T
    for ch in chains:
        g_row = ch["b_tot"] - ch["b_row"] + ch["i_row"]
        m_new = jnp.maximum(ch["b_tot"] + ch["m_prev"], jnp.max(g_row, axis=1, keepdims=True))
        wk = jnp.exp(g_row - m_new)
        decay = jnp.exp(ch["b_tot"] + ch["m_prev"] - m_new)
        vw = (ch["vt_aug"].astype(F32) * wk).astype(BF16)
        c_st[ch["j"]] = decay * ch["caug"] + _dot(vw, ch["kb"])
        m_st[ch["j"]] = m_new


def _mlstm(q, k, vt, gt):
    B, T, d_a = q.shape
    L = CHUNK
    nc = T // L
    ng = gt.shape[1]
    fwd = lambda w: pl.BlockSpec((1, L, w), lambda b, c: (b, c, 0))
    bwd = lambda w: pl.BlockSpec((1, L, w), lambda b, c: (b, nc - 1 - c, 0))
    fwd_t = lambda r: pl.BlockSpec((1, r, L), lambda b, c: (b, 0, c))
    bwd_t = lambda r: pl.BlockSpec((1, r, L), lambda b, c: (b, 0, nc - 1 - c))
    in_specs = [fwd(d_a), fwd(d_a), fwd_t(d_a), fwd_t(ng), bwd(d_a), bwd(d_a), bwd_t(d_a), bwd_t(ng)]
    nch = 2 * N_HEADS
    return pl.pallas_call(
        _mlstm_kernel,
        out_shape=(jax.ShapeDtypeStruct((B, T, d_a), F32), jax.ShapeDtypeStruct((B, T, d_a), F32)),
        grid=(B, nc), in_specs=in_specs, out_specs=(fwd(d_a), bwd(d_a)),
        scratch_shapes=[pltpu.VMEM((nch, HEAD_DIM + SUBLANES, HEAD_DIM), F32),
                        pltpu.VMEM((nch, 1, 1), F32)],
        compiler_params=_cparams(("parallel", "arbitrary")),
        name="mlstm")(q, k, vt, gt, q, k, vt, gt)


MIX_SUB_ROWS = 256


def _mix_out_kernel(hf_ref, hb_ref, og_ref, u_ref, s_ref, x_ref, mnw_ref, sw_ref, sb_ref,
                    wout_ref, fnw_ref, rwt_ref, x1_ref, xn_ref, aff_ref):
    tm = x_ref.shape[0]
    d_a = og_ref.shape[1]
    d_b = u_ref.shape[1]
    sub = min(tm, MIX_SUB_ROWS)
    subs = [slice(i * sub, (i + 1) * sub) for i in range(tm // sub)]
    mnw = mnw_ref[...]
    sbias = sb_ref[...]

    gates = []
    for rs in subs:
        rows = []
        for cc in range(sub // CHUNK):
            r0 = rs.start + cc * CHUNK
            cols = [_dot(sw_ref[g], s_ref[r0:r0 + CHUNK, g * HEAD_DIM:(g + 1) * HEAD_DIM])
                    for g in range(d_b // HEAD_DIM)]
            rows.append(jnp.concatenate(cols, axis=1) + sbias)
        gates.append(jnp.concatenate(rows, axis=0))

    mixes = []
    for rs, gate in zip(subs, gates):
        h = hf_ref[rs, :] + hb_ref[rs, :]
        parts = [_rms(h[:, hd * HEAD_DIM:(hd + 1) * HEAD_DIM], mnw[:, hd * HEAD_DIM:(hd + 1) * HEAD_DIM])
                 for hd in range(d_a // HEAD_DIM)]
        a_out = (og_ref[rs, :].astype(F32) * jnp.concatenate(parts, axis=1)).astype(BF16)
        b_out = (u_ref[rs, :].astype(F32) * gate).astype(BF16)
        mixes.append(jnp.concatenate([a_out, b_out], axis=1))

    x1s = [x_ref[rs, :] + _dot(mix, wout_ref[...]) for rs, mix in zip(subs, mixes)]

    xns = []
    for rs, x1 in zip(subs, x1s):
        x1_ref[rs, :] = x1
        xn = _rms(x1, fnw_ref[...])
        xns.append(xn)
        chunks = xn.shape[1] // LANES
        for j in range(chunks):
            xn_ref[pl.ds(rs.start * chunks + j, sub, stride=chunks), :] = xn[:, j * LANES:(j + 1) * LANES]

    E = rwt_ref.shape[0]
    r0 = rwt_ref[...].astype(BF16)
    r1 = (rwt_ref[...] - r0.astype(F32)).astype(BF16)
    r01 = jnp.concatenate([r0, r1], axis=0)
    logits = []
    for xn in xns:
        x0 = xn.astype(BF16)
        x1 = (xn - x0.astype(F32)).astype(BF16)
        a = lax.dot_general(r01, x0, _NT, preferred_element_type=F32)
        b = lax.dot_general(r0, x1, _NT, preferred_element_type=F32)
        logits.append(a[:E] + a[E:] + b)
    for rs, lg in zip(subs, logits):
        ex = jnp.exp(lg - jnp.max(lg, axis=0, keepdims=True))
        aff = ex / jnp.sum(ex, axis=0, keepdims=True)
        for j in range(sub // LANES):
            aff_ref[rs.start // LANES + j] = aff[:, j * LANES:(j + 1) * LANES]


def _mix_out(hf, hb, og, u, s, x, mnw, sw, sbias, wout, fnw, rwt, tm):
    N, D = x.shape
    d_a = og.shape[1]
    d_b = u.shape[1]
    E = rwt.shape[0]
    nt = N // tm
    full = lambda a: pl.BlockSpec(a.shape, lambda i: (0,) * a.ndim)
    tok = lambda w: pl.BlockSpec((tm, w), lambda i: (i, 0))
    in_specs = [tok(d_a), tok(d_a), tok(d_a), tok(d_b), tok(d_b), tok(D)] + [
        full(a) for a in (mnw, sw, sbias, wout, fnw, rwt)]
    chunks = D // LANES
    out_shape = (jax.ShapeDtypeStruct((N, D), F32), jax.ShapeDtypeStruct((N * chunks, LANES), F32),
                 jax.ShapeDtypeStruct((N // LANES, E, LANES), F32))
    out_specs = (tok(D), pl.BlockSpec((tm * chunks, LANES), lambda i: (i, 0)),
                 pl.BlockSpec((tm // LANES, E, LANES), lambda i: (i, 0, 0)))
    return pl.pallas_call(
        _mix_out_kernel, out_shape=out_shape, grid=(nt,), in_specs=in_specs, out_specs=out_specs,
        compiler_params=_cparams(("parallel",)), name="mix_out")(
            hf, hb, og, u, s, x, mnw, sw, sbias, wout, fnw, rwt)


def _select_kernel(aff_ref, posm_ref, off_ref, cnt_s, wi_s, *, cap):
    nb, E, _ = aff_ref.shape
    aff = aff_ref[...]

    def count_ge(cand):
        c = jnp.sum((aff >= cand).astype(I32), axis=0, keepdims=True)
        return jnp.sum(c, axis=2, keepdims=True)

    def bit_step(i, thr_bits):
        cand = thr_bits | jnp.left_shift(jnp.int32(1), 30 - i)
        return jnp.where(count_ge(pltpu.bitcast(cand, F32)) >= cap, cand, thr_bits)

    thr = pltpu.bitcast(lax.fori_loop(0, 31, bit_step, jnp.zeros((1, E, 1), I32)), F32)
    gt = aff > thr
    eq = aff == thr
    n_gt = jnp.sum(jnp.sum(gt.astype(I32), axis=0, keepdims=True), axis=2, keepdims=True)
    need = cap - n_gt

    li = lax.broadcasted_iota(I32, (LANES, LANES), 0)
    lj = lax.broadcasted_iota(I32, (LANES, LANES), 1)
    upper = (li < lj).astype(BF16)

    def excl_cumsum(flag):
        fb = flag.astype(BF16).reshape(nb * E, LANES)
        wi_s[...] = _dot(fb, upper).astype(I32).reshape(nb, E, LANES)
        cnt_s[...] = jnp.sum(flag.astype(I32), axis=2, keepdims=True)

        def blk(b, run):
            wi_s[b] = wi_s[b] + run
            return run + cnt_s[b]

        lax.fori_loop(0, nb, blk, jnp.zeros((E, 1), I32))
        return wi_s[...]

    eq_rank = excl_cumsum(eq)
    sel = gt | (eq & (eq_rank < need))
    pos = excl_cumsum(sel)
    posm_ref[...] = jnp.where(sel, pos, -1)
    off_ref[...] = jnp.broadcast_to(pos[:, :, 0:1], off_ref.shape)


def _select(aff3, cap):
    nb, E, _ = aff3.shape
    return pl.pallas_call(
        functools.partial(_select_kernel, cap=cap),
        out_shape=(jax.ShapeDtypeStruct((nb, E, LANES), I32), jax.ShapeDtypeStruct((nb, E, LANES), I32)),
        scratch_shapes=[pltpu.VMEM((nb, E, 1), I32), pltpu.VMEM((nb, E, LANES), I32)],
        compiler_params=_cparams(None), name="select")(aff3)


def _compact_kernel(off_sm, posm_ref, aff_ref, acc_ref):
    nb, E, _ = posm_ref.shape
    acc_ref[...] = jnp.zeros_like(acc_ref)
    srow = lax.broadcasted_iota(I32, (2 * LANES, LANES), 0)
    r8 = lax.broadcasted_iota(I32, (SUBLANES, LANES), 0)
    lane8 = lax.broadcasted_iota(I32, (SUBLANES, LANES), 1)

    def blk(b, carry):
        tok = b * LANES + lane8
        t_hi = jnp.right_shift(tok, 8).astype(F32)
        t_lo = jnp.bitwise_and(tok, 255).astype(F32)
        pm = posm_ref[b]
        af = aff_ref[b]
        for e in range(E):
            off = off_sm[b * E + e]
            j0 = jnp.right_shift(off, 7)
            rel = pm[e:e + 1, :] - j0 * LANES
            onehot = (srow == rel).astype(BF16)
            a = af[e:e + 1, :]
            a0 = a.astype(BF16)
            r1 = a - a0.astype(F32)
            a1 = r1.astype(BF16)
            a2 = (r1 - a1.astype(F32)).astype(BF16)
            lhs = jnp.where(r8 == 0, t_hi, jnp.where(r8 == 1, t_lo, 0.0))
            lhs = jnp.where(r8 == 2, a0.astype(F32), lhs)
            lhs = jnp.where(r8 == 3, a1.astype(F32), lhs)
            lhs = jnp.where(r8 == 4, a2.astype(F32), lhs).astype(BF16)
            out = lax.dot_general(lhs, onehot, _NT, preferred_element_type=F32)
            acc_ref[e, j0] = acc_ref[e, j0] + out[:, :LANES]
            acc_ref[e, j0 + 1] = acc_ref[e, j0 + 1] + out[:, LANES:]
        return carry

    lax.fori_loop(0, nb, blk, 0)


def _compact(off_flat, posm3, aff3, cap):
    nb, E, _ = posm3.shape
    nt_pad = cap // LANES + 2
    gs = pltpu.PrefetchScalarGridSpec(
        num_scalar_prefetch=1, grid=(1,),
        in_specs=[pl.BlockSpec(posm3.shape, lambda i, o: (0, 0, 0)),
                  pl.BlockSpec(aff3.shape, lambda i, o: (0, 0, 0))],
        out_specs=pl.BlockSpec((E, nt_pad, SUBLANES, LANES), lambda i, o: (0, 0, 0, 0)))
    return pl.pallas_call(
        _compact_kernel, out_shape=jax.ShapeDtypeStruct((E, nt_pad, SUBLANES, LANES), F32),
        grid_spec=gs, compiler_params=_cparams(("arbitrary",)), name="compact")(off_flat, posm3, aff3)


def _ffn_kernel(idc_sm, idn_sm, xn_hbm, cacc_ref, wg_ref, wu_ref, wd_ref, y_ref, xbuf, sem,
                *, n_real, fc):
    s = pl.program_id(0)
    ts = y_ref.shape[0]
    chunks = xbuf.shape[1] // ts
    slot = lax.rem(s, 2)

    def row_copy(idx_sm, r, dst_slot):
        src = pl.multiple_of(idx_sm[0, 0, r] * chunks, chunks)
        return pltpu.make_async_copy(xn_hbm.at[pl.ds(src, chunks), :],
                                     xbuf.at[dst_slot, pl.ds(r * chunks, chunks), :], sem.at[dst_slot])

    @pl.when(s == 0)
    def _():
        def body(r, carry):
            row_copy(idc_sm, r, 0).start()
            return carry
        lax.fori_loop(0, ts, body, 0, unroll=8)

    pltpu.make_async_copy(xn_hbm.at[pl.ds(0, ts * chunks), :], xbuf.at[slot], sem.at[slot]).wait()

    @pl.when(s < n_real)
    def _():
        xs = xbuf.at[slot]
        x = jnp.concatenate([xs[pl.ds(j, ts, stride=chunks), :] for j in range(chunks)],
                            axis=1).astype(BF16)
        F = wg_ref.shape[2]
        nchunk = F // fc
        rows_per_chunk = ts // nchunk
        acc = jnp.zeros((ts, wd_ref.shape[2]), F32)
        for c in range(nchunk):
            for r in range(c * rows_per_chunk, (c + 1) * rows_per_chunk):
                row_copy(idn_sm, r, 1 - slot).start()
            cs = slice(c * fc, (c + 1) * fc)
            gte = _dot(x, wg_ref[0, :, cs])
            up = _dot(x, wu_ref[0, :, cs])
            hid = (gte * _sigmoid(gte) * up).astype(BF16)
            acc = acc + _dot(hid, wd_ref[0, cs, :])
        for g in range(ts // LANES):
            t = cacc_ref[0, g]
            vrow = t[2:3, :] + t[3:4, :] + t[4:5, :]
            vmat = jnp.broadcast_to(vrow, (LANES, LANES)).T
            vfull = jnp.concatenate([vmat] * (acc.shape[1] // LANES), axis=1)
            rs = slice(g * LANES, (g + 1) * LANES)
            y_ref[rs, :] = (acc[rs, :] * vfull).astype(y_ref.dtype)

    @pl.when(s >= n_real)
    def _():
        y_ref[...] = jnp.zeros_like(y_ref)


def _ffn(idx3, xn, cacc, wg, wu, wd, cap, ts):
    E, D, F = wg.shape
    nts = cap // ts
    n_real = E * nts
    last = n_real - 1
    eidx = lambda s: jnp.minimum(s // nts, E - 1)
    in_specs = [
        pl.BlockSpec((1, 1, ts), lambda s: (jnp.minimum(s, last), 0, 0), memory_space=pltpu.SMEM),
        pl.BlockSpec((1, 1, ts), lambda s: (jnp.minimum(s + 1, last), 0, 0), memory_space=pltpu.SMEM),
        pl.BlockSpec(memory_space=pl.ANY),
        pl.BlockSpec((1, ts // LANES, SUBLANES, LANES),
                     lambda s: (eidx(s), lax.rem(jnp.minimum(s, last), nts), 0, 0)),
        pl.BlockSpec((1, D, F), lambda s: (eidx(s), 0, 0)),
        pl.BlockSpec((1, D, F), lambda s: (eidx(s), 0, 0)),
        pl.BlockSpec((1, F, D), lambda s: (eidx(s), 0, 0)),
    ]
    return pl.pallas_call(
        functools.partial(_ffn_kernel, n_real=n_real, fc=512),
        out_shape=jax.ShapeDtypeStruct(((n_real + 1) * ts, D), BF16),
        grid=(n_real + 1,), in_specs=in_specs,
        out_specs=pl.BlockSpec((ts, D), lambda s: (s, 0)),
        scratch_shapes=[pltpu.VMEM((2, ts * (D // LANES), LANES), F32), pltpu.SemaphoreType.DMA((2,))],
        compiler_params=_cparams(("arbitrary",)), name="ffn")(idx3, idx3, xn, cacc, wg, wu, wd)


WIN = 64


def _combine_kernel(off_sm, posm_ref, x1_ref, y_hbm, fnw_ref, o_ref, ycat, yext, sem, sem_ext,
                    *, cap, y_rows, nblk):
    b = pl.program_id(0)
    slot = lax.rem(b, 2)
    nsub, E, _ = posm_ref.shape
    tb = x1_ref.shape[0]
    pm = jnp.concatenate([posm_ref[j] for j in range(nsub)], axis=1)
    wrow = lax.broadcasted_iota(I32, (WIN, tb), 0)

    def starts_of(blk, r):
        out = []
        for e in range(E):
            base = jnp.left_shift(jnp.right_shift(off_sm[blk * E + e], 4), 4)
            st = jnp.minimum(e * cap + base + r * WIN, y_rows - WIN)
            out.append(pl.multiple_of(st, BF16_ROWS))
        return out

    def copies(starts, dst, dsem):
        return [pltpu.make_async_copy(y_hbm.at[pl.ds(starts[e], WIN), :], dst.at[pl.ds(e * WIN, WIN), :], dsem)
                for e in range(E)]

    def onehot(starts):
        ps = []
        for e in range(E):
            pe = pm[e:e + 1, :]
            rel = jnp.where(pe >= 0, pe + (e * cap - starts[e]), -1)
            ps.append((wrow == rel).astype(BF16))
        return jnp.concatenate(ps, axis=0)

    @pl.when(b == 0)
    def _():
        for cp in copies(starts_of(b, 0), ycat.at[0], sem.at[0]):
            cp.start()

    @pl.when(b + 1 < nblk)
    def _():
        for cp in copies(starts_of(b + 1, 0), ycat.at[1 - slot], sem.at[1 - slot]):
            cp.start()

    starts0 = starts_of(b, 0)
    p0 = onehot(starts0)
    for cp in copies(starts0, ycat.at[slot], sem.at[slot]):
        cp.wait()
    acc0 = lax.dot_general(p0, ycat[slot], _TN, preferred_element_type=F32)

    nrounds = jnp.int32(1)
    for e in range(E):
        base = jnp.left_shift(jnp.right_shift(off_sm[b * E + e], 4), 4)
        nrounds = jnp.maximum(nrounds, jnp.right_shift(off_sm[(b + 1) * E + e] - base + (WIN - 1), 6))

    def round_body(r, acc):
        starts = starts_of(b, r)
        cps = copies(starts, yext, sem_ext)
        for cp in cps:
            cp.start()
        p = onehot(starts)
        for cp in cps:
            cp.wait()
        return acc + lax.dot_general(p, yext[...], _TN, preferred_element_type=F32)

    acc = lax.fori_loop(1, nrounds, round_body, acc0)
    o_ref[...] = _rms(x1_ref[...] + acc, fnw_ref[...])


def _combine(off_flat, posm3, x1, y, fnw, cap, tb):
    N, D = x1.shape
    nb, E, _ = posm3.shape
    nsub = tb // LANES
    y_rows = y.shape[0]
    gs = pltpu.PrefetchScalarGridSpec(
        num_scalar_prefetch=1, grid=(N // tb,),
        in_specs=[pl.BlockSpec((nsub, E, LANES), lambda i, o: (i, 0, 0)),
                  pl.BlockSpec((tb, D), lambda i, o: (i, 0)),
                  pl.BlockSpec(memory_space=pl.ANY),
                  pl.BlockSpec(fnw.shape, lambda i, o: (0, 0))],
        out_specs=pl.BlockSpec((tb, D), lambda i, o: (i, 0)),
        scratch_shapes=[pltpu.VMEM((2, E * WIN, D), BF16), pltpu.VMEM((E * WIN, D), BF16),
                        pltpu.SemaphoreType.DMA((2,)), pltpu.SemaphoreType.DMA])
    return pl.pallas_call(
        functools.partial(_combine_kernel, cap=cap, y_rows=y_rows, nblk=N // tb),
        out_shape=jax.ShapeDtypeStruct((N, D), F32), grid_spec=gs,
        compiler_params=_cparams(("arbitrary",)), name="combine")(off_flat, posm3, x1, y, fnw)


def _prep_params(norm_mix_w, w_in, conv_w, conv_b, gate_b, mlstm_norm_w, sgu_norm_w, sgu_w, sgu_b,
                 w_out, norm_ffn_w, router_w, w_gate, w_up, w_down, norm_final_w):
    d_a = mlstm_norm_w.shape[1]
    d_b = sgu_norm_w.shape[1]
    ng = gate_b.shape[1]
    w = w_in[0]
    o0, o1, o2, o3, o4, o5 = 2 * d_a, 3 * d_a, 4 * d_a, 4 * d_a + ng, 4 * d_a + ng + d_b, 4 * d_a + ng + 2 * d_b
    wg = w[:, o2:o3]
    return dict(
        nw=norm_mix_w[0][None, :],
        wqk=w[:, :o0].astype(BF16), wvt=w[:, o0:o1].T.astype(BF16), wo=w[:, o1:o2].astype(BF16),
        wgt=wg.T.astype(BF16),
        wu=w[:, o3:o4].astype(BF16), ws=w[:, o4:o5].astype(BF16),
        cw=conv_w[0], cb=conv_b[0][None, :], gbt=gate_b[0][:, None],
        snw=sgu_norm_w[0][None, :], mnw=mlstm_norm_w[0][None, :],
        sw=sgu_w[0].astype(BF16),
        sbias=jnp.repeat(sgu_b[0].T, HEAD_DIM, axis=1),
        wout=w_out[0].astype(BF16), fnw=norm_ffn_w[0][None, :], rwt=router_w[0].T,
        wgate=w_gate[0].astype(BF16), wup=w_up[0].astype(BF16), wdown=w_down[0].astype(BF16),
        nfw=norm_final_w[None, :],
    )


def _encoder(x, p):
    B, T, D = x.shape
    N = B * T
    E = N_EXPERTS
    cap = (N * CAPACITY_FACTOR) // E
    tm1 = min(512, T)
    q, k, vt, og, gt, u, s = _in_proj(
        x, p["nw"], p["wqk"], p["wvt"], p["wo"], p["wgt"], p["wu"], p["ws"],
        p["cw"], p["cb"], p["gbt"], p["snw"], tm1)
    hf, hb = _mlstm(q, k, vt, gt)
    flat = lambda a: a.reshape(N, a.shape[-1])
    x1, xn, aff3 = _mix_out(flat(hf), flat(hb), flat(og), flat(u), flat(s), flat(x),
                            p["mnw"], p["sw"], p["sbias"], p["wout"], p["fnw"], p["rwt"], min(512, N))
    posm3, off3 = _select(aff3, cap)
    nb = N // LANES
    off_flat = off3[:, :, 0].reshape(nb * E)
    cacc = _compact(off_flat, posm3, aff3, cap)
    nt = cap // LANES
    idx = (cacc[:, :nt, 0, :] * 256.0 + cacc[:, :nt, 1, :]).astype(I32).reshape(E * cap)
    ts = min(512, cap)
    y = _ffn(idx.reshape(E * cap // ts, 1, ts), xn, cacc, p["wgate"], p["wup"], p["wdown"], cap, ts)
    tb = min(256, N)
    sub = tb // LANES
    off_tb = jnp.concatenate([off3[::sub, :, 0], jnp.full((1, E), cap, I32)], axis=0).reshape(-1)
    out = _combine(off_tb, posm3, x1, y, p["nfw"], cap, tb)
    return out.reshape(B, T, D)


def kernel(x_prompt, x_sample, norm_mix_w, w_in, conv_w, conv_b, gate_b, mlstm_norm_w, sgu_norm_w,
           sgu_w, sgu_b, w_out, norm_ffn_w, router_w, w_gate, w_up, w_down, norm_final_w):
    p = _prep_params(norm_mix_w, w_in, conv_w, conv_b, gate_b, mlstm_norm_w, sgu_norm_w, sgu_w,
                     sgu_b, w_out, norm_ffn_w, router_w, w_gate, w_up, w_down, norm_final_w)
    return (_encoder(x_prompt, p), _encoder(x_sample, p))
```

```python
import functools
import math

import jax
import jax.numpy as jnp
from jax import lax
from jax.experimental import pallas as pl
from jax.experimental.pallas import tpu as pltpu

F32 = jnp.float32
BF16 = jnp.bfloat16
I32 = jnp.int32

EPS = 1e-6
N_HEADS = 4
HEAD_DIM = 128
CHUNK = 128
N_EXPERTS = 16
CAPACITY_FACTOR = 2
LANES = 128
SUBLANES = 8
BF16_ROWS = 16
NEG_BIG = -1e30
VMEM_LIMIT = 48 * 1024 * 1024

_NT = (((1,), (1,)), ((), ()))
_TN = (((0,), (0,)), ((), ()))


def _cparams(sem, vmem=VMEM_LIMIT):
    return pltpu.CompilerParams(dimension_semantics=sem, vmem_limit_bytes=vmem)


def _dot(a, b):
    return jnp.dot(a, b, preferred_element_type=F32)


def _sigmoid(x):
    return 1.0 / (1.0 + jnp.exp(-x))


def _gelu(x):
    return 0.5 * x * (1.0 + lax.erf(x * (1.0 / math.sqrt(2.0))))


def _rms(x, w):
    ms = jnp.mean(x * x, axis=-1, keepdims=True)
    return x * lax.rsqrt(ms + EPS) * w


PROJ_COLS = 256


def _in_proj_kernel(x_ref, xp_ref, xn_ref, nw_ref, wqk_ref, wvt_ref, wo_ref, wgt_ref,
                    wu_ref, ws_ref, cw_ref, cb_ref, gbt_ref, snw_ref,
                    q_ref, k_ref, vt_ref, og_ref, gt_ref, u_ref, s_ref):
    i = pl.program_id(1)
    n_i = pl.num_programs(1)
    tm = x_ref.shape[1]
    d_a = q_ref.shape[2]
    nw = nw_ref[...]
    hb = _rms(x_ref[0], nw).astype(BF16)
    hp = jnp.where(i == 0, 0.0, _rms(xp_ref[0], nw)).astype(BF16)
    hn = jnp.where(i == n_i - 1, 0.0, _rms(xn_ref[0], nw)).astype(BF16)
    h_ext = jnp.concatenate([hb, hp, hn], axis=0)
    cw = cw_ref[...]
    cb = cb_ref[...]
    snw = snw_ref[...]
    row = lax.broadcasted_iota(I32, (tm, PROJ_COLS), 0)

    def qk_tile(j):
        cs = slice(j * PROJ_COLS, (j + 1) * PROJ_COLS)
        ze = _dot(h_ext, wqk_ref[:, cs])
        z = ze[:tm]
        zp = ze[tm + SUBLANES - 1:tm + SUBLANES]
        zn = ze[tm + SUBLANES:tm + SUBLANES + 1]
        z_prev = jnp.where(row == 0, zp, pltpu.roll(z, 1, axis=0))
        z_next = jnp.where(row == tm - 1, zn, pltpu.roll(z, tm - 1, axis=0))
        conv = cb[:, cs] + cw[0:1, cs] * z_prev + cw[1:2, cs] * z + cw[2:3, cs] * z_next
        qk = conv * _sigmoid(conv)
        if cs.start < d_a:
            q_ref[0, :, cs] = qk.astype(BF16)
        else:
            ks = slice(cs.start - d_a, cs.stop - d_a)
            k_ref[0, :, ks] = (qk * (1.0 / math.sqrt(HEAD_DIM))).astype(BF16)

    def u_tile(j):
        cs = slice(j * PROJ_COLS, (j + 1) * PROJ_COLS)
        u_ref[0, :, cs] = _gelu(_dot(hb, wu_ref[:, cs])).astype(BF16)

    def s_tile(j):
        cs = slice(j * PROJ_COLS, (j + 1) * PROJ_COLS)
        sv = _gelu(_dot(hb, ws_ref[:, cs]))
        for g in range(PROJ_COLS // HEAD_DIM):
            gs = slice(g * HEAD_DIM, (g + 1) * HEAD_DIM)
            og_cols = slice(cs.start + gs.start, cs.start + gs.stop)
            s_ref[0, :, og_cols] = _rms(sv[:, gs], snw[:, og_cols]).astype(BF16)

    def o_tile(j):
        cs = slice(j * PROJ_COLS, (j + 1) * PROJ_COLS)
        og_ref[0, :, cs] = _sigmoid(_dot(hb, wo_ref[:, cs])).astype(BF16)

    def vt_tile(j):
        cs = slice(j * PROJ_COLS, (j + 1) * PROJ_COLS)
        vt_ref[0, cs, :] = lax.dot_general(wvt_ref[cs, :], hb, _NT, preferred_element_type=F32).astype(BF16)

    n_qk = wqk_ref.shape[1] // PROJ_COLS
    n_b = wu_ref.shape[1] // PROJ_COLS
    n_a = wo_ref.shape[1] // PROJ_COLS
    light = [(u_tile, j) for j in range(n_b)] + [(s_tile, j) for j in range(n_b)]
    light = [light[(k // 2) + (k % 2) * n_b] for k in range(2 * n_b)]
    for j in range(n_qk):
        qk_tile(j)
        if j < len(light):
            fn, jj = light[j]
            fn(jj)
    for fn, jj in light[n_qk:]:
        fn(jj)
    for j in range(n_a):
        o_tile(j)
        vt_tile(j)

    zgt = lax.dot_general(wgt_ref[...], hb, _NT, preferred_element_type=F32) + gbt_ref[...]
    rowt = lax.broadcasted_iota(I32, zgt.shape, 0)
    gt_ref[0] = jnp.where(rowt < 2 * N_HEADS, zgt, jax.nn.log_sigmoid(zgt))


def _in_proj(x, nw, wqk, wvt, wo, wgt, wu, ws, cw, cb, gbt, snw, tm):
    B, T, D = x.shape
    d_a = wvt.shape[0]
    d_b = wu.shape[1]
    ng = wgt.shape[0]
    nt = T // tm
    hb8 = tm // SUBLANES
    last8 = T // SUBLANES - 1
    full = lambda a: pl.BlockSpec(a.shape, lambda b, i: (0,) * a.ndim)
    tok = lambda w: pl.BlockSpec((1, tm, w), lambda b, i: (b, i, 0))
    in_specs = [
        pl.BlockSpec((1, tm, D), lambda b, i: (b, i, 0)),
        pl.BlockSpec((1, SUBLANES, D), lambda b, i: (b, jnp.maximum(i * hb8 - 1, 0), 0)),
        pl.BlockSpec((1, SUBLANES, D), lambda b, i: (b, jnp.minimum((i + 1) * hb8, last8), 0)),
    ] + [full(a) for a in (nw, wqk, wvt, wo, wgt, wu, ws, cw, cb, gbt, snw)]
    out_shape = (
        jax.ShapeDtypeStruct((B, T, d_a), BF16), jax.ShapeDtypeStruct((B, T, d_a), BF16),
        jax.ShapeDtypeStruct((B, d_a, T), BF16), jax.ShapeDtypeStruct((B, T, d_a), BF16),
        jax.ShapeDtypeStruct((B, ng, T), F32),
        jax.ShapeDtypeStruct((B, T, d_b), BF16), jax.ShapeDtypeStruct((B, T, d_b), BF16),
    )
    out_specs = (tok(d_a), tok(d_a), pl.BlockSpec((1, d_a, tm), lambda b, i: (b, 0, i)), tok(d_a),
                 pl.BlockSpec((1, ng, tm), lambda b, i: (b, 0, i)), tok(d_b), tok(d_b))
    return pl.pallas_call(
        _in_proj_kernel, out_shape=out_shape, grid=(B, nt), in_specs=in_specs,
        out_specs=out_specs, compiler_params=_cparams(("parallel", "arbitrary")),
        name="in_proj")(x, x, x, nw, wqk, wvt, wo, wgt, wu, ws, cw, cb, gbt, snw)


def _split3(x):
    x0 = x.astype(BF16)
    r1 = x - x0.astype(F32)
    x1 = r1.astype(BF16)
    x2 = (r1 - x1.astype(F32)).astype(BF16)
    return x0, x1, x2


def _mlstm_kernel(qf_ref, kf_ref, vtf_ref, gtf_ref, qb_ref, kb_ref, vtb_ref, gtb_ref,
                  hf_ref, hb_ref, c_st, m_st):
    c = pl.program_id(1)
    L = qf_ref.shape[1]
    d = HEAD_DIM

    @pl.when(c == 0)
    def _():
        c_st[...] = jnp.zeros_like(c_st)
        m_st[...] = jnp.zeros_like(m_st)

    r0 = lax.broadcasted_iota(I32, (L, L), 0)
    r1 = lax.broadcasted_iota(I32, (L, L), 1)
    ones8 = jnp.ones((SUBLANES, L), BF16)
    ng = 2 * N_HEADS
    dirs = ((qf_ref, kf_ref, vtf_ref, gtf_ref[0], r0 <= r1, hf_ref),
            (qb_ref, kb_ref, vtb_ref, gtb_ref[0], r0 >= r1, hb_ref))

    chains = []
    for dr, (q_ref, k_ref, vt_ref, gt, mask_st, h_ref) in enumerate(dirs):
        g3 = jnp.concatenate(_split3(gt), axis=0)
        b3 = _dot(g3, mask_st.astype(BF16))
        nr = gt.shape[0]
        br_all = b3[:nr] + b3[nr:2 * nr] + b3[2 * nr:]
        a_rows = gt[:ng] - br_all[ng:]
        a_cols = jnp.concatenate([a_rows, jnp.zeros((L - ng, L), F32)], axis=0).T
        for hd in range(N_HEADS):
            j = dr * N_HEADS + hd
            hs = slice(hd * d, (hd + 1) * d)
            chains.append(dict(
                j=j, hs=hs, h_ref=h_ref, mask=mask_st, qb=q_ref[0, :, hs], kb=k_ref[0, :, hs],
                vt_aug=jnp.concatenate([vt_ref[0, hs, :], ones8], axis=0),
                a_col=a_cols[:, j:j + 1], i_row=gt[j:j + 1, :], b_row=br_all[ng + j:ng + j + 1, :],
                b_tot=jnp.sum(gt[ng + j:ng + j + 1, :], axis=1, keepdims=True),
                caug=c_st[j], m_prev=m_st[j]))

    for ch in chains:
        ch["s_raw"] = lax.dot_general(ch["kb"], ch["qb"], _NT, preferred_element_type=F32)
        ch["ia"] = lax.dot_general(ch["caug"].astype(BF16), ch["qb"], _NT, preferred_element_type=F32)
    for ch in chains:
        dmat = jnp.where(ch["mask"], ch["a_col"] + ch["b_row"], NEG_BIG)
        inter = ch["b_row"] + ch["m_prev"]
        m_t = jnp.maximum(jnp.max(dmat, axis=0, keepdims=True), inter)
        st = ch["s_raw"] * jnp.exp(dmat - m_t)
        w_inter = jnp.exp(inter - m_t)
        den = jnp.sum(st, axis=0, keepdims=True) + w_inter * ch["ia"][d:d + 1]
        ch["st"] = st.astype(BF16)
        ch["w_inter"] = w_inter
        ch["rden"] = 1.0 / jnp.maximum(jnp.abs(den), jnp.exp(-m_t))
    for ch in chains:
        num = _dot(ch["vt_aug"][:d], ch["st"]) + ch["w_inter"] * ch["ia"][:d]
        ch["h_ref"][0, :, ch["hs"]] = (num * ch["rden"]).T
    for ch in chains:
        g_row = ch["b_tot"] - ch["b_row"] + ch["i_row"]
        m_new = jnp.maximum(ch["b_tot"] + ch["m_prev"], jnp.max(g_row, axis=1, keepdims=True))
        wk = jnp.exp(g_row - m_new)
        decay = jnp.exp(ch["b_tot"] + ch["m_prev"] - m_new)
        vw = (ch["vt_aug"].astype(F32) * wk).astype(BF16)
        c_st[ch["j"]] = decay * ch["caug"] + _dot(vw, ch["kb"])
        m_st[ch["j"]] = m_new


def _mlstm(q, k, vt, gt):
    B, T, d_a = q.shape
    L = CHUNK
    nc = T // L
    ng = gt.shape[1]
    fwd = lambda w: pl.BlockSpec((1, L, w), lambda b, c: (b, c, 0))
    bwd = lambda w: pl.BlockSpec((1, L, w), lambda b, c: (b, nc - 1 - c, 0))
    fwd_t = lambda r: pl.BlockSpec((1, r, L), lambda b, c: (b, 0, c))
    bwd_t = lambda r: pl.BlockSpec((1, r, L), lambda b, c: (b, 0, nc - 1 - c))
    in_specs = [fwd(d_a), fwd(d_a), fwd_t(d_a), fwd_t(ng), bwd(d_a), bwd(d_a), bwd_t(d_a), bwd_t(ng)]
    nch = 2 * N_HEADS
    return pl.pallas_call(
        _mlstm_kernel,
        out_shape=(jax.ShapeDtypeStruct((B, T, d_a), F32), jax.ShapeDtypeStruct((B, T, d_a), F32)),
        grid=(B, nc), in_specs=in_specs, out_specs=(fwd(d_a), bwd(d_a)),
        scratch_shapes=[pltpu.VMEM((nch, HEAD_DIM + SUBLANES, HEAD_DIM), F32),
                        pltpu.VMEM((nch, 1, 1), F32)],
        compiler_params=_cparams(("parallel", "arbitrary")),
        name="mlstm")(q, k, vt, gt, q, k, vt, gt)


MIX_SUB_ROWS = 256


def _mix_out_kernel(hf_ref, hb_ref, og_ref, u_ref, s_ref, x_ref, mnw_ref, sw_ref, sb_ref,
                    wout_ref, fnw_ref, rwt_ref, x1_ref, xn_ref, aff_ref):
    tm = x_ref.shape[0]
    d_a = og_ref.shape[1]
    d_b = u_ref.shape[1]
    sub = min(tm, MIX_SUB_ROWS)
    subs = [slice(i * sub, (i + 1) * sub) for i in range(tm // sub)]
    mnw = mnw_ref[...]
    sbias = sb_ref[...]

    gates = []
    for rs in subs:
        rows = []
        for cc in range(sub // CHUNK):
            r0 = rs.start + cc * CHUNK
            cols = [_dot(sw_ref[g], s_ref[r0:r0 + CHUNK, g * HEAD_DIM:(g + 1) * HEAD_DIM])
                    for g in range(d_b // HEAD_DIM)]
            rows.append(jnp.concatenate(cols, axis=1) + sbias)
        gates.append(jnp.concatenate(rows, axis=0))

    mixes = []
    for rs, gate in zip(subs, gates):
        h = hf_ref[rs, :] + hb_ref[rs, :]
        parts = [_rms(h[:, hd * HEAD_DIM:(hd + 1) * HEAD_DIM], mnw[:, hd * HEAD_DIM:(hd + 1) * HEAD_DIM])
                 for hd in range(d_a // HEAD_DIM)]
        a_out = (og_ref[rs, :].astype(F32) * jnp.concatenate(parts, axis=1)).astype(BF16)
        b_out = (u_ref[rs, :].astype(F32) * gate).astype(BF16)
        mixes.append(jnp.concatenate([a_out, b_out], axis=1))

    x1s = [x_ref[rs, :] + _dot(mix, wout_ref[...]) for rs, mix in zip(subs, mixes)]

    xns = []
    for rs, x1 in zip(subs, x1s):
        x1_ref[rs, :] = x1
        xn = _rms(x1, fnw_ref[...])
        xns.append(xn)
        chunks = xn.shape[1] // LANES
        for j in range(chunks):
            xn_ref[pl.ds(rs.start * chunks + j, sub, stride=chunks), :] = xn[:, j * LANES:(j + 1) * LANES]

    E = rwt_ref.shape[0]
    r0 = rwt_ref[...].astype(BF16)
    r1 = (rwt_ref[...] - r0.astype(F32)).astype(BF16)
    r01 = jnp.concatenate([r0, r1], axis=0)
    logits = []
    for xn in xns:
        x0 = xn.astype(BF16)
        x1 = (xn - x0.astype(F32)).astype(BF16)
        a = lax.dot_general(r01, x0, _NT, preferred_element_type=F32)
        b = lax.dot_general(r0, x1, _NT, preferred_element_type=F32)
        logits.append(a[:E] + a[E:] + b)
    for rs, lg in zip(subs, logits):
        ex = jnp.exp(lg - jnp.max(lg, axis=0, keepdims=True))
        aff = ex / jnp.sum(ex, axis=0, keepdims=True)
        for j in range(sub // LANES):
            aff_ref[rs.start // LANES + j] = aff[:, j * LANES:(j + 1) * LANES]


def _mix_out(hf, hb, og, u, s, x, mnw, sw, sbias, wout, fnw, rwt, tm):
    N, D = x.shape
    d_a = og.shape[1]
    d_b = u.shape[1]
    E = rwt.shape[0]
    nt = N // tm
    full = lambda a: pl.BlockSpec(a.shape, lambda i: (0,) * a.ndim)
    tok = lambda w: pl.BlockSpec((tm, w), lambda i: (i, 0))
    in_specs = [tok(d_a), tok(d_a), tok(d_a), tok(d_b), tok(d_b), tok(D)] + [
        full(a) for a in (mnw, sw, sbias, wout, fnw, rwt)]
    chunks = D // LANES
    out_shape = (jax.ShapeDtypeStruct((N, D), F32), jax.ShapeDtypeStruct((N * chunks, LANES), F32),
                 jax.ShapeDtypeStruct((N // LANES, E, LANES), F32))
    out_specs = (tok(D), pl.BlockSpec((tm * chunks, LANES), lambda i: (i, 0)),
                 pl.BlockSpec((tm // LANES, E, LANES), lambda i: (i, 0, 0)))
    return pl.pallas_call(
        _mix_out_kernel, out_shape=out_shape, grid=(nt,), in_specs=in_specs, out_specs=out_specs,
        compiler_params=_cparams(("parallel",)), name="mix_out")(
            hf, hb, og, u, s, x, mnw, sw, sbias, wout, fnw, rwt)


def _select_kernel(aff_ref, posm_ref, off_ref, cnt_s, wi_s, *, cap):
    nb, E, _ = aff_ref.shape
    aff = aff_ref[...]

    def count_ge(cand):
        c = jnp.sum((aff >= cand).astype(I32), axis=0, keepdims=True)
        return jnp.sum(c, axis=2, keepdims=True)

    def bit_step(i, thr_bits):
        cand = thr_bits | jnp.left_shift(jnp.int32(1), 30 - i)
        return jnp.where(count_ge(pltpu.bitcast(cand, F32)) >= cap, cand, thr_bits)

    thr = pltpu.bitcast(lax.fori_loop(0, 31, bit_step, jnp.zeros((1, E, 1), I32)), F32)
    gt = aff > thr
    eq = aff == thr
    n_gt = jnp.sum(jnp.sum(gt.astype(I32), axis=0, keepdims=True), axis=2, keepdims=True)
    need = cap - n_gt

    li = lax.broadcasted_iota(I32, (LANES, LANES), 0)
    lj = lax.broadcasted_iota(I32, (LANES, LANES), 1)
    upper = (li < lj).astype(BF16)

    def excl_cumsum(flag):
        fb = flag.astype(BF16).reshape(nb * E, LANES)
        wi_s[...] = _dot(fb, upper).astype(I32).reshape(nb, E, LANES)
        cnt_s[...] = jnp.sum(flag.astype(I32), axis=2, keepdims=True)

        def blk(b, run):
            wi_s[b] = wi_s[b] + run
            return run + cnt_s[b]

        lax.fori_loop(0, nb, blk, jnp.zeros((E, 1), I32))
        return wi_s[...]

    eq_rank = excl_cumsum(eq)
    sel = gt | (eq & (eq_rank < need))
    pos = excl_cumsum(sel)
    posm_ref[...] = jnp.where(sel, pos, -1)
    off_ref[...] = jnp.broadcast_to(pos[:, :, 0:1], off_ref.shape)


def _select(aff3, cap):
    nb, E, _ = aff3.shape
    return pl.pallas_call(
        functools.partial(_select_kernel, cap=cap),
        out_shape=(jax.ShapeDtypeStruct((nb, E, LANES), I32), jax.ShapeDtypeStruct((nb, E, LANES), I32)),
        scratch_shapes=[pltpu.VMEM((nb, E, 1), I32), pltpu.VMEM((nb, E, LANES), I32)],
        compiler_params=_cparams(None), name="select")(aff3)


def _compact_kernel(off_sm, posm_ref, aff_ref, acc_ref):
    nb, E, _ = posm_ref.shape
    acc_ref[...] = jnp.zeros_like(acc_ref)
    srow = lax.broadcasted_iota(I32, (2 * LANES, LANES), 0)
    r8 = lax.broadcasted_iota(I32, (SUBLANES, LANES), 0)
    lane8 = lax.broadcasted_iota(I32, (SUBLANES, LANES), 1)

    def blk(b, carry):
        tok = b * LANES + lane8
        t_hi = jnp.right_shift(tok, 8).astype(F32)
        t_lo = jnp.bitwise_and(tok, 255).astype(F32)
        pm = posm_ref[b]
        af = aff_ref[b]
        for e in range(E):
            off = off_sm[b * E + e]
            j0 = jnp.right_shift(off, 7)
            rel = pm[e:e + 1, :] - j0 * LANES
            onehot = (srow == rel).astype(BF16)
            a = af[e:e + 1, :]
            a0 = a.astype(BF16)
            r1 = a - a0.astype(F32)
            a1 = r1.astype(BF16)
            a2 = (r1 - a1.astype(F32)).astype(BF16)
            lhs = jnp.where(r8 == 0, t_hi, jnp.where(r8 == 1, t_lo, 0.0))
            lhs = jnp.where(r8 == 2, a0.astype(F32), lhs)
            lhs = jnp.where(r8 == 3, a1.astype(F32), lhs)
            lhs = jnp.where(r8 == 4, a2.astype(F32), lhs).astype(BF16)
            out = lax.dot_general(lhs, onehot, _NT, preferred_element_type=F32)
            acc_ref[e, j0] = acc_ref[e, j0] + out[:, :LANES]
            acc_ref[e, j0 + 1] = acc_ref[e, j0 + 1] + out[:, LANES:]
        return carry

    lax.fori_loop(0, nb, blk, 0)


def _compact(off_flat, posm3, aff3, cap):
    nb, E, _ = posm3.shape
    nt_pad = cap // LANES + 2
    gs = pltpu.PrefetchScalarGridSpec(
        num_scalar_prefetch=1, grid=(1,),
        in_specs=[pl.BlockSpec(posm3.shape, lambda i, o: (0, 0, 0)),
                  pl.BlockSpec(aff3.shape, lambda i, o: (0, 0, 0))],
        out_specs=pl.BlockSpec((E, nt_pad, SUBLANES, LANES), lambda i, o: (0, 0, 0, 0)))
    return pl.pallas_call(
        _compact_kernel, out_shape=jax.ShapeDtypeStruct((E, nt_pad, SUBLANES, LANES), F32),
        grid_spec=gs, compiler_params=_cparams(("arbitrary",)), name="compact")(off_flat, posm3, aff3)


def _ffn_kernel(idc_sm, idn_sm, xn_hbm, cacc_ref, wg_ref, wu_ref, wd_ref, y_ref, xbuf, sem,
                *, n_real, fc):
    s = pl.program_id(0)
    ts = y_ref.shape[0]
    chunks = xbuf.shape[1] // ts
    slot = lax.rem(s, 2)

    def row_copy(idx_sm, r, dst_slot):
        src = pl.multiple_of(idx_sm[0, 0, r] * chunks, chunks)
        return pltpu.make_async_copy(xn_hbm.at[pl.ds(src, chunks), :],
                                     xbuf.at[dst_slot, pl.ds(r * chunks, chunks), :], sem.at[dst_slot])

    @pl.when(s == 0)
    def _():
        def body(r, carry):
            row_copy(idc_sm, r, 0).start()
            return carry
        lax.fori_loop(0, ts, body, 0, unroll=8)

    pltpu.make_async_copy(xn_hbm.at[pl.ds(0, ts * chunks), :], xbuf.at[slot], sem.at[slot]).wait()

    @pl.when(s < n_real)
    def _():
        xs = xbuf.at[slot]
        x = jnp.concatenate([xs[pl.ds(j, ts, stride=chunks), :] for j in range(chunks)],
                            axis=1).astype(BF16)
        F = wg_ref.shape[2]
        nchunk = F // fc
        rows_per_chunk = ts // nchunk
        acc = jnp.zeros((ts, wd_ref.shape[2]), F32)
        for c in range(nchunk):
            for r in range(c * rows_per_chunk, (c + 1) * rows_per_chunk):
                row_copy(idn_sm, r, 1 - slot).start(priority=r % 2)
            cs = slice(c * fc, (c + 1) * fc)
            gte = _dot(x, wg_ref[0, :, cs])
            up = _dot(x, wu_ref[0, :, cs])
            hid = (gte * _sigmoid(gte) * up).astype(BF16)
            acc = acc + _dot(hid, wd_ref[0, cs, :])
        for g in range(ts // LANES):
            t = cacc_ref[0, g]
            vrow = t[2:3, :] + t[3:4, :] + t[4:5, :]
            vmat = jnp.broadcast_to(vrow, (LANES, LANES)).T
            vfull = jnp.concatenate([vmat] * (acc.shape[1] // LANES), axis=1)
            rs = slice(g * LANES, (g + 1) * LANES)
            y_ref[rs, :] = (acc[rs, :] * vfull).astype(y_ref.dtype)

    @pl.when(s >= n_real)
    def _():
        y_ref[...] = jnp.zeros_like(y_ref)


def _ffn(idx3, xn, cacc, wg, wu, wd, cap, ts):
    E, D, F = wg.shape
    nts = cap // ts
    n_real = E * nts
    last = n_real - 1
    eidx = lambda s: jnp.minimum(s // nts, E - 1)
    in_specs = [
        pl.BlockSpec((1, 1, ts), lambda s: (jnp.minimum(s, last), 0, 0), memory_space=pltpu.SMEM),
        pl.BlockSpec((1, 1, ts), lambda s: (jnp.minimum(s + 1, last), 0, 0), memory_space=pltpu.SMEM),
        pl.BlockSpec(memory_space=pl.ANY),
        pl.BlockSpec((1, ts // LANES, SUBLANES, LANES),
                     lambda s: (eidx(s), lax.rem(jnp.minimum(s, last), nts), 0, 0)),
        pl.BlockSpec((1, D, F), lambda s: (eidx(s), 0, 0)),
        pl.BlockSpec((1, D, F), lambda s: (eidx(s), 0, 0)),
        pl.BlockSpec((1, F, D), lambda s: (eidx(s), 0, 0)),
    ]
    return pl.pallas_call(
        functools.partial(_ffn_kernel, n_real=n_real, fc=512),
        out_shape=jax.ShapeDtypeStruct(((n_real + 1) * ts, D), BF16),
        grid=(n_real + 1,), in_specs=in_specs,
        out_specs=pl.BlockSpec((ts, D), lambda s: (s, 0)),
        scratch_shapes=[pltpu.VMEM((2, ts * (D // LANES), LANES), F32), pltpu.SemaphoreType.DMA((2,))],
        compiler_params=_cparams(("arbitrary",)), name="ffn")(idx3, idx3, xn, cacc, wg, wu, wd)


WIN = 64


def _combine_kernel(off_sm, posm_ref, x1_ref, y_hbm, fnw_ref, o_ref, ycat, yext, sem, sem_ext,
                    *, cap, y_rows, nblk):
    b = pl.program_id(0)
    slot = lax.rem(b, 2)
    nsub, E, _ = posm_ref.shape
    tb = x1_ref.shape[0]
    pm = jnp.concatenate([posm_ref[j] for j in range(nsub)], axis=1)
    wrow = lax.broadcasted_iota(I32, (WIN, tb), 0)

    def starts_of(blk, r):
        out = []
        for e in range(E):
            base = jnp.left_shift(jnp.right_shift(off_sm[blk * E + e], 4), 4)
            st = jnp.minimum(e * cap + base + r * WIN, y_rows - WIN)
            out.append(pl.multiple_of(st, BF16_ROWS))
        return out

    def copies(starts, dst, dsem):
        return [pltpu.make_async_copy(y_hbm.at[pl.ds(starts[e], WIN), :], dst.at[pl.ds(e * WIN, WIN), :], dsem)
                for e in range(E)]

    def onehot(starts):
        ps = []
        for e in range(E):
            pe = pm[e:e + 1, :]
            rel = jnp.where(pe >= 0, pe + (e * cap - starts[e]), -1)
            ps.append((wrow == rel).astype(BF16))
        return jnp.concatenate(ps, axis=0)

    @pl.when(b == 0)
    def _():
        for cp in copies(starts_of(b, 0), ycat.at[0], sem.at[0]):
            cp.start()

    @pl.when(b + 1 < nblk)
    def _():
        for cp in copies(starts_of(b + 1, 0), ycat.at[1 - slot], sem.at[1 - slot]):
            cp.start()

    starts0 = starts_of(b, 0)
    p0 = onehot(starts0)
    for cp in copies(starts0, ycat.at[slot], sem.at[slot]):
        cp.wait()
    acc0 = lax.dot_general(p0, ycat[slot], _TN, preferred_element_type=F32)

    nrounds = jnp.int32(1)
    for e in range(E):
        base = jnp.left_shift(jnp.right_shift(off_sm[b * E + e], 4), 4)
        nrounds = jnp.maximum(nrounds, jnp.right_shift(off_sm[(b + 1) * E + e] - base + (WIN - 1), 6))

    def round_body(r, acc):
        starts = starts_of(b, r)
        cps = copies(starts, yext, sem_ext)
        for cp in cps:
            cp.start()
        p = onehot(starts)
        for cp in cps:
            cp.wait()
        return acc + lax.dot_general(p, yext[...], _TN, preferred_element_type=F32)

    acc = lax.fori_loop(1, nrounds, round_body, acc0)
    o_ref[...] = _rms(x1_ref[...] + acc, fnw_ref[...])


def _combine(off_flat, posm3, x1, y, fnw, cap, tb):
    N, D = x1.shape
    nb, E, _ = posm3.shape
    nsub = tb // LANES
    y_rows = y.shape[0]
    gs = pltpu.PrefetchScalarGridSpec(
        num_scalar_prefetch=1, grid=(N // tb,),
        in_specs=[pl.BlockSpec((nsub, E, LANES), lambda i, o: (i, 0, 0)),
                  pl.BlockSpec((tb, D), lambda i, o: (i, 0)),
                  pl.BlockSpec(memory_space=pl.ANY),
                  pl.BlockSpec(fnw.shape, lambda i, o: (0, 0))],
        out_specs=pl.BlockSpec((tb, D), lambda i, o: (i, 0)),
        scratch_shapes=[pltpu.VMEM((2, E * WIN, D), BF16), pltpu.VMEM((E * WIN, D), BF16),
                        pltpu.SemaphoreType.DMA((2,)), pltpu.SemaphoreType.DMA])
    return pl.pallas_call(
        functools.partial(_combine_kernel, cap=cap, y_rows=y_rows, nblk=N // tb),
        out_shape=jax.ShapeDtypeStruct((N, D), F32), grid_spec=gs,
        compiler_params=_cparams(("arbitrary",)), name="combine")(off_flat, posm3, x1, y, fnw)


def _prep_params(norm_mix_w, w_in, conv_w, conv_b, gate_b, mlstm_norm_w, sgu_norm_w, sgu_w, sgu_b,
                 w_out, norm_ffn_w, router_w, w_gate, w_up, w_down, norm_final_w):
    d_a = mlstm_norm_w.shape[1]
    d_b = sgu_norm_w.shape[1]
    ng = gate_b.shape[1]
    w = w_in[0]
    o0, o1, o2, o3, o4, o5 = 2 * d_a, 3 * d_a, 4 * d_a, 4 * d_a + ng, 4 * d_a + ng + d_b, 4 * d_a + ng + 2 * d_b
    wg = w[:, o2:o3]
    return dict(
        nw=norm_mix_w[0][None, :],
        wqk=w[:, :o0].astype(BF16), wvt=w[:, o0:o1].T.astype(BF16), wo=w[:, o1:o2].astype(BF16),
        wgt=wg.T.astype(BF16),
        wu=w[:, o3:o4].astype(BF16), ws=w[:, o4:o5].astype(BF16),
        cw=conv_w[0], cb=conv_b[0][None, :], gbt=gate_b[0][:, None],
        snw=sgu_norm_w[0][None, :], mnw=mlstm_norm_w[0][None, :],
        sw=sgu_w[0].astype(BF16),
        sbias=jnp.repeat(sgu_b[0].T, HEAD_DIM, axis=1),
        wout=w_out[0].astype(BF16), fnw=norm_ffn_w[0][None, :], rwt=router_w[0].T,
        wgate=w_gate[0].astype(BF16), wup=w_up[0].astype(BF16), wdown=w_down[0].astype(BF16),
        nfw=norm_final_w[None, :],
    )


def _encoder(x, p):
    B, T, D = x.shape
    N = B * T
    E = N_EXPERTS
    cap = (N * CAPACITY_FACTOR) // E
    tm1 = min(512, T)
    q, k, vt, og, gt, u, s = _in_proj(
        x, p["nw"], p["wqk"], p["wvt"], p["wo"], p["wgt"], p["wu"], p["ws"],
        p["cw"], p["cb"], p["gbt"], p["snw"], tm1)
    hf, hb = _mlstm(q, k, vt, gt)
    flat = lambda a: a.reshape(N, a.shape[-1])
    x1, xn, aff3 = _mix_out(flat(hf), flat(hb), flat(og), flat(u), flat(s), flat(x),
                            p["mnw"], p["sw"], p["sbias"], p["wout"], p["fnw"], p["rwt"], min(512, N))
    posm3, off3 = _select(aff3, cap)
    nb = N // LANES
    off_flat = off3[:, :, 0].reshape(nb * E)
    cacc = _compact(off_flat, posm3, aff3, cap)
    nt = cap // LANES
    idx = (cacc[:, :nt, 0, :] * 256.0 + cacc[:, :nt, 1, :]).astype(I32).reshape(E * cap)
    ts = min(512, cap)
    y = _ffn(idx.reshape(E * cap // ts, 1, ts), xn, cacc, p["wgate"], p["wup"], p["wdown"], cap, ts)
    tb = min(256, N)
    sub = tb // LANES
    off_tb = jnp.concatenate([off3[::sub, :, 0], jnp.full((1, E), cap, I32)], axis=0).reshape(-1)
    out = _combine(off_tb, posm3, x1, y, p["nfw"], cap, tb)
    return out.reshape(B, T, D)


def kernel(x_prompt, x_sample, norm_mix_w, w_in, conv_w, conv_b, gate_b, mlstm_norm_w, sgu_norm_w,
           sgu_w, sgu_b, w_out, norm_ffn_w, router_w, w_gate, w_up, w_down, norm_final_w):
    p = _prep_params(norm_mix_w, w_in, conv_w, conv_b, gate_b, mlstm_norm_w, sgu_norm_w, sgu_w,
                     sgu_b, w_out, norm_ffn_w, router_w, w_gate, w_up, w_down, norm_final_w)
    return (_encoder(x_prompt, p), _encoder(x_sample, p))
```

```python
import functools
import math

import jax
import jax.numpy as jnp
from jax import lax
from jax.experimental import pallas as pl
from jax.experimental.pallas import tpu as pltpu

F32 = jnp.float32
BF16 = jnp.bfloat16
I32 = jnp.int32

EPS = 1e-6
N_HEADS = 4
HEAD_DIM = 128
CHUNK = 128
N_EXPERTS = 16
CAPACITY_FACTOR = 2
LANES = 128
SUBLANES = 8
BF16_ROWS = 16
NEG_BIG = -1e30
VMEM_LIMIT = 48 * 1024 * 1024

_NT = (((1,), (1,)), ((), ()))
_TN = (((0,), (0,)), ((), ()))


def _cparams(sem, vmem=VMEM_LIMIT):
    return pltpu.CompilerParams(dimension_semantics=sem, vmem_limit_bytes=vmem)


def _dot(a, b):
    return jnp.dot(a, b, preferred_element_type=F32)


def _sigmoid(x):
    return 1.0 / (1.0 + jnp.exp(-x))


def _gelu(x):
    return 0.5 * x * (1.0 + lax.erf(x * (1.0 / math.sqrt(2.0))))


def _rms(x, w):
    ms = jnp.mean(x * x, axis=-1, keepdims=True)
    return x * lax.rsqrt(ms + EPS) * w


PROJ_COLS = 256


def _in_proj_kernel(x_ref, xp_ref, xn_ref, nw_ref, wqk_ref, wvt_ref, wo_ref, wgt_ref,
                    wu_ref, ws_ref, cw_ref, cb_ref, gbt_ref, snw_ref,
                    q_ref, k_ref, vt_ref, og_ref, gt_ref, u_ref, s_ref):
    i = pl.program_id(1)
    n_i = pl.num_programs(1)
    tm = x_ref.shape[1]
    d_a = q_ref.shape[2]
    nw = nw_ref[...]
    hb = _rms(x_ref[0], nw).astype(BF16)
    hp = jnp.where(i == 0, 0.0, _rms(xp_ref[0], nw)).astype(BF16)
    hn = jnp.where(i == n_i - 1, 0.0, _rms(xn_ref[0], nw)).astype(BF16)
    h_ext = jnp.concatenate([hb, hp, hn], axis=0)
    cw = cw_ref[...]
    cb = cb_ref[...]
    snw = snw_ref[...]
    row = lax.broadcasted_iota(I32, (tm, PROJ_COLS), 0)

    def qk_tile(j):
        cs = slice(j * PROJ_COLS, (j + 1) * PROJ_COLS)
        ze = _dot(h_ext, wqk_ref[:, cs])
        z = ze[:tm]
        zp = ze[tm + SUBLANES - 1:tm + SUBLANES]
        zn = ze[tm + SUBLANES:tm + SUBLANES + 1]
        z_prev = jnp.where(row == 0, zp, pltpu.roll(z, 1, axis=0))
        z_next = jnp.where(row == tm - 1, zn, pltpu.roll(z, tm - 1, axis=0))
        conv = cb[:, cs] + cw[0:1, cs] * z_prev + cw[1:2, cs] * z + cw[2:3, cs] * z_next
        qk = conv * _sigmoid(conv)
        if cs.start < d_a:
            q_ref[0, :, cs] = qk.astype(BF16)
        else:
            ks = slice(cs.start - d_a, cs.stop - d_a)
            k_ref[0, :, ks] = (qk * (1.0 / math.sqrt(HEAD_DIM))).astype(BF16)

    def u_tile(j):
        cs = slice(j * PROJ_COLS, (j + 1) * PROJ_COLS)
        u_ref[0, :, cs] = _gelu(_dot(hb, wu_ref[:, cs])).astype(BF16)

    def s_tile(j):
        cs = slice(j * PROJ_COLS, (j + 1) * PROJ_COLS)
        sv = _gelu(_dot(hb, ws_ref[:, cs]))
        for g in range(PROJ_COLS // HEAD_DIM):
            gs = slice(g * HEAD_DIM, (g + 1) * HEAD_DIM)
            og_cols = slice(cs.start + gs.start, cs.start + gs.stop)
            s_ref[0, :, og_cols] = _rms(sv[:, gs], snw[:, og_cols]).astype(BF16)

    def o_tile(j):
        cs = slice(j * PROJ_COLS, (j + 1) * PROJ_COLS)
        og_ref[0, :, cs] = _sigmoid(_dot(hb, wo_ref[:, cs])).astype(BF16)

    def vt_tile(j):
        cs = slice(j * PROJ_COLS, (j + 1) * PROJ_COLS)
        vt_ref[0, cs, :] = lax.dot_general(wvt_ref[cs, :], hb, _NT, preferred_element_type=F32).astype(BF16)

    n_qk = wqk_ref.shape[1] // PROJ_COLS
    n_b = wu_ref.shape[1] // PROJ_COLS
    n_a = wo_ref.shape[1] // PROJ_COLS
    light = [(u_tile, j) for j in range(n_b)] + [(s_tile, j) for j in range(n_b)]
    light = [light[(k // 2) + (k % 2) * n_b] for k in range(2 * n_b)]
    for j in range(n_qk):
        qk_tile(j)
        if j < len(light):
            fn, jj = light[j]
            fn(jj)
    for fn, jj in light[n_qk:]:
        fn(jj)
    for j in range(n_a):
        o_tile(j)
        vt_tile(j)

    zgt = lax.dot_general(wgt_ref[...], hb, _NT, preferred_element_type=F32) + gbt_ref[...]
    rowt = lax.broadcasted_iota(I32, zgt.shape, 0)
    gt_ref[0] = jnp.where(rowt < 2 * N_HEADS, zgt, jax.nn.log_sigmoid(zgt))


def _in_proj(x, nw, wqk, wvt, wo, wgt, wu, ws, cw, cb, gbt, snw, tm):
    B, T, D = x.shape
    d_a = wvt.shape[0]
    d_b = wu.shape[1]
    ng = wgt.shape[0]
    nt = T // tm
    hb8 = tm // SUBLANES
    last8 = T // SUBLANES - 1
    full = lambda a: pl.BlockSpec(a.shape, lambda b, i: (0,) * a.ndim)
    tok = lambda w: pl.BlockSpec((1, tm, w), lambda b, i: (b, i, 0))
    in_specs = [
        pl.BlockSpec((1, tm, D), lambda b, i: (b, i, 0)),
        pl.BlockSpec((1, SUBLANES, D), lambda b, i: (b, jnp.maximum(i * hb8 - 1, 0), 0)),
        pl.BlockSpec((1, SUBLANES, D), lambda b, i: (b, jnp.minimum((i + 1) * hb8, last8), 0)),
    ] + [full(a) for a in (nw, wqk, wvt, wo, wgt, wu, ws, cw, cb, gbt, snw)]
    out_shape = (
        jax.ShapeDtypeStruct((B, T, d_a), BF16), jax.ShapeDtypeStruct((B, T, d_a), BF16),
        jax.ShapeDtypeStruct((B, d_a, T), BF16), jax.ShapeDtypeStruct((B, T, d_a), BF16),
        jax.ShapeDtypeStruct((B, ng, T), F32),
        jax.ShapeDtypeStruct((B, T, d_b), BF16), jax.ShapeDtypeStruct((B, T, d_b), BF16),
    )
    out_specs = (tok(d_a), tok(d_a), pl.BlockSpec((1, d_a, tm), lambda b, i: (b, 0, i)), tok(d_a),
                 pl.BlockSpec((1, ng, tm), lambda b, i: (b, 0, i)), tok(d_b), tok(d_b))
    return pl.pallas_call(
        _in_proj_kernel, out_shape=out_shape, grid=(B, nt), in_specs=in_specs,
        out_specs=out_specs, compiler_params=_cparams(("parallel", "arbitrary")),
        name="in_proj")(x, x, x, nw, wqk, wvt, wo, wgt, wu, ws, cw, cb, gbt, snw)


def _split3(x):
    x0 = x.astype(BF16)
    r1 = x - x0.astype(F32)
    x1 = r1.astype(BF16)
    x2 = (r1 - x1.astype(F32)).astype(BF16)
    return x0, x1, x2


def _mlstm_kernel(*refs, n_cast):
    qf_ref, kf_ref, vtf_ref, gtf_ref, qb_ref, kb_ref, vtb_ref, gtb_ref = refs[:8]
    cast_in = refs[8:8 + n_cast]
    hf_ref, hb_ref = refs[8 + n_cast:10 + n_cast]
    cast_out = refs[10 + n_cast:10 + 2 * n_cast]
    c_st, m_st = refs[10 + 2 * n_cast:]
    for w_ref, o_ref in zip(cast_in, cast_out):
        o_ref[...] = w_ref[...].astype(BF16)

    c = pl.program_id(1)
    L = qf_ref.shape[1]
    d = HEAD_DIM

    @pl.when(c == 0)
    def _():
        c_st[...] = jnp.zeros_like(c_st)
        m_st[...] = jnp.zeros_like(m_st)

    r0 = lax.broadcasted_iota(I32, (L, L), 0)
    r1 = lax.broadcasted_iota(I32, (L, L), 1)
    ones8 = jnp.ones((SUBLANES, L), BF16)
    ng = 2 * N_HEADS
    dirs = ((qf_ref, kf_ref, vtf_ref, gtf_ref[0], r0 <= r1, hf_ref),
            (qb_ref, kb_ref, vtb_ref, gtb_ref[0], r0 >= r1, hb_ref))

    chains = []
    for dr, (q_ref, k_ref, vt_ref, gt, mask_st, h_ref) in enumerate(dirs):
        g3 = jnp.concatenate(_split3(gt), axis=0)
        b3 = _dot(g3, mask_st.astype(BF16))
        nr = gt.shape[0]
        br_all = b3[:nr] + b3[nr:2 * nr] + b3[2 * nr:]
        a_rows = gt[:ng] - br_all[ng:]
        a_cols = jnp.concatenate([a_rows, jnp.zeros((L - ng, L), F32)], axis=0).T
        for hd in range(N_HEADS):
            j = dr * N_HEADS + hd
            hs = slice(hd * d, (hd + 1) * d)
            chains.append(dict(
                j=j, hs=hs, h_ref=h_ref, mask=mask_st, qb=q_ref[0, :, hs], kb=k_ref[0, :, hs],
                vt_aug=jnp.concatenate([vt_ref[0, hs, :], ones8], axis=0),
                a_col=a_cols[:, j:j + 1], i_row=gt[j:j + 1, :], b_row=br_all[ng + j:ng + j + 1, :],
                b_tot=jnp.sum(gt[ng + j:ng + j + 1, :], axis=1, keepdims=True),
                caug=c_st[j], m_prev=m_st[j]))

    for ch in chains:
        ch["s_raw"] = lax.dot_general(ch["kb"], ch["qb"], _NT, preferred_element_type=F32)
        ch["ia"] = lax.dot_general(ch["caug"].astype(BF16), ch["qb"], _NT, preferred_element_type=F32)
    for ch in chains:
        dmat = jnp.where(ch["mask"], ch["a_col"] + ch["b_row"], NEG_BIG)
        inter = ch["b_row"] + ch["m_prev"]
        m_t = jnp.maximum(jnp.max(dmat, axis=0, keepdims=True), inter)
        st = ch["s_raw"] * jnp.exp(dmat - m_t)
        w_inter = jnp.exp(inter - m_t)
        den = jnp.sum(st, axis=0, keepdims=True) + w_inter * ch["ia"][d:d + 1]
        ch["st"] = st.astype(BF16)
        ch["w_inter"] = w_inter
        ch["rden"] = 1.0 / jnp.maximum(jnp.abs(den), jnp.exp(-m_t))
    for ch in chains:
        num = _dot(ch["vt_aug"][:d], ch["st"]) + ch["w_inter"] * ch["ia"][:d]
        ch["h_ref"][0, :, ch["hs"]] = (num * ch["rden"]).T
    for ch in chains:
        g_row = ch["b_tot"] - ch["b_row"] + ch["i_row"]
        m_new = jnp.maximum(ch["b_tot"] + ch["m_prev"], jnp.max(g_row, axis=1, keepdims=True))
        wk = jnp.exp(g_row - m_new)
        decay = jnp.exp(ch["b_tot"] + ch["m_prev"] - m_new)
        vw = (ch["vt_aug"].astype(F32) * wk).astype(BF16)
        c_st[ch["j"]] = decay * ch["caug"] + _dot(vw, ch["kb"])
        m_st[ch["j"]] = m_new


CAST_SLAB_BYTES = 2 * 1024 * 1024


def _cast_slabs(w, nsteps):
    E, R, C = w.shape
    if (E * R) % nsteps:
        return None
    rows = (E * R) // nsteps
    if rows % BF16_ROWS or R % rows or rows * C * 4 > CAST_SLAB_BYTES:
        return None
    return w.reshape(nsteps, rows, C)


def _mlstm(q, k, vt, gt, cast_slabs=()):
    B, T, d_a = q.shape
    L = CHUNK
    nc = T // L
    ng = gt.shape[1]
    fwd = lambda w: pl.BlockSpec((1, L, w), lambda b, c: (b, c, 0))
    bwd = lambda w: pl.BlockSpec((1, L, w), lambda b, c: (b, nc - 1 - c, 0))
    fwd_t = lambda r: pl.BlockSpec((1, r, L), lambda b, c: (b, 0, c))
    bwd_t = lambda r: pl.BlockSpec((1, r, L), lambda b, c: (b, 0, nc - 1 - c))
    slab = lambda a: pl.BlockSpec((1,) + a.shape[1:], lambda b, c: (b * nc + c, 0, 0))
    in_specs = [fwd(d_a), fwd(d_a), fwd_t(d_a), fwd_t(ng), bwd(d_a), bwd(d_a), bwd_t(d_a), bwd_t(ng)]
    in_specs += [slab(a) for a in cast_slabs]
    out_shape = [jax.ShapeDtypeStruct((B, T, d_a), F32), jax.ShapeDtypeStruct((B, T, d_a), F32)]
    out_shape += [jax.ShapeDtypeStruct(a.shape, BF16) for a in cast_slabs]
    out_specs = [fwd(d_a), bwd(d_a)] + [slab(a) for a in cast_slabs]
    nch = 2 * N_HEADS
    outs = pl.pallas_call(
        functools.partial(_mlstm_kernel, n_cast=len(cast_slabs)),
        out_shape=tuple(out_shape), grid=(B, nc), in_specs=in_specs, out_specs=tuple(out_specs),
        scratch_shapes=[pltpu.VMEM((nch, HEAD_DIM + SUBLANES, HEAD_DIM), F32),
                        pltpu.VMEM((nch, 1, 1), F32)],
        compiler_params=_cparams(("parallel", "arbitrary")),
        name="mlstm")(q, k, vt, gt, q, k, vt, gt, *cast_slabs)
    return outs[0], outs[1], tuple(outs[2:])


MIX_SUB_ROWS = 256


def _mix_out_kernel(hf_ref, hb_ref, og_ref, u_ref, s_ref, x_ref, mnw_ref, sw_ref, sb_ref,
                    wout_ref, fnw_ref, rwt_ref, x1_ref, xn_ref, aff_ref):
    tm = x_ref.shape[0]
    d_a = og_ref.shape[1]
    d_b = u_ref.shape[1]
    sub = min(tm, MIX_SUB_ROWS)
    subs = [slice(i * sub, (i + 1) * sub) for i in range(tm // sub)]
    mnw = mnw_ref[...]
    sbias = sb_ref[...]

    gates = []
    for rs in subs:
        rows = []
        for cc in range(sub // CHUNK):
            r0 = rs.start + cc * CHUNK
            cols = [_dot(sw_ref[g], s_ref[r0:r0 + CHUNK, g * HEAD_DIM:(g + 1) * HEAD_DIM])
                    for g in range(d_b // HEAD_DIM)]
            rows.append(jnp.concatenate(cols, axis=1) + sbias)
        gates.append(jnp.concatenate(rows, axis=0))

    mixes = []
    for rs, gate in zip(subs, gates):
        h = hf_ref[rs, :] + hb_ref[rs, :]
        parts = [_rms(h[:, hd * HEAD_DIM:(hd + 1) * HEAD_DIM], mnw[:, hd * HEAD_DIM:(hd + 1) * HEAD_DIM])
                 for hd in range(d_a // HEAD_DIM)]
        a_out = (og_ref[rs, :].astype(F32) * jnp.concatenate(parts, axis=1)).astype(BF16)
        b_out = (u_ref[rs, :].astype(F32) * gate).astype(BF16)
        mixes.append(jnp.concatenate([a_out, b_out], axis=1))

    x1s = [x_ref[rs, :] + _dot(mix, wout_ref[...]) for rs, mix in zip(subs, mixes)]

    xns = []
    for rs, x1 in zip(subs, x1s):
        x1_ref[rs, :] = x1
        xn = _rms(x1, fnw_ref[...])
        xns.append(xn)
        chunks = xn.shape[1] // LANES
        for j in range(chunks):
            xn_ref[pl.ds(rs.start * chunks + j, sub, stride=chunks), :] = xn[:, j * LANES:(j + 1) * LANES]

    E = rwt_ref.shape[0]
    r0 = rwt_ref[...].astype(BF16)
    r1 = (rwt_ref[...] - r0.astype(F32)).astype(BF16)
    r01 = jnp.concatenate([r0, r1], axis=0)
    logits = []
    for xn in xns:
        x0 = xn.astype(BF16)
        x1 = (xn - x0.astype(F32)).astype(BF16)
        a = lax.dot_general(r01, x0, _NT, preferred_element_type=F32)
        b = lax.dot_general(r0, x1, _NT, preferred_element_type=F32)
        logits.append(a[:E] + a[E:] + b)
    for rs, lg in zip(subs, logits):
        ex = jnp.exp(lg - jnp.max(lg, axis=0, keepdims=True))
        aff = ex / jnp.sum(ex, axis=0, keepdims=True)
        for j in range(sub // LANES):
            aff_ref[rs.start // LANES + j] = aff[:, j * LANES:(j + 1) * LANES]


def _mix_out(hf, hb, og, u, s, x, mnw, sw, sbias, wout, fnw, rwt, tm):
    N, D = x.shape
    d_a = og.shape[1]
    d_b = u.shape[1]
    E = rwt.shape[0]
    nt = N // tm
    full = lambda a: pl.BlockSpec(a.shape, lambda i: (0,) * a.ndim)
    tok = lambda w: pl.BlockSpec((tm, w), lambda i: (i, 0))
    in_specs = [tok(d_a), tok(d_a), tok(d_a), tok(d_b), tok(d_b), tok(D)] + [
        full(a) for a in (mnw, sw, sbias, wout, fnw, rwt)]
    chunks = D // LANES
    out_shape = (jax.ShapeDtypeStruct((N, D), F32), jax.ShapeDtypeStruct((N * chunks, LANES), F32),
                 jax.ShapeDtypeStruct((N // LANES, E, LANES), F32))
    out_specs = (tok(D), pl.BlockSpec((tm * chunks, LANES), lambda i: (i, 0)),
                 pl.BlockSpec((tm // LANES, E, LANES), lambda i: (i, 0, 0)))
    return pl.pallas_call(
        _mix_out_kernel, out_shape=out_shape, grid=(nt,), in_specs=in_specs, out_specs=out_specs,
        compiler_params=_cparams(("parallel",)), name="mix_out")(
            hf, hb, og, u, s, x, mnw, sw, sbias, wout, fnw, rwt)


def _select_kernel(aff_ref, posm_ref, off_ref, cnt_s, wi_s, *, cap):
    nb, E, _ = aff_ref.shape
    aff = aff_ref[...]

    def count_ge(cand):
        c = jnp.sum((aff >= cand).astype(I32), axis=0, keepdims=True)
        return jnp.sum(c, axis=2, keepdims=True)

    def bit_step(i, thr_bits):
        cand = thr_bits | jnp.left_shift(jnp.int32(1), 30 - i)
        return jnp.where(count_ge(pltpu.bitcast(cand, F32)) >= cap, cand, thr_bits)

    thr = pltpu.bitcast(lax.fori_loop(0, 31, bit_step, jnp.zeros((1, E, 1), I32)), F32)
    gt = aff > thr
    eq = aff == thr
    n_gt = jnp.sum(jnp.sum(gt.astype(I32), axis=0, keepdims=True), axis=2, keepdims=True)
    need = cap - n_gt

    li = lax.broadcasted_iota(I32, (LANES, LANES), 0)
    lj = lax.broadcasted_iota(I32, (LANES, LANES), 1)
    upper = (li < lj).astype(BF16)

    def excl_cumsum(flag):
        fb = flag.astype(BF16).reshape(nb * E, LANES)
        wi_s[...] = _dot(fb, upper).astype(I32).reshape(nb, E, LANES)
        cnt_s[...] = jnp.sum(flag.astype(I32), axis=2, keepdims=True)

        def blk(b, run):
            wi_s[b] = wi_s[b] + run
            return run + cnt_s[b]

        lax.fori_loop(0, nb, blk, jnp.zeros((E, 1), I32))
        return wi_s[...]

    eq_rank = excl_cumsum(eq)
    sel = gt | (eq & (eq_rank < need))
    pos = excl_cumsum(sel)
    posm_ref[...] = jnp.where(sel, pos, -1)
    off_ref[...] = jnp.broadcast_to(pos[:, :, 0:1], off_ref.shape)


def _select(aff3, cap):
    nb, E, _ = aff3.shape
    return pl.pallas_call(
        functools.partial(_select_kernel, cap=cap),
        out_shape=(jax.ShapeDtypeStruct((nb, E, LANES), I32), jax.ShapeDtypeStruct((nb, E, LANES), I32)),
        scratch_shapes=[pltpu.VMEM((nb, E, 1), I32), pltpu.VMEM((nb, E, LANES), I32)],
        compiler_params=_cparams(None), name="select")(aff3)


def _compact_kernel(off_sm, posm_ref, aff_ref, acc_ref):
    nb, E, _ = posm_ref.shape
    acc_ref[...] = jnp.zeros_like(acc_ref)
    srow = lax.broadcasted_iota(I32, (2 * LANES, LANES), 0)
    r8 = lax.broadcasted_iota(I32, (SUBLANES, LANES), 0)
    lane8 = lax.broadcasted_iota(I32, (SUBLANES, LANES), 1)

    def blk(b, carry):
        tok = b * LANES + lane8
        t_hi = jnp.right_shift(tok, 8).astype(F32)
        t_lo = jnp.bitwise_and(tok, 255).astype(F32)
        pm = posm_ref[b]
        af = aff_ref[b]
        for e in range(E):
            off = off_sm[b * E + e]
            j0 = jnp.right_shift(off, 7)
            rel = pm[e:e + 1, :] - j0 * LANES
            onehot = (srow == rel).astype(BF16)
            a = af[e:e + 1, :]
            a0 = a.astype(BF16)
            r1 = a - a0.astype(F32)
            a1 = r1.astype(BF16)
            a2 = (r1 - a1.astype(F32)).astype(BF16)
            lhs = jnp.where(r8 == 0, t_hi, jnp.where(r8 == 1, t_lo, 0.0))
            lhs = jnp.where(r8 == 2, a0.astype(F32), lhs)
            lhs = jnp.where(r8 == 3, a1.astype(F32), lhs)
            lhs = jnp.where(r8 == 4, a2.astype(F32), lhs).astype(BF16)
            out = lax.dot_general(lhs, onehot, _NT, preferred_element_type=F32)
            acc_ref[e, j0] = acc_ref[e, j0] + out[:, :LANES]
            acc_ref[e, j0 + 1] = acc_ref[e, j0 + 1] + out[:, LANES:]
        return carry

    lax.fori_loop(0, nb, blk, 0)


def _compact(off_flat, posm3, aff3, cap):
    nb, E, _ = posm3.shape
    nt_pad = cap // LANES + 2
    gs = pltpu.PrefetchScalarGridSpec(
        num_scalar_prefetch=1, grid=(1,),
        in_specs=[pl.BlockSpec(posm3.shape, lambda i, o: (0, 0, 0)),
                  pl.BlockSpec(aff3.shape, lambda i, o: (0, 0, 0))],
        out_specs=pl.BlockSpec((E, nt_pad, SUBLANES, LANES), lambda i, o: (0, 0, 0, 0)))
    return pl.pallas_call(
        _compact_kernel, out_shape=jax.ShapeDtypeStruct((E, nt_pad, SUBLANES, LANES), F32),
        grid_spec=gs, compiler_params=_cparams(("arbitrary",)), name="compact")(off_flat, posm3, aff3)


def _ffn_kernel(idc_sm, idn_sm, xn_hbm, cacc_ref, wg_ref, wu_ref, wd_ref, y_ref, xbuf, sem,
                *, n_real, fc):
    s = pl.program_id(0)
    ts = y_ref.shape[0]
    chunks = xbuf.shape[1] // ts
    slot = lax.rem(s, 2)

    def row_copy(idx_sm, r, dst_slot):
        src = pl.multiple_of(idx_sm[0, 0, r] * chunks, chunks)
        return pltpu.make_async_copy(xn_hbm.at[pl.ds(src, chunks), :],
                                     xbuf.at[dst_slot, pl.ds(r * chunks, chunks), :], sem.at[dst_slot])

    @pl.when(s == 0)
    def _():
        def body(r, carry):
            row_copy(idc_sm, r, 0).start()
            return carry
        lax.fori_loop(0, ts, body, 0, unroll=8)

    pltpu.make_async_copy(xn_hbm.at[pl.ds(0, ts * chunks), :], xbuf.at[slot], sem.at[slot]).wait()

    @pl.when(s < n_real)
    def _():
        xs = xbuf.at[slot]
        x = jnp.concatenate([xs[pl.ds(j, ts, stride=chunks), :] for j in range(chunks)],
                            axis=1).astype(BF16)
        F = wg_ref.shape[2]
        nchunk = F // fc
        rows_per_chunk = ts // nchunk
        acc = jnp.zeros((ts, wd_ref.shape[2]), F32)
        for c in range(nchunk):
            for r in range(c * rows_per_chunk, (c + 1) * rows_per_chunk):
                row_copy(idn_sm, r, 1 - slot).start(priority=r % 2)
            cs = slice(c * fc, (c + 1) * fc)
            gte = _dot(x, wg_ref[0, :, cs])
            up = _dot(x, wu_ref[0, :, cs])
            hid = (gte * _sigmoid(gte) * up).astype(BF16)
            acc = acc + _dot(hid, wd_ref[0, cs, :])
        for g in range(ts // LANES):
            t = cacc_ref[0, g]
            vrow = t[2:3, :] + t[3:4, :] + t[4:5, :]
            vmat = jnp.broadcast_to(vrow, (LANES, LANES)).T
            vfull = jnp.concatenate([vmat] * (acc.shape[1] // LANES), axis=1)
            rs = slice(g * LANES, (g + 1) * LANES)
            y_ref[rs, :] = (acc[rs, :] * vfull).astype(y_ref.dtype)

    @pl.when(s >= n_real)
    def _():
        y_ref[...] = jnp.zeros_like(y_ref)


def _ffn(idx3, xn, cacc, wg, wu, wd, cap, ts):
    E, D, F = wg.shape
    nts = cap // ts
    n_real = E * nts
    last = n_real - 1
    eidx = lambda s: jnp.minimum(s // nts, E - 1)
    in_specs = [
        pl.BlockSpec((1, 1, ts), lambda s: (jnp.minimum(s, last), 0, 0), memory_space=pltpu.SMEM),
        pl.BlockSpec((1, 1, ts), lambda s: (jnp.minimum(s + 1, last), 0, 0), memory_space=pltpu.SMEM),
        pl.BlockSpec(memory_space=pl.ANY),
        pl.BlockSpec((1, ts // LANES, SUBLANES, LANES),
                     lambda s: (eidx(s), lax.rem(jnp.minimum(s, last), nts), 0, 0)),
        pl.BlockSpec((1, D, F), lambda s: (eidx(s), 0, 0)),
        pl.BlockSpec((1, D, F), lambda s: (eidx(s), 0, 0)),
        pl.BlockSpec((1, F, D), lambda s: (eidx(s), 0, 0)),
    ]
    return pl.pallas_call(
        functools.partial(_ffn_kernel, n_real=n_real, fc=512),
        out_shape=jax.ShapeDtypeStruct(((n_real + 1) * ts, D), BF16),
        grid=(n_real + 1,), in_specs=in_specs,
        out_specs=pl.BlockSpec((ts, D), lambda s: (s, 0)),
        scratch_shapes=[pltpu.VMEM((2, ts * (D // LANES), LANES), F32), pltpu.SemaphoreType.DMA((2,))],
        compiler_params=_cparams(("arbitrary",)), name="ffn")(idx3, idx3, xn, cacc, wg, wu, wd)


WIN = 64


def _combine_kernel(off_sm, posm_ref, x1_ref, y_hbm, fnw_ref, o_ref, ycat, yext, sem, sem_ext,
                    *, cap, y_rows, nblk):
    b = pl.program_id(0)
    slot = lax.rem(b, 2)
    nsub, E, _ = posm_ref.shape
    tb = x1_ref.shape[0]
    pm = jnp.concatenate([posm_ref[j] for j in range(nsub)], axis=1)
    wrow = lax.broadcasted_iota(I32, (WIN, tb), 0)

    def starts_of(blk, r):
        out = []
        for e in range(E):
            base = jnp.left_shift(jnp.right_shift(off_sm[blk * E + e], 4), 4)
            st = jnp.minimum(e * cap + base + r * WIN, y_rows - WIN)
            out.append(pl.multiple_of(st, BF16_ROWS))
        return out

    def copies(starts, dst, dsem):
        return [pltpu.make_async_copy(y_hbm.at[pl.ds(starts[e], WIN), :], dst.at[pl.ds(e * WIN, WIN), :], dsem)
                for e in range(E)]

    def onehot(starts):
        ps = []
        for e in range(E):
            pe = pm[e:e + 1, :]
            rel = jnp.where(pe >= 0, pe + (e * cap - starts[e]), -1)
            ps.append((wrow == rel).astype(BF16))
        return jnp.concatenate(ps, axis=0)

    @pl.when(b == 0)
    def _():
        for cp in copies(starts_of(b, 0), ycat.at[0], sem.at[0]):
            cp.start()

    @pl.when(b + 1 < nblk)
    def _():
        for cp in copies(starts_of(b + 1, 0), ycat.at[1 - slot], sem.at[1 - slot]):
            cp.start()

    starts0 = starts_of(b, 0)
    p0 = onehot(starts0)
    for cp in copies(starts0, ycat.at[slot], sem.at[slot]):
        cp.wait()
    acc0 = lax.dot_general(p0, ycat[slot], _TN, preferred_element_type=F32)

    nrounds = jnp.int32(1)
    for e in range(E):
        base = jnp.left_shift(jnp.right_shift(off_sm[b * E + e], 4), 4)
        nrounds = jnp.maximum(nrounds, jnp.right_shift(off_sm[(b + 1) * E + e] - base + (WIN - 1), 6))

    def round_body(r, acc):
        starts = starts_of(b, r)
        cps = copies(starts, yext, sem_ext)
        for cp in cps:
            cp.start()
        p = onehot(starts)
        for cp in cps:
            cp.wait()
        return acc + lax.dot_general(p, yext[...], _TN, preferred_element_type=F32)

    acc = lax.fori_loop(1, nrounds, round_body, acc0)
    o_ref[...] = _rms(x1_ref[...] + acc, fnw_ref[...])


def _combine(off_flat, posm3, x1, y, fnw, cap, tb):
    N, D = x1.shape
    nb, E, _ = posm3.shape
    nsub = tb // LANES
    y_rows = y.shape[0]
    gs = pltpu.PrefetchScalarGridSpec(
        num_scalar_prefetch=1, grid=(N // tb,),
        in_specs=[pl.BlockSpec((nsub, E, LANES), lambda i, o: (i, 0, 0)),
                  pl.BlockSpec((tb, D), lambda i, o: (i, 0)),
                  pl.BlockSpec(memory_space=pl.ANY),
                  pl.BlockSpec(fnw.shape, lambda i, o: (0, 0))],
        out_specs=pl.BlockSpec((tb, D), lambda i, o: (i, 0)),
        scratch_shapes=[pltpu.VMEM((2, E * WIN, D), BF16), pltpu.VMEM((E * WIN, D), BF16),
                        pltpu.SemaphoreType.DMA((2,)), pltpu.SemaphoreType.DMA])
    return pl.pallas_call(
        functools.partial(_combine_kernel, cap=cap, y_rows=y_rows, nblk=N // tb),
        out_shape=jax.ShapeDtypeStruct((N, D), F32), grid_spec=gs,
        compiler_params=_cparams(("arbitrary",)), name="combine")(off_flat, posm3, x1, y, fnw)


def _prep_params(norm_mix_w, w_in, conv_w, conv_b, gate_b, mlstm_norm_w, sgu_norm_w, sgu_w, sgu_b,
                 w_out, norm_ffn_w, router_w, w_gate, w_up, w_down, norm_final_w):
    d_a = mlstm_norm_w.shape[1]
    d_b = sgu_norm_w.shape[1]
    ng = gate_b.shape[1]
    w = w_in[0]
    o0, o1, o2, o3, o4, o5 = 2 * d_a, 3 * d_a, 4 * d_a, 4 * d_a + ng, 4 * d_a + ng + d_b, 4 * d_a + ng + 2 * d_b
    wg = w[:, o2:o3]
    return dict(
        nw=norm_mix_w[0][None, :],
        wqk=w[:, :o0].astype(BF16), wvt=w[:, o0:o1].T.astype(BF16), wo=w[:, o1:o2].astype(BF16),
        wgt=wg.T.astype(BF16),
        wu=w[:, o3:o4].astype(BF16), ws=w[:, o4:o5].astype(BF16),
        cw=conv_w[0], cb=conv_b[0][None, :], gbt=gate_b[0][:, None],
        snw=sgu_norm_w[0][None, :], mnw=mlstm_norm_w[0][None, :],
        sw=sgu_w[0].astype(BF16),
        sbias=jnp.repeat(sgu_b[0].T, HEAD_DIM, axis=1),
        wout=w_out[0].astype(BF16), fnw=norm_ffn_w[0][None, :], rwt=router_w[0].T,
        nfw=norm_final_w[None, :],
    )


def _mixer(x, p, expert_weights):
    B, T, D = x.shape
    q, k, vt, og, gt, u, s = _in_proj(
        x, p["nw"], p["wqk"], p["wvt"], p["wo"], p["wgt"], p["wu"], p["ws"],
        p["cw"], p["cb"], p["gbt"], p["snw"], min(512, T))
    nsteps = B * (T // CHUNK)
    slabs = [_cast_slabs(w, nsteps) for w in expert_weights]
    hf, hb, cast = _mlstm(q, k, vt, gt, tuple(sl for sl in slabs if sl is not None))
    cast = list(cast)
    weights_bf16 = [w.astype(BF16) if sl is None else cast.pop(0).reshape(w.shape)
                    for w, sl in zip(expert_weights, slabs)]
    return (hf, hb, og, u, s), weights_bf16


def _moe_tail(x, mixed, p, wgate, wup, wdown):
    B, T, D = x.shape
    N = B * T
    E = N_EXPERTS
    cap = (N * CAPACITY_FACTOR) // E
    hf, hb, og, u, s = mixed
    flat = lambda a: a.reshape(N, a.shape[-1])
    x1, xn, aff3 = _mix_out(flat(hf), flat(hb), flat(og), flat(u), flat(s), flat(x),
                            p["mnw"], p["sw"], p["sbias"], p["wout"], p["fnw"], p["rwt"], min(512, N))
    posm3, off3 = _select(aff3, cap)
    nb = N // LANES
    off_flat = off3[:, :, 0].reshape(nb * E)
    cacc = _compact(off_flat, posm3, aff3, cap)
    nt = cap // LANES
    idx = (cacc[:, :nt, 0, :] * 256.0 + cacc[:, :nt, 1, :]).astype(I32).reshape(E * cap)
    ts = min(512, cap)
    y = _ffn(idx.reshape(E * cap // ts, 1, ts), xn, cacc, wgate, wup, wdown, cap, ts)
    tb = min(256, N)
    sub = tb // LANES
    off_tb = jnp.concatenate([off3[::sub, :, 0], jnp.full((1, E), cap, I32)], axis=0).reshape(-1)
    out = _combine(off_tb, posm3, x1, y, p["nfw"], cap, tb)
    return out.reshape(B, T, D)


def kernel(x_prompt, x_sample, norm_mix_w, w_in, conv_w, conv_b, gate_b, mlstm_norm_w, sgu_norm_w,
           sgu_w, sgu_b, w_out, norm_ffn_w, router_w, w_gate, w_up, w_down, norm_final_w):
    p = _prep_params(norm_mix_w, w_in, conv_w, conv_b, gate_b, mlstm_norm_w, sgu_norm_w, sgu_w,
                     sgu_b, w_out, norm_ffn_w, router_w, w_gate, w_up, w_down, norm_final_w)
    mixed_p, (wgate,) = _mixer(x_prompt, p, [w_gate[0]])
    mixed_s, (wup, wdown) = _mixer(x_sample, p, [w_up[0], w_down[0]])
    return (_moe_tail(x_prompt, mixed_p, p, wgate, wup, wdown),
            _moe_tail(x_sample, mixed_s, p, wgate, wup, wdown))
```

```python
import functools
import math

import jax
import jax.numpy as jnp
from jax import lax
from jax.experimental import pallas as pl
from jax.experimental.pallas import tpu as pltpu

F32 = jnp.float32
BF16 = jnp.bfloat16
I32 = jnp.int32

EPS = 1e-6
N_HEADS = 4
HEAD_DIM = 128
CHUNK = 128
N_EXPERTS = 16
CAPACITY_FACTOR = 2
LANES = 128
SUBLANES = 8
BF16_ROWS = 16
NEG_BIG = -1e30
VMEM_LIMIT = 48 * 1024 * 1024

_NT = (((1,), (1,)), ((), ()))
_TN = (((0,), (0,)), ((), ()))


def _cparams(sem, vmem=VMEM_LIMIT):
    return pltpu.CompilerParams(dimension_semantics=sem, vmem_limit_bytes=vmem)


def _dot(a, b):
    return jnp.dot(a, b, preferred_element_type=F32)


def _sigmoid(x):
    return 1.0 / (1.0 + jnp.exp(-x))


def _gelu(x):
    return 0.5 * x * (1.0 + lax.erf(x * (1.0 / math.sqrt(2.0))))


def _rms(x, w):
    ms = jnp.mean(x * x, axis=-1, keepdims=True)
    return x * lax.rsqrt(ms + EPS) * w


PROJ_COLS = 256


def _in_proj_kernel(x_ref, xp_ref, xn_ref, nw_ref, wqk_ref, wvt_ref, wo_ref, wgt_ref,
                    wu_ref, ws_ref, cw_ref, cb_ref, gbt_ref, snw_ref,
                    q_ref, k_ref, vt_ref, og_ref, gt_ref, u_ref, s_ref):
    i = pl.program_id(1)
    n_i = pl.num_programs(1)
    tm = x_ref.shape[1]
    d_a = q_ref.shape[2]
    nw = nw_ref[...]
    hb = _rms(x_ref[0], nw).astype(BF16)
    hp = jnp.where(i == 0, 0.0, _rms(xp_ref[0], nw)).astype(BF16)
    hn = jnp.where(i == n_i - 1, 0.0, _rms(xn_ref[0], nw)).astype(BF16)
    h_ext = jnp.concatenate([hb, hp, hn], axis=0)
    cw = cw_ref[...]
    cb = cb_ref[...]
    snw = snw_ref[...]
    row = lax.broadcasted_iota(I32, (tm, PROJ_COLS), 0)

    def tile_cols(j):
        return slice(j * PROJ_COLS, (j + 1) * PROJ_COLS)

    def qk_mm(j):
        return _dot(h_ext, wqk_ref[:, tile_cols(j)])

    def qk_ep(j, ze):
        cs = tile_cols(j)
        z = ze[:tm]
        zp = ze[tm + SUBLANES - 1:tm + SUBLANES]
        zn = ze[tm + SUBLANES:tm + SUBLANES + 1]
        z_prev = jnp.where(row == 0, zp, pltpu.roll(z, 1, axis=0))
        z_next = jnp.where(row == tm - 1, zn, pltpu.roll(z, tm - 1, axis=0))
        conv = cb[:, cs] + cw[0:1, cs] * z_prev + cw[1:2, cs] * z + cw[2:3, cs] * z_next
        qk = conv * _sigmoid(conv)
        if cs.start < d_a:
            q_ref[0, :, cs] = qk.astype(BF16)
        else:
            ks = slice(cs.start - d_a, cs.stop - d_a)
            k_ref[0, :, ks] = (qk * (1.0 / math.sqrt(HEAD_DIM))).astype(BF16)

    def u_mm(j):
        return _dot(hb, wu_ref[:, tile_cols(j)])

    def u_ep(j, r):
        u_ref[0, :, tile_cols(j)] = _gelu(r).astype(BF16)

    def s_mm(j):
        return _dot(hb, ws_ref[:, tile_cols(j)])

    def s_ep(j, r):
        cs = tile_cols(j)
        sv = _gelu(r)
        for g in range(PROJ_COLS // HEAD_DIM):
            gs = slice(g * HEAD_DIM, (g + 1) * HEAD_DIM)
            og_cols = slice(cs.start + gs.start, cs.start + gs.stop)
            s_ref[0, :, og_cols] = _rms(sv[:, gs], snw[:, og_cols]).astype(BF16)

    def o_mm(j):
        return _dot(hb, wo_ref[:, tile_cols(j)])

    def o_ep(j, r):
        og_ref[0, :, tile_cols(j)] = _sigmoid(r).astype(BF16)

    def vt_mm(j):
        return lax.dot_general(wvt_ref[tile_cols(j), :], hb, _NT, preferred_element_type=F32)

    def vt_ep(j, r):
        vt_ref[0, tile_cols(j), :] = r.astype(BF16)

    n_qk = wqk_ref.shape[1] // PROJ_COLS
    n_b = wu_ref.shape[1] // PROJ_COLS
    n_a = wo_ref.shape[1] // PROJ_COLS
    light = [(u_mm, u_ep, j) for j in range(n_b)] + [(s_mm, s_ep, j) for j in range(n_b)]
    light = [light[(k // 2) + (k % 2) * n_b] for k in range(2 * n_b)]
    tiles = []
    for j in range(n_qk):
        tiles.append((qk_mm, qk_ep, j))
        if j < len(light):
            tiles.append(light[j])
    tiles += light[n_qk:]
    for j in range(n_a):
        tiles += [(o_mm, o_ep, j), (vt_mm, vt_ep, j)]
    pending = None
    for mm, ep, j in tiles:
        res = mm(j)
        if pending is not None:
            pending[0](pending[1], pending[2])
        pending = (ep, j, res)
    pending[0](pending[1], pending[2])

    zgt = lax.dot_general(wgt_ref[...], hb, _NT, preferred_element_type=F32) + gbt_ref[...]
    rowt = lax.broadcasted_iota(I32, zgt.shape, 0)
    gt_ref[0] = jnp.where(rowt < 2 * N_HEADS, zgt, jax.nn.log_sigmoid(zgt))


def _in_proj(x, nw, wqk, wvt, wo, wgt, wu, ws, cw, cb, gbt, snw, tm):
    B, T, D = x.shape
    d_a = wvt.shape[0]
    d_b = wu.shape[1]
    ng = wgt.shape[0]
    nt = T // tm
    hb8 = tm // SUBLANES
    last8 = T // SUBLANES - 1
    full = lambda a: pl.BlockSpec(a.shape, lambda b, i: (0,) * a.ndim)
    tok = lambda w: pl.BlockSpec((1, tm, w), lambda b, i: (b, i, 0))
    in_specs = [
        pl.BlockSpec((1, tm, D), lambda b, i: (b, i, 0)),
        pl.BlockSpec((1, SUBLANES, D), lambda b, i: (b, jnp.maximum(i * hb8 - 1, 0), 0)),
        pl.BlockSpec((1, SUBLANES, D), lambda b, i: (b, jnp.minimum((i + 1) * hb8, last8), 0)),
    ] + [full(a) for a in (nw, wqk, wvt, wo, wgt, wu, ws, cw, cb, gbt, snw)]
    out_shape = (
        jax.ShapeDtypeStruct((B, T, d_a), BF16), jax.ShapeDtypeStruct((B, T, d_a), BF16),
        jax.ShapeDtypeStruct((B, d_a, T), BF16), jax.ShapeDtypeStruct((B, T, d_a), BF16),
        jax.ShapeDtypeStruct((B, ng, T), F32),
        jax.ShapeDtypeStruct((B, T, d_b), BF16), jax.ShapeDtypeStruct((B, T, d_b), BF16),
    )
    out_specs = (tok(d_a), tok(d_a), pl.BlockSpec((1, d_a, tm), lambda b, i: (b, 0, i)), tok(d_a),
                 pl.BlockSpec((1, ng, tm), lambda b, i: (b, 0, i)), tok(d_b), tok(d_b))
    return pl.pallas_call(
        _in_proj_kernel, out_shape=out_shape, grid=(B, nt), in_specs=in_specs,
        out_specs=out_specs, compiler_params=_cparams(("parallel", "arbitrary")),
        name="in_proj")(x, x, x, nw, wqk, wvt, wo, wgt, wu, ws, cw, cb, gbt, snw)


def _split3(x):
    x0 = x.astype(BF16)
    r1 = x - x0.astype(F32)
    x1 = r1.astype(BF16)
    x2 = (r1 - x1.astype(F32)).astype(BF16)
    return x0, x1, x2


def _mlstm_kernel(*refs, n_cast):
    qf_ref, kf_ref, vtf_ref, gtf_ref, qb_ref, kb_ref, vtb_ref, gtb_ref = refs[:8]
    cast_in = refs[8:8 + n_cast]
    hf_ref, hb_ref = refs[8 + n_cast:10 + n_cast]
    cast_out = refs[10 + n_cast:10 + 2 * n_cast]
    c_st, m_st = refs[10 + 2 * n_cast:]
    for w_ref, o_ref in zip(cast_in, cast_out):
        o_ref[...] = w_ref[...].astype(BF16)

    c = pl.program_id(1)
    L = qf_ref.shape[1]
    d = HEAD_DIM

    @pl.when(c == 0)
    def _():
        c_st[...] = jnp.zeros_like(c_st)
        m_st[...] = jnp.zeros_like(m_st)

    r0 = lax.broadcasted_iota(I32, (L, L), 0)
    r1 = lax.broadcasted_iota(I32, (L, L), 1)
    ones8 = jnp.ones((SUBLANES, L), BF16)
    ng = 2 * N_HEADS
    dirs = ((qf_ref, kf_ref, vtf_ref, gtf_ref[0], r0 <= r1, hf_ref),
            (qb_ref, kb_ref, vtb_ref, gtb_ref[0], r0 >= r1, hb_ref))

    chains = []
    for dr, (q_ref, k_ref, vt_ref, gt, mask_st, h_ref) in enumerate(dirs):
        g3 = jnp.concatenate(_split3(gt), axis=0)
        b3 = _dot(g3, mask_st.astype(BF16))
        nr = gt.shape[0]
        br_all = b3[:nr] + b3[nr:2 * nr] + b3[2 * nr:]
        a_rows = gt[:ng] - br_all[ng:]
        a_cols = jnp.concatenate([a_rows, jnp.zeros((L - ng, L), F32)], axis=0).T
        for hd in range(N_HEADS):
            j = dr * N_HEADS + hd
            hs = slice(hd * d, (hd + 1) * d)
            chains.append(dict(
                j=j, hs=hs, h_ref=h_ref, mask=mask_st, qb=q_ref[0, :, hs], kb=k_ref[0, :, hs],
                vt_aug=jnp.concatenate([vt_ref[0, hs, :], ones8], axis=0),
                a_col=a_cols[:, j:j + 1], i_row=gt[j:j + 1, :], b_row=br_all[ng + j:ng + j + 1, :],
                b_tot=jnp.sum(gt[ng + j:ng + j + 1, :], axis=1, keepdims=True),
                caug=c_st[j], m_prev=m_st[j]))

    for ch in chains:
        ch["s_raw"] = lax.dot_general(ch["kb"], ch["qb"], _NT, preferred_element_type=F32)
        ch["ia"] = lax.dot_general(ch["caug"].astype(BF16), ch["qb"], _NT, preferred_element_type=F32)
    for ch in chains:
        dmat = jnp.where(ch["mask"], ch["a_col"] + ch["b_row"], NEG_BIG)
        inter = ch["b_row"] + ch["m_prev"]
        m_t = jnp.maximum(jnp.max(dmat, axis=0, keepdims=True), inter)
        st = ch["s_raw"] * jnp.exp(dmat - m_t)
        w_inter = jnp.exp(inter - m_t)
        den = jnp.sum(st, axis=0, keepdims=True) + w_inter * ch["ia"][d:d + 1]
        ch["st"] = st.astype(BF16)
        ch["w_inter"] = w_inter
        ch["rden"] = 1.0 / jnp.maximum(jnp.abs(den), jnp.exp(-m_t))
    for ch in chains:
        num = _dot(ch["vt_aug"][:d], ch["st"]) + ch["w_inter"] * ch["ia"][:d]
        ch["h_ref"][0, :, ch["hs"]] = (num * ch["rden"]).T
    for ch in chains:
        g_row = ch["b_tot"] - ch["b_row"] + ch["i_row"]
        m_new = jnp.maximum(ch["b_tot"] + ch["m_prev"], jnp.max(g_row, axis=1, keepdims=True))
        wk = jnp.exp(g_row - m_new)
        decay = jnp.exp(ch["b_tot"] + ch["m_prev"] - m_new)
        vw = (ch["vt_aug"].astype(F32) * wk).astype(BF16)
        c_st[ch["j"]] = decay * ch["caug"] + _dot(vw, ch["kb"])
        m_st[ch["j"]] = m_new


CAST_SLAB_BYTES = 2 * 1024 * 1024


def _cast_slabs(w, nsteps):
    E, R, C = w.shape
    if (E * R) % nsteps:
        return None
    rows = (E * R) // nsteps
    if rows % BF16_ROWS or R % rows or rows * C * 4 > CAST_SLAB_BYTES:
        return None
    return w.reshape(nsteps, rows, C)


def _mlstm(q, k, vt, gt, cast_slabs=()):
    B, T, d_a = q.shape
    L = CHUNK
    nc = T // L
    ng = gt.shape[1]
    fwd = lambda w: pl.BlockSpec((1, L, w), lambda b, c: (b, c, 0))
    bwd = lambda w: pl.BlockSpec((1, L, w), lambda b, c: (b, nc - 1 - c, 0))
    fwd_t = lambda r: pl.BlockSpec((1, r, L), lambda b, c: (b, 0, c))
    bwd_t = lambda r: pl.BlockSpec((1, r, L), lambda b, c: (b, 0, nc - 1 - c))
    slab = lambda a: pl.BlockSpec((1,) + a.shape[1:], lambda b, c: (b * nc + c, 0, 0))
    in_specs = [fwd(d_a), fwd(d_a), fwd_t(d_a), fwd_t(ng), bwd(d_a), bwd(d_a), bwd_t(d_a), bwd_t(ng)]
    in_specs += [slab(a) for a in cast_slabs]
    out_shape = [jax.ShapeDtypeStruct((B, T, d_a), F32), jax.ShapeDtypeStruct((B, T, d_a), F32)]
    out_shape += [jax.ShapeDtypeStruct(a.shape, BF16) for a in cast_slabs]
    out_specs = [fwd(d_a), bwd(d_a)] + [slab(a) for a in cast_slabs]
    nch = 2 * N_HEADS
    outs = pl.pallas_call(
        functools.partial(_mlstm_kernel, n_cast=len(cast_slabs)),
        out_shape=tuple(out_shape), grid=(B, nc), in_specs=in_specs, out_specs=tuple(out_specs),
        scratch_shapes=[pltpu.VMEM((nch, HEAD_DIM + SUBLANES, HEAD_DIM), F32),
                        pltpu.VMEM((nch, 1, 1), F32)],
        compiler_params=_cparams(("parallel", "arbitrary")),
        name="mlstm")(q, k, vt, gt, q, k, vt, gt, *cast_slabs)
    return outs[0], outs[1], tuple(outs[2:])


MIX_SUB_ROWS = 256


def _mix_out_kernel(hf_ref, hb_ref, og_ref, u_ref, s_ref, x_ref, mnw_ref, sw_ref, sb_ref,
                    wout_ref, fnw_ref, rwt_ref, x1_ref, xn_ref, aff_ref):
    tm = x_ref.shape[0]
    d_a = og_ref.shape[1]
    d_b = u_ref.shape[1]
    sub = min(tm, MIX_SUB_ROWS)
    subs = [slice(i * sub, (i + 1) * sub) for i in range(tm // sub)]
    mnw = mnw_ref[...]
    sbias = sb_ref[...]

    gates = []
    for rs in subs:
        rows = []
        for cc in range(sub // CHUNK):
            r0 = rs.start + cc * CHUNK
            cols = [_dot(sw_ref[g], s_ref[r0:r0 + CHUNK, g * HEAD_DIM:(g + 1) * HEAD_DIM])
                    for g in range(d_b // HEAD_DIM)]
            rows.append(jnp.concatenate(cols, axis=1) + sbias)
        gates.append(jnp.concatenate(rows, axis=0))

    mixes = []
    for rs, gate in zip(subs, gates):
        h = hf_ref[rs, :] + hb_ref[rs, :]
        parts = [_rms(h[:, hd * HEAD_DIM:(hd + 1) * HEAD_DIM], mnw[:, hd * HEAD_DIM:(hd + 1) * HEAD_DIM])
                 for hd in range(d_a // HEAD_DIM)]
        a_out = (og_ref[rs, :].astype(F32) * jnp.concatenate(parts, axis=1)).astype(BF16)
        b_out = (u_ref[rs, :].astype(F32) * gate).astype(BF16)
        mixes.append(jnp.concatenate([a_out, b_out], axis=1))

    x1s = [x_ref[rs, :] + _dot(mix, wout_ref[...]) for rs, mix in zip(subs, mixes)]

    xns = []
    for rs, x1 in zip(subs, x1s):
        x1_ref[rs, :] = x1
        xn = _rms(x1, fnw_ref[...])
        xns.append(xn)
        chunks = xn.shape[1] // LANES
        for j in range(chunks):
            xn_ref[pl.ds(rs.start * chunks + j, sub, stride=chunks), :] = xn[:, j * LANES:(j + 1) * LANES]

    E = rwt_ref.shape[0]
    r0 = rwt_ref[...].astype(BF16)
    r1 = (rwt_ref[...] - r0.astype(F32)).astype(BF16)
    r01 = jnp.concatenate([r0, r1], axis=0)
    logits = []
    for xn in xns:
        x0 = xn.astype(BF16)
        x1 = (xn - x0.astype(F32)).astype(BF16)
        a = lax.dot_general(r01, x0, _NT, preferred_element_type=F32)
        b = lax.dot_general(r0, x1, _NT, preferred_element_type=F32)
        logits.append(a[:E] + a[E:] + b)
    for rs, lg in zip(subs, logits):
        ex = jnp.exp(lg - jnp.max(lg, axis=0, keepdims=True))
        aff = ex / jnp.sum(ex, axis=0, keepdims=True)
        for j in range(sub // LANES):
            aff_ref[rs.start // LANES + j] = aff[:, j * LANES:(j + 1) * LANES]


def _mix_out(hf, hb, og, u, s, x, mnw, sw, sbias, wout, fnw, rwt, tm):
    N, D = x.shape
    d_a = og.shape[1]
    d_b = u.shape[1]
    E = rwt.shape[0]
    nt = N // tm
    full = lambda a: pl.BlockSpec(a.shape, lambda i: (0,) * a.ndim)
    tok = lambda w: pl.BlockSpec((tm, w), lambda i: (i, 0))
    in_specs = [tok(d_a), tok(d_a), tok(d_a), tok(d_b), tok(d_b), tok(D)] + [
        full(a) for a in (mnw, sw, sbias, wout, fnw, rwt)]
    chunks = D // LANES
    out_shape = (jax.ShapeDtypeStruct((N, D), F32), jax.ShapeDtypeStruct((N * chunks, LANES), F32),
                 jax.ShapeDtypeStruct((N // LANES, E, LANES), F32))
    out_specs = (tok(D), pl.BlockSpec((tm * chunks, LANES), lambda i: (i, 0)),
                 pl.BlockSpec((tm // LANES, E, LANES), lambda i: (i, 0, 0)))
    return pl.pallas_call(
        _mix_out_kernel, out_shape=out_shape, grid=(nt,), in_specs=in_specs, out_specs=out_specs,
        compiler_params=_cparams(("parallel",)), name="mix_out")(
            hf, hb, og, u, s, x, mnw, sw, sbias, wout, fnw, rwt)


def _select_kernel(aff_ref, posm_ref, off_ref, cnt_s, wi_s, *, cap):
    nb, E, _ = aff_ref.shape
    aff = aff_ref[...]

    def count_ge(cand):
        c = jnp.sum((aff >= cand).astype(I32), axis=0, keepdims=True)
        return jnp.sum(c, axis=2, keepdims=True)

    def bit_step(i, thr_bits):
        cand = thr_bits | jnp.left_shift(jnp.int32(1), 30 - i)
        return jnp.where(count_ge(pltpu.bitcast(cand, F32)) >= cap, cand, thr_bits)

    thr = pltpu.bitcast(lax.fori_loop(0, 31, bit_step, jnp.zeros((1, E, 1), I32)), F32)
    gt = aff > thr
    eq = aff == thr
    n_gt = jnp.sum(jnp.sum(gt.astype(I32), axis=0, keepdims=True), axis=2, keepdims=True)
    need = cap - n_gt

    li = lax.broadcasted_iota(I32, (LANES, LANES), 0)
    lj = lax.broadcasted_iota(I32, (LANES, LANES), 1)
    upper = (li < lj).astype(BF16)

    def excl_cumsum(flag):
        fb = flag.astype(BF16).reshape(nb * E, LANES)
        wi_s[...] = _dot(fb, upper).astype(I32).reshape(nb, E, LANES)
        cnt_s[...] = jnp.sum(flag.astype(I32), axis=2, keepdims=True)

        def blk(b, run):
            wi_s[b] = wi_s[b] + run
            return run + cnt_s[b]

        lax.fori_loop(0, nb, blk, jnp.zeros((E, 1), I32))
        return wi_s[...]

    eq_rank = excl_cumsum(eq)
    sel = gt | (eq & (eq_rank < need))
    pos = excl_cumsum(sel)
    posm_ref[...] = jnp.where(sel, pos, -1)
    off_ref[...] = jnp.broadcast_to(pos[:, :, 0:1], off_ref.shape)


def _select(aff3, cap):
    nb, E, _ = aff3.shape
    return pl.pallas_call(
        functools.partial(_select_kernel, cap=cap),
        out_shape=(jax.ShapeDtypeStruct((nb, E, LANES), I32), jax.ShapeDtypeStruct((nb, E, LANES), I32)),
        scratch_shapes=[pltpu.VMEM((nb, E, 1), I32), pltpu.VMEM((nb, E, LANES), I32)],
        compiler_params=_cparams(None), name="select")(aff3)


def _compact_kernel(off_sm, posm_ref, aff_ref, acc_ref):
    nb, E, _ = posm_ref.shape
    acc_ref[...] = jnp.zeros_like(acc_ref)
    srow = lax.broadcasted_iota(I32, (2 * LANES, LANES), 0)
    r8 = lax.broadcasted_iota(I32, (SUBLANES, LANES), 0)
    lane8 = lax.broadcasted_iota(I32, (SUBLANES, LANES), 1)

    def blk(b, carry):
        tok = b * LANES + lane8
        t_hi = jnp.right_shift(tok, 8).astype(F32)
        t_lo = jnp.bitwise_and(tok, 255).astype(F32)
        pm = posm_ref[b]
        af = aff_ref[b]
        for e in range(E):
            off = off_sm[b * E + e]
            j0 = jnp.right_shift(off, 7)
            rel = pm[e:e + 1, :] - j0 * LANES
            onehot = (srow == rel).astype(BF16)
            a = af[e:e + 1, :]
            a0 = a.astype(BF16)
            r1 = a - a0.astype(F32)
            a1 = r1.astype(BF16)
            a2 = (r1 - a1.astype(F32)).astype(BF16)
            lhs = jnp.where(r8 == 0, t_hi, jnp.where(r8 == 1, t_lo, 0.0))
            lhs = jnp.where(r8 == 2, a0.astype(F32), lhs)
            lhs = jnp.where(r8 == 3, a1.astype(F32), lhs)
            lhs = jnp.where(r8 == 4, a2.astype(F32), lhs).astype(BF16)
            out = lax.dot_general(lhs, onehot, _NT, preferred_element_type=F32)
            acc_ref[e, j0] = acc_ref[e, j0] + out[:, :LANES]
            acc_ref[e, j0 + 1] = acc_ref[e, j0 + 1] + out[:, LANES:]
        return carry

    lax.fori_loop(0, nb, blk, 0)


def _compact(off_flat, posm3, aff3, cap):
    nb, E, _ = posm3.shape
    nt_pad = cap // LANES + 2
    gs = pltpu.PrefetchScalarGridSpec(
        num_scalar_prefetch=1, grid=(1,),
        in_specs=[pl.BlockSpec(posm3.shape, lambda i, o: (0, 0, 0)),
                  pl.BlockSpec(aff3.shape, lambda i, o: (0, 0, 0))],
        out_specs=pl.BlockSpec((E, nt_pad, SUBLANES, LANES), lambda i, o: (0, 0, 0, 0)))
    return pl.pallas_call(
        _compact_kernel, out_shape=jax.ShapeDtypeStruct((E, nt_pad, SUBLANES, LANES), F32),
        grid_spec=gs, compiler_params=_cparams(("arbitrary",)), name="compact")(off_flat, posm3, aff3)


def _ffn_kernel(idc_sm, idn_sm, xn_hbm, cacc_ref, wg_ref, wu_ref, wd_ref, y_ref, xbuf, sem,
                *, n_pairs, fc):
    g = pl.program_id(0)
    ts = y_ref.shape[0] // 2
    chunks = xbuf.shape[1] // ts
    D = wd_ref.shape[2]
    F = wg_ref.shape[2]
    nchunk = F // fc
    rows_per_chunk = ts // nchunk
    scale_rows = min(ts, LANES)

    def row_copy(idx_sm, i, r, dst_slot):
        src = pl.multiple_of(idx_sm[0, 0, i] * chunks, chunks)
        return pltpu.make_async_copy(xn_hbm.at[pl.ds(src, chunks), :],
                                     xbuf.at[dst_slot, pl.ds(r * chunks, chunks), :], sem.at[dst_slot])

    def wait_tile(slot):
        pltpu.make_async_copy(xn_hbm.at[pl.ds(0, ts * chunks), :], xbuf.at[slot], sem.at[slot]).wait()

    def run_tile(slot, next_idx_sm, next_base):
        xs = xbuf.at[slot]
        x = jnp.concatenate([xs[pl.ds(j, ts, stride=chunks), :] for j in range(chunks)],
                            axis=1).astype(BF16)
        acc = jnp.zeros((ts, D), F32)
        for c in range(nchunk):
            for r in range(c * rows_per_chunk, (c + 1) * rows_per_chunk):
                row_copy(next_idx_sm, next_base + r, r, 1 - slot).start(priority=r % 2)
            cs = slice(c * fc, (c + 1) * fc)
            gte = _dot(x, wg_ref[0, :, cs])
            up = _dot(x, wu_ref[0, :, cs])
            hid = (gte * _sigmoid(gte) * up).astype(BF16)
            acc = acc + _dot(hid, wd_ref[0, cs, :])
        for k in range(ts // scale_rows):
            r0 = slot * ts + k * scale_rows
            t = cacc_ref[0, r0 // LANES]
            vrow = t[2:3, :] + t[3:4, :] + t[4:5, :]
            vmat = jnp.broadcast_to(vrow, (LANES, LANES)).T[r0 % LANES:r0 % LANES + scale_rows]
            vfull = jnp.concatenate([vmat] * (D // LANES), axis=1)
            y_ref[r0:r0 + scale_rows, :] = (
                acc[k * scale_rows:(k + 1) * scale_rows, :] * vfull).astype(y_ref.dtype)

    @pl.when(g == 0)
    def _():
        def body(r, carry):
            row_copy(idc_sm, r, r, 0).start()
            return carry
        lax.fori_loop(0, ts, body, 0, unroll=8)

    @pl.when(g < n_pairs)
    def _():
        wait_tile(0)
        run_tile(0, idc_sm, ts)
        wait_tile(1)
        run_tile(1, idn_sm, 0)

    @pl.when(g >= n_pairs)
    def _():
        wait_tile(0)
        y_ref[...] = jnp.zeros_like(y_ref)


def _ffn(idx, xn, cacc, wg, wu, wd, cap, ts):
    E, D, F = wg.shape
    pairs_per_expert = cap // (2 * ts)
    n_pairs = E * pairs_per_expert
    last = n_pairs - 1
    idx3 = idx.reshape(n_pairs, 1, 2 * ts)
    eidx = lambda g: jnp.minimum(g // pairs_per_expert, E - 1)
    in_specs = [
        pl.BlockSpec((1, 1, 2 * ts), lambda g: (jnp.minimum(g, last), 0, 0), memory_space=pltpu.SMEM),
        pl.BlockSpec((1, 1, 2 * ts), lambda g: (jnp.minimum(g + 1, last), 0, 0), memory_space=pltpu.SMEM),
        pl.BlockSpec(memory_space=pl.ANY),
        pl.BlockSpec((1, 2 * ts // LANES, SUBLANES, LANES),
                     lambda g: (eidx(g), lax.rem(jnp.minimum(g, last), pairs_per_expert), 0, 0)),
        pl.BlockSpec((1, D, F), lambda g: (eidx(g), 0, 0)),
        pl.BlockSpec((1, D, F), lambda g: (eidx(g), 0, 0)),
        pl.BlockSpec((1, F, D), lambda g: (eidx(g), 0, 0)),
    ]
    return pl.pallas_call(
        functools.partial(_ffn_kernel, n_pairs=n_pairs, fc=min(512, F)),
        out_shape=jax.ShapeDtypeStruct(((n_pairs + 1) * 2 * ts, D), BF16),
        grid=(n_pairs + 1,), in_specs=in_specs,
        out_specs=pl.BlockSpec((2 * ts, D), lambda g: (g, 0)),
        scratch_shapes=[pltpu.VMEM((2, ts * (D // LANES), LANES), F32), pltpu.SemaphoreType.DMA((2,))],
        compiler_params=_cparams(("arbitrary",)), name="ffn")(idx3, idx3, xn, cacc, wg, wu, wd)


WIN = 64


def _combine_kernel(off_sm, posm_ref, x1_ref, y_hbm, fnw_ref, o_ref, ycat, yext, sem, sem_ext,
                    *, cap, y_rows, nblk):
    b = pl.program_id(0)
    slot = lax.rem(b, 2)
    nsub, E, _ = posm_ref.shape
    tb = x1_ref.shape[0]
    pm = jnp.concatenate([posm_ref[j] for j in range(nsub)], axis=1)
    wrow = lax.broadcasted_iota(I32, (WIN, tb), 0)

    def starts_of(blk, r):
        out = []
        for e in range(E):
            base = jnp.left_shift(jnp.right_shift(off_sm[blk * E + e], 4), 4)
            st = jnp.minimum(e * cap + base + r * WIN, y_rows - WIN)
            out.append(pl.multiple_of(st, BF16_ROWS))
        return out

    def copies(starts, dst, dsem):
        return [pltpu.make_async_copy(y_hbm.at[pl.ds(starts[e], WIN), :], dst.at[pl.ds(e * WIN, WIN), :], dsem)
                for e in range(E)]

    def onehot(starts):
        ps = []
        for e in range(E):
            pe = pm[e:e + 1, :]
            rel = jnp.where(pe >= 0, pe + (e * cap - starts[e]), -1)
            ps.append((wrow == rel).astype(BF16))
        return jnp.concatenate(ps, axis=0)

    @pl.when(b == 0)
    def _():
        for cp in copies(starts_of(b, 0), ycat.at[0], sem.at[0]):
            cp.start()

    @pl.when(b + 1 < nblk)
    def _():
        for cp in copies(starts_of(b + 1, 0), ycat.at[1 - slot], sem.at[1 - slot]):
            cp.start()

    starts0 = starts_of(b, 0)
    p0 = onehot(starts0)
    for cp in copies(starts0, ycat.at[slot], sem.at[slot]):
        cp.wait()
    acc0 = lax.dot_general(p0, ycat[slot], _TN, preferred_element_type=F32)

    nrounds = jnp.int32(1)
    for e in range(E):
        base = jnp.left_shift(jnp.right_shift(off_sm[b * E + e], 4), 4)
        nrounds = jnp.maximum(nrounds, jnp.right_shift(off_sm[(b + 1) * E + e] - base + (WIN - 1), 6))

    def round_body(r, acc):
        starts = starts_of(b, r)
        cps = copies(starts, yext, sem_ext)
        for cp in cps:
            cp.start()
        p = onehot(starts)
        for cp in cps:
            cp.wait()
        return acc + lax.dot_general(p, yext[...], _TN, preferred_element_type=F32)

    acc = lax.fori_loop(1, nrounds, round_body, acc0)
    o_ref[...] = _rms(x1_ref[...] + acc, fnw_ref[...])


def _combine(off_flat, posm3, x1, y, fnw, cap, tb):
    N, D = x1.shape
    nb, E, _ = posm3.shape
    nsub = tb // LANES
    y_rows = y.shape[0]
    gs = pltpu.PrefetchScalarGridSpec(
        num_scalar_prefetch=1, grid=(N // tb,),
        in_specs=[pl.BlockSpec((nsub, E, LANES), lambda i, o: (i, 0, 0)),
                  pl.BlockSpec((tb, D), lambda i, o: (i, 0)),
                  pl.BlockSpec(memory_space=pl.ANY),
                  pl.BlockSpec(fnw.shape, lambda i, o: (0, 0))],
        out_specs=pl.BlockSpec((tb, D), lambda i, o: (i, 0)),
        scratch_shapes=[pltpu.VMEM((2, E * WIN, D), BF16), pltpu.VMEM((E * WIN, D), BF16),
                        pltpu.SemaphoreType.DMA((2,)), pltpu.SemaphoreType.DMA])
    return pl.pallas_call(
        functools.partial(_combine_kernel, cap=cap, y_rows=y_rows, nblk=N // tb),
        out_shape=jax.ShapeDtypeStruct((N, D), F32), grid_spec=gs,
        compiler_params=_cparams(("arbitrary",)), name="combine")(off_flat, posm3, x1, y, fnw)


def _prep_params(norm_mix_w, w_in, conv_w, conv_b, gate_b, mlstm_norm_w, sgu_norm_w, sgu_w, sgu_b,
                 w_out, norm_ffn_w, router_w, w_gate, w_up, w_down, norm_final_w):
    d_a = mlstm_norm_w.shape[1]
    d_b = sgu_norm_w.shape[1]
    ng = gate_b.shape[1]
    w = w_in[0]
    o0, o1, o2, o3, o4, o5 = 2 * d_a, 3 * d_a, 4 * d_a, 4 * d_a + ng, 4 * d_a + ng + d_b, 4 * d_a + ng + 2 * d_b
    wg = w[:, o2:o3]
    return dict(
        nw=norm_mix_w[0][None, :],
        wqk=w[:, :o0].astype(BF16), wvt=w[:, o0:o1].T.astype(BF16), wo=w[:, o1:o2].astype(BF16),
        wgt=wg.T.astype(BF16),
        wu=w[:, o3:o4].astype(BF16), ws=w[:, o4:o5].astype(BF16),
        cw=conv_w[0], cb=conv_b[0][None, :], gbt=gate_b[0][:, None],
        snw=sgu_norm_w[0][None, :], mnw=mlstm_norm_w[0][None, :],
        sw=sgu_w[0].astype(BF16),
        sbias=jnp.repeat(sgu_b[0].T, HEAD_DIM, axis=1),
        wout=w_out[0].astype(BF16), fnw=norm_ffn_w[0][None, :], rwt=router_w[0].T,
        nfw=norm_final_w[None, :],
    )


def _mixer(x, p, expert_weights):
    B, T, D = x.shape
    q, k, vt, og, gt, u, s = _in_proj(
        x, p["nw"], p["wqk"], p["wvt"], p["wo"], p["wgt"], p["wu"], p["ws"],
        p["cw"], p["cb"], p["gbt"], p["snw"], min(512, T))
    nsteps = B * (T // CHUNK)
    slabs = [_cast_slabs(w, nsteps) for w in expert_weights]
    hf, hb, cast = _mlstm(q, k, vt, gt, tuple(sl for sl in slabs if sl is not None))
    cast = list(cast)
    weights_bf16 = [w.astype(BF16) if sl is None else cast.pop(0).reshape(w.shape)
                    for w, sl in zip(expert_weights, slabs)]
    return (hf, hb, og, u, s), weights_bf16


def _moe_tail(x, mixed, p, wgate, wup, wdown):
    B, T, D = x.shape
    N = B * T
    E = N_EXPERTS
    cap = (N * CAPACITY_FACTOR) // E
    hf, hb, og, u, s = mixed
    flat = lambda a: a.reshape(N, a.shape[-1])
    x1, xn, aff3 = _mix_out(flat(hf), flat(hb), flat(og), flat(u), flat(s), flat(x),
                            p["mnw"], p["sw"], p["sbias"], p["wout"], p["fnw"], p["rwt"], min(512, N))
    posm3, off3 = _select(aff3, cap)
    nb = N // LANES
    off_flat = off3[:, :, 0].reshape(nb * E)
    cacc = _compact(off_flat, posm3, aff3, cap)
    nt = cap // LANES
    idx = (cacc[:, :nt, 0, :] * 256.0 + cacc[:, :nt, 1, :]).astype(I32).reshape(E * cap)
    ts = min(512, cap // 2)
    y = _ffn(idx, xn, cacc, wgate, wup, wdown, cap, ts)
    tb = min(256, N)
    sub = tb // LANES
    off_tb = jnp.concatenate([off3[::sub, :, 0], jnp.full((1, E), cap, I32)], axis=0).reshape(-1)
    out = _combine(off_tb, posm3, x1, y, p["nfw"], cap, tb)
    return out.reshape(B, T, D)


def kernel(x_prompt, x_sample, norm_mix_w, w_in, conv_w, conv_b, gate_b, mlstm_norm_w, sgu_norm_w,
           sgu_w, sgu_b, w_out, norm_ffn_w, router_w, w_gate, w_up, w_down, norm_final_w):
    p = _prep_params(norm_mix_w, w_in, conv_w, conv_b, gate_b, mlstm_norm_w, sgu_norm_w, sgu_w,
                     sgu_b, w_out, norm_ffn_w, router_w, w_gate, w_up, w_down, norm_final_w)
    mixed_p, (wgate,) = _mixer(x_prompt, p, [w_gate[0]])
    mixed_s, (wup, wdown) = _mixer(x_sample, p, [w_up[0], w_down[0]])
    return (_moe_tail(x_prompt, mixed_p, p, wgate, wup, wdown),
            _moe_tail(x_sample, mixed_s, p, wgate, wup, wdown))
```

```python
import functools
import math

import jax
import jax.numpy as jnp
from jax import lax
from jax.experimental import pallas as pl
from jax.experimental.pallas import tpu as pltpu

F32 = jnp.float32
BF16 = jnp.bfloat16
I32 = jnp.int32

EPS = 1e-6
N_HEADS = 4
HEAD_DIM = 128
CHUNK = 128
N_EXPERTS = 16
CAPACITY_FACTOR = 2
LANES = 128
SUBLANES = 8
BF16_ROWS = 16
NEG_BIG = -1e30
VMEM_LIMIT = 48 * 1024 * 1024

_NT = (((1,), (1,)), ((), ()))
_TN = (((0,), (0,)), ((), ()))


def _cparams(sem, vmem=VMEM_LIMIT):
    return pltpu.CompilerParams(dimension_semantics=sem, vmem_limit_bytes=vmem)


def _dot(a, b):
    return jnp.dot(a, b, preferred_element_type=F32)


def _sigmoid(x):
    return 1.0 / (1.0 + jnp.exp(-x))


def _gelu(x):
    return 0.5 * x * (1.0 + lax.erf(x * (1.0 / math.sqrt(2.0))))


def _rms(x, w):
    ms = jnp.mean(x * x, axis=-1, keepdims=True)
    return x * lax.rsqrt(ms + EPS) * w


PROJ_COLS = 256


def _in_proj_kernel(x_ref, xp_ref, xn_ref, nw_ref, wqk_ref, wvt_ref, wo_ref, wgt_ref,
                    wu_ref, ws_ref, cw_ref, cb_ref, gbt_ref, snw_ref,
                    q_ref, k_ref, vt_ref, og_ref, gt_ref, u_ref, s_ref):
    i = pl.program_id(1)
    n_i = pl.num_programs(1)
    tm = x_ref.shape[1]
    d_a = q_ref.shape[2]
    nw = nw_ref[...]
    hb = _rms(x_ref[0], nw).astype(BF16)
    hp = jnp.where(i == 0, 0.0, _rms(xp_ref[0], nw)).astype(BF16)
    hn = jnp.where(i == n_i - 1, 0.0, _rms(xn_ref[0], nw)).astype(BF16)
    h_ext = jnp.concatenate([hb, hp, hn], axis=0)
    cw = cw_ref[...]
    cb = cb_ref[...]
    snw = snw_ref[...]
    row = lax.broadcasted_iota(I32, (tm, PROJ_COLS), 0)

    def tile_cols(j):
        return slice(j * PROJ_COLS, (j + 1) * PROJ_COLS)

    def qk_mm(j):
        return _dot(h_ext, wqk_ref[:, tile_cols(j)])

    def qk_ep(j, ze):
        cs = tile_cols(j)
        z = ze[:tm]
        zp = ze[tm + SUBLANES - 1:tm + SUBLANES]
        zn = ze[tm + SUBLANES:tm + SUBLANES + 1]
        z_prev = jnp.where(row == 0, zp, pltpu.roll(z, 1, axis=0))
        z_next = jnp.where(row == tm - 1, zn, pltpu.roll(z, tm - 1, axis=0))
        conv = cb[:, cs] + cw[0:1, cs] * z_prev + cw[1:2, cs] * z + cw[2:3, cs] * z_next
        qk = conv * _sigmoid(conv)
        if cs.start < d_a:
            q_ref[0, :, cs] = qk.astype(BF16)
        else:
            ks = slice(cs.start - d_a, cs.stop - d_a)
            k_ref[0, :, ks] = (qk * (1.0 / math.sqrt(HEAD_DIM))).astype(BF16)

    def u_mm(j):
        return _dot(hb, wu_ref[:, tile_cols(j)])

    def u_ep(j, r):
        u_ref[0, :, tile_cols(j)] = _gelu(r).astype(BF16)

    def s_mm(j):
        return _dot(hb, ws_ref[:, tile_cols(j)])

    def s_ep(j, r):
        cs = tile_cols(j)
        sv = _gelu(r)
        for g in range(PROJ_COLS // HEAD_DIM):
            gs = slice(g * HEAD_DIM, (g + 1) * HEAD_DIM)
            og_cols = slice(cs.start + gs.start, cs.start + gs.stop)
            s_ref[0, :, og_cols] = _rms(sv[:, gs], snw[:, og_cols]).astype(BF16)

    def o_mm(j):
        return _dot(hb, wo_ref[:, tile_cols(j)])

    def o_ep(j, r):
        og_ref[0, :, tile_cols(j)] = _sigmoid(r).astype(BF16)

    def vt_mm(j):
        return lax.dot_general(wvt_ref[tile_cols(j), :], hb, _NT, preferred_element_type=F32)

    def vt_ep(j, r):
        vt_ref[0, tile_cols(j), :] = r.astype(BF16)

    n_qk = wqk_ref.shape[1] // PROJ_COLS
    n_b = wu_ref.shape[1] // PROJ_COLS
    n_a = wo_ref.shape[1] // PROJ_COLS
    light = [(u_mm, u_ep, j) for j in range(n_b)] + [(s_mm, s_ep, j) for j in range(n_b)]
    light = [light[(k // 2) + (k % 2) * n_b] for k in range(2 * n_b)]
    tiles = []
    for j in range(n_qk):
        tiles.append((qk_mm, qk_ep, j))
        if j < len(light):
            tiles.append(light[j])
    tiles += light[n_qk:]
    for j in range(n_a):
        tiles += [(o_mm, o_ep, j), (vt_mm, vt_ep, j)]
    pending = None
    for mm, ep, j in tiles:
        res = mm(j)
        if pending is not None:
            pending[0](pending[1], pending[2])
        pending = (ep, j, res)
    pending[0](pending[1], pending[2])

    zgt = lax.dot_general(wgt_ref[...], hb, _NT, preferred_element_type=F32) + gbt_ref[...]
    rowt = lax.broadcasted_iota(I32, zgt.shape, 0)
    gt_ref[0] = jnp.where(rowt < 2 * N_HEADS, zgt, jax.nn.log_sigmoid(zgt))


def _in_proj(x, nw, wqk, wvt, wo, wgt, wu, ws, cw, cb, gbt, snw, tm):
    B, T, D = x.shape
    d_a = wvt.shape[0]
    d_b = wu.shape[1]
    ng = wgt.shape[0]
    nt = T // tm
    hb8 = tm // SUBLANES
    last8 = T // SUBLANES - 1
    full = lambda a: pl.BlockSpec(a.shape, lambda b, i: (0,) * a.ndim)
    tok = lambda w: pl.BlockSpec((1, tm, w), lambda b, i: (b, i, 0))
    in_specs = [
        pl.BlockSpec((1, tm, D), lambda b, i: (b, i, 0)),
        pl.BlockSpec((1, SUBLANES, D), lambda b, i: (b, jnp.maximum(i * hb8 - 1, 0), 0)),
        pl.BlockSpec((1, SUBLANES, D), lambda b, i: (b, jnp.minimum((i + 1) * hb8, last8), 0)),
    ] + [full(a) for a in (nw, wqk, wvt, wo, wgt, wu, ws, cw, cb, gbt, snw)]
    out_shape = (
        jax.ShapeDtypeStruct((B, T, d_a), BF16), jax.ShapeDtypeStruct((B, T, d_a), BF16),
        jax.ShapeDtypeStruct((B, d_a, T), BF16), jax.ShapeDtypeStruct((B, T, d_a), BF16),
        jax.ShapeDtypeStruct((B, ng, T), F32),
        jax.ShapeDtypeStruct((B, T, d_b), BF16), jax.ShapeDtypeStruct((B, T, d_b), BF16),
    )
    out_specs = (tok(d_a), tok(d_a), pl.BlockSpec((1, d_a, tm), lambda b, i: (b, 0, i)), tok(d_a),
                 pl.BlockSpec((1, ng, tm), lambda b, i: (b, 0, i)), tok(d_b), tok(d_b))
    return pl.pallas_call(
        _in_proj_kernel, out_shape=out_shape, grid=(B, nt), in_specs=in_specs,
        out_specs=out_specs, compiler_params=_cparams(("parallel", "arbitrary")),
        name="in_proj")(x, x, x, nw, wqk, wvt, wo, wgt, wu, ws, cw, cb, gbt, snw)


def _split3(x):
    x0 = x.astype(BF16)
    r1 = x - x0.astype(F32)
    x1 = r1.astype(BF16)
    x2 = (r1 - x1.astype(F32)).astype(BF16)
    return x0, x1, x2


def _mlstm_kernel(*refs, n_cast):
    qf_ref, kf_ref, vtf_ref, gtf_ref, qb_ref, kb_ref, vtb_ref, gtb_ref = refs[:8]
    cast_in = refs[8:8 + n_cast]
    hf_ref, hb_ref = refs[8 + n_cast:10 + n_cast]
    cast_out = refs[10 + n_cast:10 + 2 * n_cast]
    c_st, m_st = refs[10 + 2 * n_cast:]
    for w_ref, o_ref in zip(cast_in, cast_out):
        o_ref[...] = w_ref[...].astype(BF16)

    c = pl.program_id(1)
    L = CHUNK
    n_sub = qf_ref.shape[1] // L
    d = HEAD_DIM

    @pl.when(c == 0)
    def _():
        c_st[...] = jnp.zeros_like(c_st)
        m_st[...] = jnp.zeros_like(m_st)

    r0 = lax.broadcasted_iota(I32, (L, L), 0)
    r1 = lax.broadcasted_iota(I32, (L, L), 1)
    ones8 = jnp.ones((SUBLANES, L), BF16)
    ng = 2 * N_HEADS
    dirs = ((qf_ref, kf_ref, vtf_ref, gtf_ref, r0 <= r1, hf_ref, lambda i: i),
            (qb_ref, kb_ref, vtb_ref, gtb_ref, r0 >= r1, hb_ref, lambda i: n_sub - 1 - i))

    subs = []
    for i in range(n_sub):
        chains = []
        for dr, (q_ref, k_ref, vt_ref, gt_ref, mask_st, h_ref, order) in enumerate(dirs):
            rows = slice(order(i) * L, (order(i) + 1) * L)
            gt = gt_ref[0, :, rows]
            g3 = jnp.concatenate(_split3(gt), axis=0)
            b3 = _dot(g3, mask_st.astype(BF16))
            nr = gt.shape[0]
            br_all = b3[:nr] + b3[nr:2 * nr] + b3[2 * nr:]
            a_rows = gt[:ng] - br_all[ng:]
            a_cols = jnp.concatenate([a_rows, jnp.zeros((L - ng, L), F32)], axis=0).T
            for hd in range(N_HEADS):
                j = dr * N_HEADS + hd
                hs = slice(hd * d, (hd + 1) * d)
                qb = q_ref[0, rows, hs]
                kb = k_ref[0, rows, hs]
                b_row = br_all[ng + j:ng + j + 1, :]
                chains.append(dict(
                    j=j, hs=hs, rows=rows, h_ref=h_ref, qb=qb, kb=kb,
                    vt_aug=jnp.concatenate([vt_ref[0, hs, rows], ones8], axis=0),
                    dmat=jnp.where(mask_st, a_cols[:, j:j + 1] + b_row, NEG_BIG),
                    i_row=gt[j:j + 1, :], b_row=b_row,
                    b_tot=jnp.sum(gt[ng + j:ng + j + 1, :], axis=1, keepdims=True),
                    s_raw=lax.dot_general(kb, qb, _NT, preferred_element_type=F32)))
        subs.append(chains)

    state = [(c_st[j], m_st[j]) for j in range(2 * N_HEADS)]
    for chains in subs:
        for ch in chains:
            ch["caug"], ch["m_prev"] = state[ch["j"]]
            ch["ia"] = lax.dot_general(ch["caug"].astype(BF16), ch["qb"], _NT, preferred_element_type=F32)
        for ch in chains:
            g_row = ch["b_tot"] - ch["b_row"] + ch["i_row"]
            m_new = jnp.maximum(ch["b_tot"] + ch["m_prev"], jnp.max(g_row, axis=1, keepdims=True))
            wk = jnp.exp(g_row - m_new)
            decay = jnp.exp(ch["b_tot"] + ch["m_prev"] - m_new)
            vw = (ch["vt_aug"].astype(F32) * wk).astype(BF16)
            state[ch["j"]] = (decay * ch["caug"] + _dot(vw, ch["kb"]), m_new)
        for ch in chains:
            inter = ch["b_row"] + ch["m_prev"]
            m_t = jnp.maximum(jnp.max(ch["dmat"], axis=0, keepdims=True), inter)
            st = ch["s_raw"] * jnp.exp(ch["dmat"] - m_t)
            w_inter = jnp.exp(inter - m_t)
            den = jnp.sum(st, axis=0, keepdims=True) + w_inter * ch["ia"][d:d + 1]
            ch["st"] = st.astype(BF16)
            ch["w_inter"] = w_inter
            ch["rden"] = 1.0 / jnp.maximum(jnp.abs(den), jnp.exp(-m_t))
        for ch in chains:
            num = _dot(ch["vt_aug"][:d], ch["st"]) + ch["w_inter"] * ch["ia"][:d]
            ch["h_ref"][0, ch["rows"], ch["hs"]] = (num * ch["rden"]).T
    for j, (caug, m) in enumerate(state):
        c_st[j] = caug
        m_st[j] = m


CAST_SLAB_BYTES = 2 * 1024 * 1024
MLSTM_CHUNKS_PER_STEP = 2


def _cast_slabs(w, nsteps):
    E, R, C = w.shape
    if (E * R) % nsteps:
        return None
    rows = (E * R) // nsteps
    if rows % BF16_ROWS or R % rows or rows * C * 4 > CAST_SLAB_BYTES:
        return None
    return w.reshape(nsteps, rows, C)


def _mlstm(q, k, vt, gt, n_sub, cast_slabs=()):
    B, T, d_a = q.shape
    L = CHUNK * n_sub
    nc = T // L
    ng = gt.shape[1]
    fwd = lambda w: pl.BlockSpec((1, L, w), lambda b, c: (b, c, 0))
    bwd = lambda w: pl.BlockSpec((1, L, w), lambda b, c: (b, nc - 1 - c, 0))
    fwd_t = lambda r: pl.BlockSpec((1, r, L), lambda b, c: (b, 0, c))
    bwd_t = lambda r: pl.BlockSpec((1, r, L), lambda b, c: (b, 0, nc - 1 - c))
    slab = lambda a: pl.BlockSpec((1,) + a.shape[1:], lambda b, c: (b * nc + c, 0, 0))
    in_specs = [fwd(d_a), fwd(d_a), fwd_t(d_a), fwd_t(ng), bwd(d_a), bwd(d_a), bwd_t(d_a), bwd_t(ng)]
    in_specs += [slab(a) for a in cast_slabs]
    out_shape = [jax.ShapeDtypeStruct((B, T, d_a), F32), jax.ShapeDtypeStruct((B, T, d_a), F32)]
    out_shape += [jax.ShapeDtypeStruct(a.shape, BF16) for a in cast_slabs]
    out_specs = [fwd(d_a), bwd(d_a)] + [slab(a) for a in cast_slabs]
    nch = 2 * N_HEADS
    outs = pl.pallas_call(
        functools.partial(_mlstm_kernel, n_cast=len(cast_slabs)),
        out_shape=tuple(out_shape), grid=(B, nc), in_specs=in_specs, out_specs=tuple(out_specs),
        scratch_shapes=[pltpu.VMEM((nch, HEAD_DIM + SUBLANES, HEAD_DIM), F32),
                        pltpu.VMEM((nch, 1, 1), F32)],
        compiler_params=_cparams(("parallel", "arbitrary")),
        name="mlstm")(q, k, vt, gt, q, k, vt, gt, *cast_slabs)
    return outs[0], outs[1], tuple(outs[2:])


MIX_SUB_ROWS = 256


def _mix_out_kernel(hf_ref, hb_ref, og_ref, u_ref, s_ref, x_ref, mnw_ref, sw_ref, sb_ref,
                    wout_ref, fnw_ref, rwt_ref, x1_ref, xn_ref, aff_ref):
    tm = x_ref.shape[0]
    d_a = og_ref.shape[1]
    d_b = u_ref.shape[1]
    sub = min(tm, MIX_SUB_ROWS)
    subs = [slice(i * sub, (i + 1) * sub) for i in range(tm // sub)]
    mnw = mnw_ref[...]
    sbias = sb_ref[...]

    gates = []
    for rs in subs:
        rows = []
        for cc in range(sub // CHUNK):
            r0 = rs.start + cc * CHUNK
            cols = [_dot(sw_ref[g], s_ref[r0:r0 + CHUNK, g * HEAD_DIM:(g + 1) * HEAD_DIM])
                    for g in range(d_b // HEAD_DIM)]
            rows.append(jnp.concatenate(cols, axis=1) + sbias)
        gates.append(jnp.concatenate(rows, axis=0))

    mixes = []
    for rs, gate in zip(subs, gates):
        h = hf_ref[rs, :] + hb_ref[rs, :]
        parts = [_rms(h[:, hd * HEAD_DIM:(hd + 1) * HEAD_DIM], mnw[:, hd * HEAD_DIM:(hd + 1) * HEAD_DIM])
                 for hd in range(d_a // HEAD_DIM)]
        a_out = (og_ref[rs, :].astype(F32) * jnp.concatenate(parts, axis=1)).astype(BF16)
        b_out = (u_ref[rs, :].astype(F32) * gate).astype(BF16)
        mixes.append(jnp.concatenate([a_out, b_out], axis=1))

    x1s = [x_ref[rs, :] + _dot(mix, wout_ref[...]) for rs, mix in zip(subs, mixes)]

    xns = []
    for rs, x1 in zip(subs, x1s):
        x1_ref[rs, :] = x1
        xn = _rms(x1, fnw_ref[...])
        xns.append(xn)
        chunks = xn.shape[1] // LANES
        for j in range(chunks):
            xn_ref[pl.ds(rs.start * chunks + j, sub, stride=chunks), :] = xn[:, j * LANES:(j + 1) * LANES]

    E = rwt_ref.shape[0]
    r0 = rwt_ref[...].astype(BF16)
    r1 = (rwt_ref[...] - r0.astype(F32)).astype(BF16)
    r01 = jnp.concatenate([r0, r1], axis=0)
    logits = []
    for xn in xns:
        x0 = xn.astype(BF16)
        x1 = (xn - x0.astype(F32)).astype(BF16)
        a = lax.dot_general(r01, x0, _NT, preferred_element_type=F32)
        b = lax.dot_general(r0, x1, _NT, preferred_element_type=F32)
        logits.append(a[:E] + a[E:] + b)
    for rs, lg in zip(subs, logits):
        ex = jnp.exp(lg - jnp.max(lg, axis=0, keepdims=True))
        aff = ex / jnp.sum(ex, axis=0, keepdims=True)
        for j in range(sub // LANES):
            aff_ref[rs.start // LANES + j] = aff[:, j * LANES:(j + 1) * LANES]


def _mix_out(hf, hb, og, u, s, x, mnw, sw, sbias, wout, fnw, rwt, tm):
    N, D = x.shape
    d_a = og.shape[1]
    d_b = u.shape[1]
    E = rwt.shape[0]
    nt = N // tm
    full = lambda a: pl.BlockSpec(a.shape, lambda i: (0,) * a.ndim)
    tok = lambda w: pl.BlockSpec((tm, w), lambda i: (i, 0))
    in_specs = [tok(d_a), tok(d_a), tok(d_a), tok(d_b), tok(d_b), tok(D)] + [
        full(a) for a in (mnw, sw, sbias, wout, fnw, rwt)]
    chunks = D // LANES
    out_shape = (jax.ShapeDtypeStruct((N, D), F32), jax.ShapeDtypeStruct((N * chunks, LANES), F32),
                 jax.ShapeDtypeStruct((N // LANES, E, LANES), F32))
    out_specs = (tok(D), pl.BlockSpec((tm * chunks, LANES), lambda i: (i, 0)),
                 pl.BlockSpec((tm // LANES, E, LANES), lambda i: (i, 0, 0)))
    return pl.pallas_call(
        _mix_out_kernel, out_shape=out_shape, grid=(nt,), in_specs=in_specs, out_specs=out_specs,
        compiler_params=_cparams(("parallel",)), name="mix_out")(
            hf, hb, og, u, s, x, mnw, sw, sbias, wout, fnw, rwt)


def _select_kernel(aff_ref, posm_ref, off_ref, cnt_s, wi_s, *, cap):
    nb, E, _ = aff_ref.shape
    aff = aff_ref[...]

    def count_ge(cand):
        c = jnp.sum((aff >= cand).astype(I32), axis=0, keepdims=True)
        return jnp.sum(c, axis=2, keepdims=True)

    def bit_step(i, thr_bits):
        cand = thr_bits | jnp.left_shift(jnp.int32(1), 30 - i)
        return jnp.where(count_ge(pltpu.bitcast(cand, F32)) >= cap, cand, thr_bits)

    thr = pltpu.bitcast(lax.fori_loop(0, 31, bit_step, jnp.zeros((1, E, 1), I32)), F32)
    gt = aff > thr
    eq = aff == thr
    n_gt = jnp.sum(jnp.sum(gt.astype(I32), axis=0, keepdims=True), axis=2, keepdims=True)
    need = cap - n_gt

    li = lax.broadcasted_iota(I32, (LANES, LANES), 0)
    lj = lax.broadcasted_iota(I32, (LANES, LANES), 1)
    upper = (li < lj).astype(BF16)

    def excl_cumsum(flag):
        fb = flag.astype(BF16).reshape(nb * E, LANES)
        wi_s[...] = _dot(fb, upper).astype(I32).reshape(nb, E, LANES)
        cnt_s[...] = jnp.sum(flag.astype(I32), axis=2, keepdims=True)

        def blk(b, run):
            wi_s[b] = wi_s[b] + run
            return run + cnt_s[b]

        lax.fori_loop(0, nb, blk, jnp.zeros((E, 1), I32))
        return wi_s[...]

    eq_rank = excl_cumsum(eq)
    sel = gt | (eq & (eq_rank < need))
    pos = excl_cumsum(sel)
    posm_ref[...] = jnp.where(sel, pos, -1)
    off_ref[...] = jnp.broadcast_to(pos[:, :, 0:1], off_ref.shape)


def _select(aff3, cap):
    nb, E, _ = aff3.shape
    return pl.pallas_call(
        functools.partial(_select_kernel, cap=cap),
        out_shape=(jax.ShapeDtypeStruct((nb, E, LANES), I32), jax.ShapeDtypeStruct((nb, E, LANES), I32)),
        scratch_shapes=[pltpu.VMEM((nb, E, 1), I32), pltpu.VMEM((nb, E, LANES), I32)],
        compiler_params=_cparams(None), name="select")(aff3)


def _compact_kernel(off_sm, posm_ref, aff_ref, acc_ref):
    nb, E, _ = posm_ref.shape
    acc_ref[...] = jnp.zeros_like(acc_ref)
    srow = lax.broadcasted_iota(I32, (2 * LANES, LANES), 0)
    r8 = lax.broadcasted_iota(I32, (SUBLANES, LANES), 0)
    lane8 = lax.broadcasted_iota(I32, (SUBLANES, LANES), 1)

    def blk(b, carry):
        tok = b * LANES + lane8
        t_hi = jnp.right_shift(tok, 8).astype(F32)
        t_lo = jnp.bitwise_and(tok, 255).astype(F32)
        pm = posm_ref[b]
        af = aff_ref[b]
        for e in range(E):
            off = off_sm[b * E + e]
            j0 = jnp.right_shift(off, 7)
            rel = pm[e:e + 1, :] - j0 * LANES
            onehot = (srow == rel).astype(BF16)
            a = af[e:e + 1, :]
            a0 = a.astype(BF16)
            r1 = a - a0.astype(F32)
            a1 = r1.astype(BF16)
            a2 = (r1 - a1.astype(F32)).astype(BF16)
            lhs = jnp.where(r8 == 0, t_hi, jnp.where(r8 == 1, t_lo, 0.0))
            lhs = jnp.where(r8 == 2, a0.astype(F32), lhs)
            lhs = jnp.where(r8 == 3, a1.astype(F32), lhs)
            lhs = jnp.where(r8 == 4, a2.astype(F32), lhs).astype(BF16)
            out = lax.dot_general(lhs, onehot, _NT, preferred_element_type=F32)
            acc_ref[e, j0] = acc_ref[e, j0] + out[:, :LANES]
            acc_ref[e, j0 + 1] = acc_ref[e, j0 + 1] + out[:, LANES:]
        return carry

    lax.fori_loop(0, nb, blk, 0)


def _compact(off_flat, posm3, aff3, cap):
    nb, E, _ = posm3.shape
    nt_pad = cap // LANES + 2
    gs = pltpu.PrefetchScalarGridSpec(
        num_scalar_prefetch=1, grid=(1,),
        in_specs=[pl.BlockSpec(posm3.shape, lambda i, o: (0, 0, 0)),
                  pl.BlockSpec(aff3.shape, lambda i, o: (0, 0, 0))],
        out_specs=pl.BlockSpec((E, nt_pad, SUBLANES, LANES), lambda i, o: (0, 0, 0, 0)))
    return pl.pallas_call(
        _compact_kernel, out_shape=jax.ShapeDtypeStruct((E, nt_pad, SUBLANES, LANES), F32),
        grid_spec=gs, compiler_params=_cparams(("arbitrary",)), name="compact")(off_flat, posm3, aff3)


def _ffn_kernel(idc_sm, idn_sm, xn_hbm, cacc_ref, wg_ref, wu_ref, wd_ref, y_ref, xbuf, sem,
                *, n_pairs, fc):
    g = pl.program_id(0)
    ts = y_ref.shape[0] // 2
    chunks = xbuf.shape[1] // ts
    D = wd_ref.shape[2]
    F = wg_ref.shape[2]
    nchunk = F // fc
    rows_per_chunk = ts // nchunk
    scale_rows = min(ts, LANES)

    def row_copy(idx_sm, i, r, dst_slot):
        src = pl.multiple_of(idx_sm[0, 0, i] * chunks, chunks)
        return pltpu.make_async_copy(xn_hbm.at[pl.ds(src, chunks), :],
                                     xbuf.at[dst_slot, pl.ds(r * chunks, chunks), :], sem.at[dst_slot])

    def wait_tile(slot):
        pltpu.make_async_copy(xn_hbm.at[pl.ds(0, ts * chunks), :], xbuf.at[slot], sem.at[slot]).wait()

    def run_tile(slot, next_idx_sm, next_base):
        xs = xbuf.at[slot]
        x = jnp.concatenate([xs[pl.ds(j, ts, stride=chunks), :] for j in range(chunks)],
                            axis=1).astype(BF16)
        acc = jnp.zeros((ts, D), F32)
        for c in range(nchunk):
            for r in range(c * rows_per_chunk, (c + 1) * rows_per_chunk):
                row_copy(next_idx_sm, next_base + r, r, 1 - slot).start(priority=r % 2)
            cs = slice(c * fc, (c + 1) * fc)
            gte = _dot(x, wg_ref[0, :, cs])
            up = _dot(x, wu_ref[0, :, cs])
            hid = (gte * _sigmoid(gte) * up).astype(BF16)
            acc = acc + _dot(hid, wd_ref[0, cs, :])
        for k in range(ts // scale_rows):
            r0 = slot * ts + k * scale_rows
            t = cacc_ref[0, r0 // LANES]
            vrow = t[2:3, :] + t[3:4, :] + t[4:5, :]
            vmat = jnp.broadcast_to(vrow, (LANES, LANES)).T[r0 % LANES:r0 % LANES + scale_rows]
            vfull = jnp.concatenate([vmat] * (D // LANES), axis=1)
            y_ref[r0:r0 + scale_rows, :] = (
                acc[k * scale_rows:(k + 1) * scale_rows, :] * vfull).astype(y_ref.dtype)

    @pl.when(g == 0)
    def _():
        def body(r, carry):
            row_copy(idc_sm, r, r, 0).start()
            return carry
        lax.fori_loop(0, ts, body, 0, unroll=8)

    @pl.when(g < n_pairs)
    def _():
        wait_tile(0)
        run_tile(0, idc_sm, ts)
        wait_tile(1)
        run_tile(1, idn_sm, 0)

    @pl.when(g >= n_pairs)
    def _():
        wait_tile(0)
        y_ref[...] = jnp.zeros_like(y_ref)


def _ffn(idx, xn, cacc, wg, wu, wd, cap, ts):
    E, D, F = wg.shape
    pairs_per_expert = cap // (2 * ts)
    n_pairs = E * pairs_per_expert
    last = n_pairs - 1
    idx3 = idx.reshape(n_pairs, 1, 2 * ts)
    eidx = lambda g: jnp.minimum(g // pairs_per_expert, E - 1)
    in_specs = [
        pl.BlockSpec((1, 1, 2 * ts), lambda g: (jnp.minimum(g, last), 0, 0), memory_space=pltpu.SMEM),
        pl.BlockSpec((1, 1, 2 * ts), lambda g: (jnp.minimum(g + 1, last), 0, 0), memory_space=pltpu.SMEM),
        pl.BlockSpec(memory_space=pl.ANY),
        pl.BlockSpec((1, 2 * ts // LANES, SUBLANES, LANES),
                     lambda g: (eidx(g), lax.rem(jnp.minimum(g, last), pairs_per_expert), 0, 0)),
        pl.BlockSpec((1, D, F), lambda g: (eidx(g), 0, 0)),
        pl.BlockSpec((1, D, F), lambda g: (eidx(g), 0, 0)),
        pl.BlockSpec((1, F, D), lambda g: (eidx(g), 0, 0)),
    ]
    return pl.pallas_call(
        functools.partial(_ffn_kernel, n_pairs=n_pairs, fc=min(512, F)),
        out_shape=jax.ShapeDtypeStruct(((n_pairs + 1) * 2 * ts, D), BF16),
        grid=(n_pairs + 1,), in_specs=in_specs,
        out_specs=pl.BlockSpec((2 * ts, D), lambda g: (g, 0)),
        scratch_shapes=[pltpu.VMEM((2, ts * (D // LANES), LANES), F32), pltpu.SemaphoreType.DMA((2,))],
        compiler_params=_cparams(("arbitrary",)), name="ffn")(idx3, idx3, xn, cacc, wg, wu, wd)


WIN = 64


def _combine_kernel(off_sm, posm_ref, x1_ref, y_hbm, fnw_ref, o_ref, ycat, yext, sem, sem_ext,
                    *, cap, y_rows, nblk):
    b = pl.program_id(0)
    slot = lax.rem(b, 2)
    nsub, E, _ = posm_ref.shape
    tb = x1_ref.shape[0]
    pm = jnp.concatenate([posm_ref[j] for j in range(nsub)], axis=1)
    wrow = lax.broadcasted_iota(I32, (WIN, tb), 0)

    def starts_of(blk, r):
        out = []
        for e in range(E):
            base = jnp.left_shift(jnp.right_shift(off_sm[blk * E + e], 4), 4)
            st = jnp.minimum(e * cap + base + r * WIN, y_rows - WIN)
            out.append(pl.multiple_of(st, BF16_ROWS))
        return out

    def copies(starts, dst, dsem):
        return [pltpu.make_async_copy(y_hbm.at[pl.ds(starts[e], WIN), :], dst.at[pl.ds(e * WIN, WIN), :], dsem)
                for e in range(E)]

    def onehot(starts):
        ps = []
        for e in range(E):
            pe = pm[e:e + 1, :]
            rel = jnp.where(pe >= 0, pe + (e * cap - starts[e]), -1)
            ps.append((wrow == rel).astype(BF16))
        return jnp.concatenate(ps, axis=0)

    @pl.when(b == 0)
    def _():
        for cp in copies(starts_of(b, 0), ycat.at[0], sem.at[0]):
            cp.start()

    @pl.when(b + 1 < nblk)
    def _():
        for cp in copies(starts_of(b + 1, 0), ycat.at[1 - slot], sem.at[1 - slot]):
            cp.start()

    starts0 = starts_of(b, 0)
    p0 = onehot(starts0)
    for cp in copies(starts0, ycat.at[slot], sem.at[slot]):
        cp.wait()
    acc0 = lax.dot_general(p0, ycat[slot], _TN, preferred_element_type=F32)

    nrounds = jnp.int32(1)
    for e in range(E):
        base = jnp.left_shift(jnp.right_shift(off_sm[b * E + e], 4), 4)
        nrounds = jnp.maximum(nrounds, jnp.right_shift(off_sm[(b + 1) * E + e] - base + (WIN - 1), 6))

    def round_body(r, acc):
        starts = starts_of(b, r)
        cps = copies(starts, yext, sem_ext)
        for cp in cps:
            cp.start()
        p = onehot(starts)
        for cp in cps:
            cp.wait()
        return acc + lax.dot_general(p, yext[...], _TN, preferred_element_type=F32)

    acc = lax.fori_loop(1, nrounds, round_body, acc0)
    o_ref[...] = _rms(x1_ref[...] + acc, fnw_ref[...])


def _combine(off_flat, posm3, x1, y, fnw, cap, tb):
    N, D = x1.shape
    nb, E, _ = posm3.shape
    nsub = tb // LANES
    y_rows = y.shape[0]
    gs = pltpu.PrefetchScalarGridSpec(
        num_scalar_prefetch=1, grid=(N // tb,),
        in_specs=[pl.BlockSpec((nsub, E, LANES), lambda i, o: (i, 0, 0)),
                  pl.BlockSpec((tb, D), lambda i, o: (i, 0)),
                  pl.BlockSpec(memory_space=pl.ANY),
                  pl.BlockSpec(fnw.shape, lambda i, o: (0, 0))],
        out_specs=pl.BlockSpec((tb, D), lambda i, o: (i, 0)),
        scratch_shapes=[pltpu.VMEM((2, E * WIN, D), BF16), pltpu.VMEM((E * WIN, D), BF16),
                        pltpu.SemaphoreType.DMA((2,)), pltpu.SemaphoreType.DMA])
    return pl.pallas_call(
        functools.partial(_combine_kernel, cap=cap, y_rows=y_rows, nblk=N // tb),
        out_shape=jax.ShapeDtypeStruct((N, D), F32), grid_spec=gs,
        compiler_params=_cparams(("arbitrary",)), name="combine")(off_flat, posm3, x1, y, fnw)


def _prep_params(norm_mix_w, w_in, conv_w, conv_b, gate_b, mlstm_norm_w, sgu_norm_w, sgu_w, sgu_b,
                 w_out, norm_ffn_w, router_w, w_gate, w_up, w_down, norm_final_w):
    d_a = mlstm_norm_w.shape[1]
    d_b = sgu_norm_w.shape[1]
    ng = gate_b.shape[1]
    w = w_in[0]
    o0, o1, o2, o3, o4, o5 = 2 * d_a, 3 * d_a, 4 * d_a, 4 * d_a + ng, 4 * d_a + ng + d_b, 4 * d_a + ng + 2 * d_b
    wg = w[:, o2:o3]
    return dict(
        nw=norm_mix_w[0][None, :],
        wqk=w[:, :o0].astype(BF16), wvt=w[:, o0:o1].T.astype(BF16), wo=w[:, o1:o2].astype(BF16),
        wgt=wg.T.astype(BF16),
        wu=w[:, o3:o4].astype(BF16), ws=w[:, o4:o5].astype(BF16),
        cw=conv_w[0], cb=conv_b[0][None, :], gbt=gate_b[0][:, None],
        snw=sgu_norm_w[0][None, :], mnw=mlstm_norm_w[0][None, :],
        sw=sgu_w[0].astype(BF16),
        sbias=jnp.repeat(sgu_b[0].T, HEAD_DIM, axis=1),
        wout=w_out[0].astype(BF16), fnw=norm_ffn_w[0][None, :], rwt=router_w[0].T,
        nfw=norm_final_w[None, :],
    )


def _mixer(x, p, expert_weights):
    B, T, D = x.shape
    q, k, vt, og, gt, u, s = _in_proj(
        x, p["nw"], p["wqk"], p["wvt"], p["wo"], p["wgt"], p["wu"], p["ws"],
        p["cw"], p["cb"], p["gbt"], p["snw"], min(512, T))
    n_sub = MLSTM_CHUNKS_PER_STEP if (T // CHUNK) % MLSTM_CHUNKS_PER_STEP == 0 else 1
    nsteps = B * (T // (CHUNK * n_sub))
    slabs = [_cast_slabs(w, nsteps) for w in expert_weights]
    hf, hb, cast = _mlstm(q, k, vt, gt, n_sub, tuple(sl for sl in slabs if sl is not None))
    cast = list(cast)
    weights_bf16 = [w.astype(BF16) if sl is None else cast.pop(0).reshape(w.shape)
                    for w, sl in zip(expert_weights, slabs)]
    return (hf, hb, og, u, s), weights_bf16


def _moe_tail(x, mixed, p, wgate, wup, wdown):
    B, T, D = x.shape
    N = B * T
    E = N_EXPERTS
    cap = (N * CAPACITY_FACTOR) // E
    hf, hb, og, u, s = mixed
    flat = lambda a: a.reshape(N, a.shape[-1])
    x1, xn, aff3 = _mix_out(flat(hf), flat(hb), flat(og), flat(u), flat(s), flat(x),
                            p["mnw"], p["sw"], p["sbias"], p["wout"], p["fnw"], p["rwt"], min(512, N))
    posm3, off3 = _select(aff3, cap)
    nb = N // LANES
    off_flat = off3[:, :, 0].reshape(nb * E)
    cacc = _compact(off_flat, posm3, aff3, cap)
    nt = cap // LANES
    idx = (cacc[:, :nt, 0, :] * 256.0 + cacc[:, :nt, 1, :]).astype(I32).reshape(E * cap)
    ts = min(512, cap // 2)
    y = _ffn(idx, xn, cacc, wgate, wup, wdown, cap, ts)
    tb = min(256, N)
    sub = tb // LANES
    off_tb = jnp.concatenate([off3[::sub, :, 0], jnp.full((1, E), cap, I32)], axis=0).reshape(-1)
    out = _combine(off_tb, posm3, x1, y, p["nfw"], cap, tb)
    return out.reshape(B, T, D)


def kernel(x_prompt, x_sample, norm_mix_w, w_in, conv_w, conv_b, gate_b, mlstm_norm_w, sgu_norm_w,
           sgu_w, sgu_b, w_out, norm_ffn_w, router_w, w_gate, w_up, w_down, norm_final_w):
    p = _prep_params(norm_mix_w, w_in, conv_w, conv_b, gate_b, mlstm_norm_w, sgu_norm_w, sgu_w,
                     sgu_b, w_out, norm_ffn_w, router_w, w_gate, w_up, w_down, norm_final_w)
    mixed_p, (wgate,) = _mixer(x_prompt, p, [w_gate[0]])
    mixed_s, (wup, wdown) = _mixer(x_sample, p, [w_up[0], w_down[0]])
    return (_moe_tail(x_prompt, mixed_p, p, wgate, wup, wdown),
            _moe_tail(x_sample, mixed_s, p, wgate, wup, wdown))
```

```python
import functools
import math

import jax
import jax.numpy as jnp
from jax import lax
from jax.experimental import pallas as pl
from jax.experimental.pallas import tpu as pltpu

F32 = jnp.float32
BF16 = jnp.bfloat16
I32 = jnp.int32

EPS = 1e-6
N_HEADS = 4
HEAD_DIM = 128
CHUNK = 128
N_EXPERTS = 16
CAPACITY_FACTOR = 2
LANES = 128
SUBLANES = 8
BF16_ROWS = 16
NEG_BIG = -1e30
VMEM_LIMIT = 48 * 1024 * 1024

_NT = (((1,), (1,)), ((), ()))
_TN = (((0,), (0,)), ((), ()))


def _cparams(sem, vmem=VMEM_LIMIT):
    return pltpu.CompilerParams(dimension_semantics=sem, vmem_limit_bytes=vmem)


def _dot(a, b):
    return jnp.dot(a, b, preferred_element_type=F32)


def _sigmoid(x):
    return 1.0 / (1.0 + jnp.exp(-x))


def _gelu(x):
    return 0.5 * x * (1.0 + lax.erf(x * (1.0 / math.sqrt(2.0))))


def _rms(x, w):
    ms = jnp.mean(x * x, axis=-1, keepdims=True)
    return x * lax.rsqrt(ms + EPS) * w


PROJ_COLS = 256


def _in_proj_kernel(x_ref, xp_ref, xn_ref, nw_ref, wqk_ref, wvt_ref, wo_ref, wgt_ref,
                    wu_ref, ws_ref, cw_ref, cb_ref, gbt_ref, snw_ref,
                    q_ref, k_ref, vt_ref, og_ref, gt_ref, u_ref, s_ref):
    i = pl.program_id(1)
    n_i = pl.num_programs(1)
    tm = x_ref.shape[1]
    d_a = q_ref.shape[2]
    nw = nw_ref[...]
    hb = _rms(x_ref[0], nw).astype(BF16)
    hp = jnp.where(i == 0, 0.0, _rms(xp_ref[0], nw)).astype(BF16)
    hn = jnp.where(i == n_i - 1, 0.0, _rms(xn_ref[0], nw)).astype(BF16)
    h_ext = jnp.concatenate([hb, hp, hn], axis=0)
    cw = cw_ref[...]
    cb = cb_ref[...]
    snw = snw_ref[...]
    row = lax.broadcasted_iota(I32, (tm, PROJ_COLS), 0)

    def tile_cols(j):
        return slice(j * PROJ_COLS, (j + 1) * PROJ_COLS)

    def qk_mm(j):
        return _dot(h_ext, wqk_ref[:, tile_cols(j)])

    def qk_ep(j, ze):
        cs = tile_cols(j)
        z = ze[:tm]
        zp = ze[tm + SUBLANES - 1:tm + SUBLANES]
        zn = ze[tm + SUBLANES:tm + SUBLANES + 1]
        z_prev = jnp.where(row == 0, zp, pltpu.roll(z, 1, axis=0))
        z_next = jnp.where(row == tm - 1, zn, pltpu.roll(z, tm - 1, axis=0))
        conv = cb[:, cs] + cw[0:1, cs] * z_prev + cw[1:2, cs] * z + cw[2:3, cs] * z_next
        qk = conv * _sigmoid(conv)
        if cs.start < d_a:
            q_ref[0, :, cs] = qk.astype(BF16)
        else:
            ks = slice(cs.start - d_a, cs.stop - d_a)
            k_ref[0, :, ks] = (qk * (1.0 / math.sqrt(HEAD_DIM))).astype(BF16)

    def u_mm(j):
        return _dot(hb, wu_ref[:, tile_cols(j)])

    def u_ep(j, r):
        u_ref[0, :, tile_cols(j)] = _gelu(r).astype(BF16)

    def s_mm(j):
        return _dot(hb, ws_ref[:, tile_cols(j)])

    def s_ep(j, r):
        cs = tile_cols(j)
        sv = _gelu(r)
        for g in range(PROJ_COLS // HEAD_DIM):
            gs = slice(g * HEAD_DIM, (g + 1) * HEAD_DIM)
            og_cols = slice(cs.start + gs.start, cs.start + gs.stop)
            s_ref[0, :, og_cols] = _rms(sv[:, gs], snw[:, og_cols]).astype(BF16)

    def o_mm(j):
        return _dot(hb, wo_ref[:, tile_cols(j)])

    def o_ep(j, r):
        og_ref[0, :, tile_cols(j)] = _sigmoid(r).astype(BF16)

    def vt_mm(j):
        return lax.dot_general(wvt_ref[tile_cols(j), :], hb, _NT, preferred_element_type=F32)

    def vt_ep(j, r):
        vt_ref[0, tile_cols(j), :] = r.astype(BF16)

    n_qk = wqk_ref.shape[1] // PROJ_COLS
    n_b = wu_ref.shape[1] // PROJ_COLS
    n_a = wo_ref.shape[1] // PROJ_COLS
    light = [(u_mm, u_ep, j) for j in range(n_b)] + [(s_mm, s_ep, j) for j in range(n_b)]
    light = [light[(k // 2) + (k % 2) * n_b] for k in range(2 * n_b)]
    tiles = []
    for j in range(n_qk):
        tiles.append((qk_mm, qk_ep, j))
        if j < len(light):
            tiles.append(light[j])
    tiles += light[n_qk:]
    for j in range(n_a):
        tiles += [(o_mm, o_ep, j), (vt_mm, vt_ep, j)]
    pending = None
    for mm, ep, j in tiles:
        res = mm(j)
        if pending is not None:
            pending[0](pending[1], pending[2])
        pending = (ep, j, res)
    pending[0](pending[1], pending[2])

    zgt = lax.dot_general(wgt_ref[...], hb, _NT, preferred_element_type=F32) + gbt_ref[...]
    rowt = lax.broadcasted_iota(I32, zgt.shape, 0)
    gt_ref[0] = jnp.where(rowt < 2 * N_HEADS, zgt, jax.nn.log_sigmoid(zgt))


def _in_proj(x, nw, wqk, wvt, wo, wgt, wu, ws, cw, cb, gbt, snw, tm):
    B, T, D = x.shape
    d_a = wvt.shape[0]
    d_b = wu.shape[1]
    ng = wgt.shape[0]
    nt = T // tm
    hb8 = tm // SUBLANES
    last8 = T // SUBLANES - 1
    full = lambda a: pl.BlockSpec(a.shape, lambda b, i: (0,) * a.ndim)
    tok = lambda w: pl.BlockSpec((1, tm, w), lambda b, i: (b, i, 0))
    in_specs = [
        pl.BlockSpec((1, tm, D), lambda b, i: (b, i, 0)),
        pl.BlockSpec((1, SUBLANES, D), lambda b, i: (b, jnp.maximum(i * hb8 - 1, 0), 0)),
        pl.BlockSpec((1, SUBLANES, D), lambda b, i: (b, jnp.minimum((i + 1) * hb8, last8), 0)),
    ] + [full(a) for a in (nw, wqk, wvt, wo, wgt, wu, ws, cw, cb, gbt, snw)]
    out_shape = (
        jax.ShapeDtypeStruct((B, T, d_a), BF16), jax.ShapeDtypeStruct((B, T, d_a), BF16),
        jax.ShapeDtypeStruct((B, d_a, T), BF16), jax.ShapeDtypeStruct((B, T, d_a), BF16),
        jax.ShapeDtypeStruct((B, ng, T), F32),
        jax.ShapeDtypeStruct((B, T, d_b), BF16), jax.ShapeDtypeStruct((B, T, d_b), BF16),
    )
    out_specs = (tok(d_a), tok(d_a), pl.BlockSpec((1, d_a, tm), lambda b, i: (b, 0, i)), tok(d_a),
                 pl.BlockSpec((1, ng, tm), lambda b, i: (b, 0, i)), tok(d_b), tok(d_b))
    return pl.pallas_call(
        _in_proj_kernel, out_shape=out_shape, grid=(B, nt), in_specs=in_specs,
        out_specs=out_specs, compiler_params=_cparams(("parallel", "arbitrary")),
        name="in_proj")(x, x, x, nw, wqk, wvt, wo, wgt, wu, ws, cw, cb, gbt, snw)


def _split3(x):
    x0 = x.astype(BF16)
    r1 = x - x0.astype(F32)
    x1 = r1.astype(BF16)
    x2 = (r1 - x1.astype(F32)).astype(BF16)
    return x0, x1, x2


def _mlstm_kernel(*refs, n_cast):
    qf_ref, kf_ref, vtf_ref, gtf_ref, qb_ref, kb_ref, vtb_ref, gtb_ref = refs[:8]
    cast_in = refs[8:8 + n_cast]
    hf_ref, hb_ref = refs[8 + n_cast:10 + n_cast]
    cast_out = refs[10 + n_cast:10 + 2 * n_cast]
    c_st, m_st = refs[10 + 2 * n_cast:]
    for w_ref, o_ref in zip(cast_in, cast_out):
        o_ref[...] = w_ref[...].astype(BF16)

    c = pl.program_id(1)
    L = CHUNK
    n_sub = qf_ref.shape[1] // L
    d = HEAD_DIM

    @pl.when(c == 0)
    def _():
        c_st[...] = jnp.zeros_like(c_st)
        m_st[...] = jnp.zeros_like(m_st)

    r0 = lax.broadcasted_iota(I32, (L, L), 0)
    r1 = lax.broadcasted_iota(I32, (L, L), 1)
    ones8 = jnp.ones((SUBLANES, L), BF16)
    ng = 2 * N_HEADS
    dirs = ((qf_ref, kf_ref, vtf_ref, gtf_ref, r0 <= r1, hf_ref, lambda i: i),
            (qb_ref, kb_ref, vtb_ref, gtb_ref, r0 >= r1, hb_ref, lambda i: n_sub - 1 - i))

    subs = []
    for i in range(n_sub):
        chains = []
        for dr, (q_ref, k_ref, vt_ref, gt_ref, mask_st, h_ref, order) in enumerate(dirs):
            rows = slice(order(i) * L, (order(i) + 1) * L)
            gt = gt_ref[0, :, rows]
            g3 = jnp.concatenate(_split3(gt), axis=0)
            b3 = _dot(g3, mask_st.astype(BF16))
            nr = gt.shape[0]
            br_all = b3[:nr] + b3[nr:2 * nr] + b3[2 * nr:]
            a_rows = gt[:ng] - br_all[ng:]
            a_cols = jnp.concatenate([a_rows, jnp.zeros((L - ng, L), F32)], axis=0).T
            for hd in range(N_HEADS):
                j = dr * N_HEADS + hd
                hs = slice(hd * d, (hd + 1) * d)
                qb = q_ref[0, rows, hs]
                kb = k_ref[0, rows, hs]
                b_row = br_all[ng + j:ng + j + 1, :]
                chains.append(dict(
                    j=j, hs=hs, rows=rows, h_ref=h_ref, qb=qb, kb=kb,
                    vt_aug=jnp.concatenate([vt_ref[0, hs, rows], ones8], axis=0),
                    dmat=jnp.where(mask_st, a_cols[:, j:j + 1] + b_row, NEG_BIG),
                    i_row=gt[j:j + 1, :], b_row=b_row,
                    b_tot=jnp.sum(gt[ng + j:ng + j + 1, :], axis=1, keepdims=True),
                    s_raw=lax.dot_general(kb, qb, _NT, preferred_element_type=F32)))
        subs.append(chains)

    state = [(c_st[j], m_st[j]) for j in range(2 * N_HEADS)]
    for chains in subs:
        for ch in chains:
            ch["caug"], ch["m_prev"] = state[ch["j"]]
            ch["ia"] = lax.dot_general(ch["caug"].astype(BF16), ch["qb"], _NT, preferred_element_type=F32)
        for ch in chains:
            g_row = ch["b_tot"] - ch["b_row"] + ch["i_row"]
            m_new = jnp.maximum(ch["b_tot"] + ch["m_prev"], jnp.max(g_row, axis=1, keepdims=True))
            wk = jnp.exp(g_row - m_new)
            decay = jnp.exp(ch["b_tot"] + ch["m_prev"] - m_new)
            vw = (ch["vt_aug"].astype(F32) * wk).astype(BF16)
            state[ch["j"]] = (decay * ch["caug"] + _dot(vw, ch["kb"]), m_new)
        for ch in chains:
            inter = ch["b_row"] + ch["m_prev"]
            m_t = jnp.maximum(jnp.max(ch["dmat"], axis=0, keepdims=True), inter)
            st = ch["s_raw"] * jnp.exp(ch["dmat"] - m_t)
            w_inter = jnp.exp(inter - m_t)
            den = jnp.sum(st, axis=0, keepdims=True) + w_inter * ch["ia"][d:d + 1]
            ch["st"] = st.astype(BF16)
            ch["w_inter"] = w_inter
            ch["rden"] = 1.0 / jnp.maximum(jnp.abs(den), jnp.exp(-m_t))
        for ch in chains:
            num = _dot(ch["vt_aug"][:d], ch["st"]) + ch["w_inter"] * ch["ia"][:d]
            ch["h_ref"][0, ch["rows"], ch["hs"]] = (num * ch["rden"]).T
    for j, (caug, m) in enumerate(state):
        c_st[j] = caug
        m_st[j] = m


CAST_SLAB_BYTES = 4 * 1024 * 1024
MLSTM_CHUNKS_PER_STEP = 4


def _cast_slabs(w, nsteps):
    E, R, C = w.shape
    if (E * R) % nsteps:
        return None
    rows = (E * R) // nsteps
    if rows % BF16_ROWS or R % rows or rows * C * 4 > CAST_SLAB_BYTES:
        return None
    return w.reshape(nsteps, rows, C)


def _mlstm(q, k, vt, gt, n_sub, cast_slabs=()):
    B, T, d_a = q.shape
    L = CHUNK * n_sub
    nc = T // L
    ng = gt.shape[1]
    fwd = lambda w: pl.BlockSpec((1, L, w), lambda b, c: (b, c, 0))
    bwd = lambda w: pl.BlockSpec((1, L, w), lambda b, c: (b, nc - 1 - c, 0))
    fwd_t = lambda r: pl.BlockSpec((1, r, L), lambda b, c: (b, 0, c))
    bwd_t = lambda r: pl.BlockSpec((1, r, L), lambda b, c: (b, 0, nc - 1 - c))
    slab = lambda a: pl.BlockSpec((1,) + a.shape[1:], lambda b, c: (b * nc + c, 0, 0))
    in_specs = [fwd(d_a), fwd(d_a), fwd_t(d_a), fwd_t(ng), bwd(d_a), bwd(d_a), bwd_t(d_a), bwd_t(ng)]
    in_specs += [slab(a) for a in cast_slabs]
    out_shape = [jax.ShapeDtypeStruct((B, T, d_a), F32), jax.ShapeDtypeStruct((B, T, d_a), F32)]
    out_shape += [jax.ShapeDtypeStruct(a.shape, BF16) for a in cast_slabs]
    out_specs = [fwd(d_a), bwd(d_a)] + [slab(a) for a in cast_slabs]
    nch = 2 * N_HEADS
    outs = pl.pallas_call(
        functools.partial(_mlstm_kernel, n_cast=len(cast_slabs)),
        out_shape=tuple(out_shape), grid=(B, nc), in_specs=in_specs, out_specs=tuple(out_specs),
        scratch_shapes=[pltpu.VMEM((nch, HEAD_DIM + SUBLANES, HEAD_DIM), F32),
                        pltpu.VMEM((nch, 1, 1), F32)],
        compiler_params=_cparams(("parallel", "arbitrary")),
        name="mlstm")(q, k, vt, gt, q, k, vt, gt, *cast_slabs)
    return outs[0], outs[1], tuple(outs[2:])


MIX_SUB_ROWS = 256


def _mix_out_kernel(hf_ref, hb_ref, og_ref, u_ref, s_ref, x_ref, mnw_ref, sw_ref, sb_ref,
                    wout_ref, fnw_ref, rwt_ref, x1_ref, xn_ref, aff_ref):
    tm = x_ref.shape[0]
    d_a = og_ref.shape[1]
    d_b = u_ref.shape[1]
    sub = min(tm, MIX_SUB_ROWS)
    subs = [slice(i * sub, (i + 1) * sub) for i in range(tm // sub)]
    mnw = mnw_ref[...]
    sbias = sb_ref[...]

    gates = []
    for rs in subs:
        rows = []
        for cc in range(sub // CHUNK):
            r0 = rs.start + cc * CHUNK
            cols = [_dot(sw_ref[g], s_ref[r0:r0 + CHUNK, g * HEAD_DIM:(g + 1) * HEAD_DIM])
                    for g in range(d_b // HEAD_DIM)]
            rows.append(jnp.concatenate(cols, axis=1) + sbias)
        gates.append(jnp.concatenate(rows, axis=0))

    mixes = []
    for rs, gate in zip(subs, gates):
        h = hf_ref[rs, :] + hb_ref[rs, :]
        parts = [_rms(h[:, hd * HEAD_DIM:(hd + 1) * HEAD_DIM], mnw[:, hd * HEAD_DIM:(hd + 1) * HEAD_DIM])
                 for hd in range(d_a // HEAD_DIM)]
        a_out = (og_ref[rs, :].astype(F32) * jnp.concatenate(parts, axis=1)).astype(BF16)
        b_out = (u_ref[rs, :].astype(F32) * gate).astype(BF16)
        mixes.append(jnp.concatenate([a_out, b_out], axis=1))

    x1s = [x_ref[rs, :] + _dot(mix, wout_ref[...]) for rs, mix in zip(subs, mixes)]

    xns = []
    for rs, x1 in zip(subs, x1s):
        x1_ref[rs, :] = x1
        xn = _rms(x1, fnw_ref[...])
        xns.append(xn)
        chunks = xn.shape[1] // LANES
        for j in range(chunks):
            xn_ref[pl.ds(rs.start * chunks + j, sub, stride=chunks), :] = xn[:, j * LANES:(j + 1) * LANES]

    E = rwt_ref.shape[0]
    r0 = rwt_ref[...].astype(BF16)
    r1 = (rwt_ref[...] - r0.astype(F32)).astype(BF16)
    r01 = jnp.concatenate([r0, r1], axis=0)
    logits = []
    for xn in xns:
        x0 = xn.astype(BF16)
        x1 = (xn - x0.astype(F32)).astype(BF16)
        a = lax.dot_general(r01, x0, _NT, preferred_element_type=F32)
        b = lax.dot_general(r0, x1, _NT, preferred_element_type=F32)
        logits.append(a[:E] + a[E:] + b)
    for rs, lg in zip(subs, logits):
        ex = jnp.exp(lg - jnp.max(lg, axis=0, keepdims=True))
        aff = ex / jnp.sum(ex, axis=0, keepdims=True)
        for j in range(sub // LANES):
            aff_ref[rs.start // LANES + j] = aff[:, j * LANES:(j + 1) * LANES]


def _mix_out(hf, hb, og, u, s, x, mnw, sw, sbias, wout, fnw, rwt, tm):
    N, D = x.shape
    d_a = og.shape[1]
    d_b = u.shape[1]
    E = rwt.shape[0]
    nt = N // tm
    full = lambda a: pl.BlockSpec(a.shape, lambda i: (0,) * a.ndim)
    tok = lambda w: pl.BlockSpec((tm, w), lambda i: (i, 0))
    in_specs = [tok(d_a), tok(d_a), tok(d_a), tok(d_b), tok(d_b), tok(D)] + [
        full(a) for a in (mnw, sw, sbias, wout, fnw, rwt)]
    chunks = D // LANES
    out_shape = (jax.ShapeDtypeStruct((N, D), F32), jax.ShapeDtypeStruct((N * chunks, LANES), F32),
                 jax.ShapeDtypeStruct((N // LANES, E, LANES), F32))
    out_specs = (tok(D), pl.BlockSpec((tm * chunks, LANES), lambda i: (i, 0)),
                 pl.BlockSpec((tm // LANES, E, LANES), lambda i: (i, 0, 0)))
    return pl.pallas_call(
        _mix_out_kernel, out_shape=out_shape, grid=(nt,), in_specs=in_specs, out_specs=out_specs,
        compiler_params=_cparams(("parallel",)), name="mix_out")(
            hf, hb, og, u, s, x, mnw, sw, sbias, wout, fnw, rwt)


def _select_kernel(aff_ref, posm_ref, off_ref, cnt_s, wi_s, *, cap):
    nb, E, _ = aff_ref.shape
    aff = aff_ref[...]

    def count_ge(cand):
        c = jnp.sum((aff >= cand).astype(I32), axis=0, keepdims=True)
        return jnp.sum(c, axis=2, keepdims=True)

    def bit_step(i, thr_bits):
        cand = thr_bits | jnp.left_shift(jnp.int32(1), 30 - i)
        return jnp.where(count_ge(pltpu.bitcast(cand, F32)) >= cap, cand, thr_bits)

    thr = pltpu.bitcast(lax.fori_loop(0, 31, bit_step, jnp.zeros((1, E, 1), I32)), F32)
    gt = aff > thr
    eq = aff == thr
    n_gt = jnp.sum(jnp.sum(gt.astype(I32), axis=0, keepdims=True), axis=2, keepdims=True)
    need = cap - n_gt

    li = lax.broadcasted_iota(I32, (LANES, LANES), 0)
    lj = lax.broadcasted_iota(I32, (LANES, LANES), 1)
    upper = (li < lj).astype(BF16)

    def excl_cumsum(flag):
        fb = flag.astype(BF16).reshape(nb * E, LANES)
        wi_s[...] = _dot(fb, upper).astype(I32).reshape(nb, E, LANES)
        cnt_s[...] = jnp.sum(flag.astype(I32), axis=2, keepdims=True)

        def blk(b, run):
            wi_s[b] = wi_s[b] + run
            return run + cnt_s[b]

        lax.fori_loop(0, nb, blk, jnp.zeros((E, 1), I32))
        return wi_s[...]

    eq_rank = excl_cumsum(eq)
    sel = gt | (eq & (eq_rank < need))
    pos = excl_cumsum(sel)
    posm_ref[...] = jnp.where(sel, pos, -1)
    off_ref[...] = jnp.broadcast_to(pos[:, :, 0:1], off_ref.shape)


def _select(aff3, cap):
    nb, E, _ = aff3.shape
    return pl.pallas_call(
        functools.partial(_select_kernel, cap=cap),
        out_shape=(jax.ShapeDtypeStruct((nb, E, LANES), I32), jax.ShapeDtypeStruct((nb, E, LANES), I32)),
        scratch_shapes=[pltpu.VMEM((nb, E, 1), I32), pltpu.VMEM((nb, E, LANES), I32)],
        compiler_params=_cparams(None), name="select")(aff3)


def _compact_kernel(off_sm, posm_ref, aff_ref, acc_ref):
    nb, E, _ = posm_ref.shape
    acc_ref[...] = jnp.zeros_like(acc_ref)
    srow = lax.broadcasted_iota(I32, (2 * LANES, LANES), 0)
    r8 = lax.broadcasted_iota(I32, (SUBLANES, LANES), 0)
    lane8 = lax.broadcasted_iota(I32, (SUBLANES, LANES), 1)

    def blk(b, carry):
        tok = b * LANES + lane8
        t_hi = jnp.right_shift(tok, 8).astype(F32)
        t_lo = jnp.bitwise_and(tok, 255).astype(F32)
        pm = posm_ref[b]
        af = aff_ref[b]
        for e in range(E):
            off = off_sm[b * E + e]
            j0 = jnp.right_shift(off, 7)
            rel = pm[e:e + 1, :] - j0 * LANES
            onehot = (srow == rel).astype(BF16)
            a = af[e:e + 1, :]
            a0 = a.astype(BF16)
            r1 = a - a0.astype(F32)
            a1 = r1.astype(BF16)
            a2 = (r1 - a1.astype(F32)).astype(BF16)
            lhs = jnp.where(r8 == 0, t_hi, jnp.where(r8 == 1, t_lo, 0.0))
            lhs = jnp.where(r8 == 2, a0.astype(F32), lhs)
            lhs = jnp.where(r8 == 3, a1.astype(F32), lhs)
            lhs = jnp.where(r8 == 4, a2.astype(F32), lhs).astype(BF16)
            out = lax.dot_general(lhs, onehot, _NT, preferred_element_type=F32)
            acc_ref[e, j0] = acc_ref[e, j0] + out[:, :LANES]
            acc_ref[e, j0 + 1] = acc_ref[e, j0 + 1] + out[:, LANES:]
        return carry

    lax.fori_loop(0, nb, blk, 0)


def _compact(off_flat, posm3, aff3, cap):
    nb, E, _ = posm3.shape
    nt_pad = cap // LANES + 2
    gs = pltpu.PrefetchScalarGridSpec(
        num_scalar_prefetch=1, grid=(1,),
        in_specs=[pl.BlockSpec(posm3.shape, lambda i, o: (0, 0, 0)),
                  pl.BlockSpec(aff3.shape, lambda i, o: (0, 0, 0))],
        out_specs=pl.BlockSpec((E, nt_pad, SUBLANES, LANES), lambda i, o: (0, 0, 0, 0)))
    return pl.pallas_call(
        _compact_kernel, out_shape=jax.ShapeDtypeStruct((E, nt_pad, SUBLANES, LANES), F32),
        grid_spec=gs, compiler_params=_cparams(("arbitrary",)), name="compact")(off_flat, posm3, aff3)


def _ffn_kernel(idc_sm, idn_sm, xn_hbm, cacc_ref, wg_ref, wu_ref, wd_ref, y_ref, xbuf, sem,
                *, n_pairs, fc):
    g = pl.program_id(0)
    ts = y_ref.shape[0] // 2
    chunks = xbuf.shape[1] // ts
    D = wd_ref.shape[2]
    F = wg_ref.shape[2]
    nchunk = F // fc
    rows_per_chunk = ts // nchunk
    scale_rows = min(ts, LANES)

    def row_copy(idx_sm, i, r, dst_slot):
        src = pl.multiple_of(idx_sm[0, 0, i] * chunks, chunks)
        return pltpu.make_async_copy(xn_hbm.at[pl.ds(src, chunks), :],
                                     xbuf.at[dst_slot, pl.ds(r * chunks, chunks), :], sem.at[dst_slot])

    def wait_tile(slot):
        pltpu.make_async_copy(xn_hbm.at[pl.ds(0, ts * chunks), :], xbuf.at[slot], sem.at[slot]).wait()

    def run_tile(slot, gathers):
        xs = xbuf.at[slot]
        x = jnp.concatenate([xs[pl.ds(j, ts, stride=chunks), :] for j in range(chunks)],
                            axis=1).astype(BF16)
        acc = jnp.zeros((ts, D), F32)
        for c in range(nchunk):
            for idx_sm, base, dst in gathers:
                for r in range(c * rows_per_chunk, (c + 1) * rows_per_chunk):
                    row_copy(idx_sm, base + r, r, dst).start(priority=r % 2)
            cs = slice(c * fc, (c + 1) * fc)
            gte = _dot(x, wg_ref[0, :, cs])
            up = _dot(x, wu_ref[0, :, cs])
            hid = (gte * _sigmoid(gte) * up).astype(BF16)
            acc = acc + _dot(hid, wd_ref[0, cs, :])
        for k in range(ts // scale_rows):
            r0 = slot * ts + k * scale_rows
            t = cacc_ref[0, r0 // LANES]
            vrow = t[2:3, :] + t[3:4, :] + t[4:5, :]
            vmat = jnp.broadcast_to(vrow, (LANES, LANES)).T[r0 % LANES:r0 % LANES + scale_rows]
            vfull = jnp.concatenate([vmat] * (D // LANES), axis=1)
            y_ref[r0:r0 + scale_rows, :] = (
                acc[k * scale_rows:(k + 1) * scale_rows, :] * vfull).astype(y_ref.dtype)

    @pl.when(g == 0)
    def _():
        def body(r, carry):
            row_copy(idc_sm, r, r, 0).start()
            return carry
        lax.fori_loop(0, ts, body, 0, unroll=8)

    @pl.when(g < n_pairs)
    def _():
        wait_tile(0)
        run_tile(0, [(idc_sm, ts, 1), (idn_sm, 0, 0)])
        wait_tile(1)
        run_tile(1, [])

    @pl.when(g >= n_pairs)
    def _():
        wait_tile(0)
        y_ref[...] = jnp.zeros_like(y_ref)


def _ffn(idx, xn, cacc, wg, wu, wd, cap, ts):
    E, D, F = wg.shape
    pairs_per_expert = cap // (2 * ts)
    n_pairs = E * pairs_per_expert
    last = n_pairs - 1
    idx3 = idx.reshape(n_pairs, 1, 2 * ts)
    eidx = lambda g: jnp.minimum(g // pairs_per_expert, E - 1)
    in_specs = [
        pl.BlockSpec((1, 1, 2 * ts), lambda g: (jnp.minimum(g, last), 0, 0), memory_space=pltpu.SMEM),
        pl.BlockSpec((1, 1, 2 * ts), lambda g: (jnp.minimum(g + 1, last), 0, 0), memory_space=pltpu.SMEM),
        pl.BlockSpec(memory_space=pl.ANY),
        pl.BlockSpec((1, 2 * ts // LANES, SUBLANES, LANES),
                     lambda g: (eidx(g), lax.rem(jnp.minimum(g, last), pairs_per_expert), 0, 0)),
        pl.BlockSpec((1, D, F), lambda g: (eidx(g), 0, 0)),
        pl.BlockSpec((1, D, F), lambda g: (eidx(g), 0, 0)),
        pl.BlockSpec((1, F, D), lambda g: (eidx(g), 0, 0)),
    ]
    return pl.pallas_call(
        functools.partial(_ffn_kernel, n_pairs=n_pairs, fc=min(512, F)),
        out_shape=jax.ShapeDtypeStruct(((n_pairs + 1) * 2 * ts, D), BF16),
        grid=(n_pairs + 1,), in_specs=in_specs,
        out_specs=pl.BlockSpec((2 * ts, D), lambda g: (g, 0)),
        scratch_shapes=[pltpu.VMEM((2, ts * (D // LANES), LANES), F32), pltpu.SemaphoreType.DMA((2,))],
        compiler_params=_cparams(("arbitrary",)), name="ffn")(idx3, idx3, xn, cacc, wg, wu, wd)


WIN = 64


def _combine_kernel(off_sm, posm_ref, x1_ref, y_hbm, fnw_ref, o_ref, ycat, yext, sem, sem_ext,
                    *, cap, y_rows, nblk):
    b = pl.program_id(0)
    slot = lax.rem(b, 2)
    nsub, E, _ = posm_ref.shape
    tb = x1_ref.shape[0]
    pm = jnp.concatenate([posm_ref[j] for j in range(nsub)], axis=1)
    wrow = lax.broadcasted_iota(I32, (WIN, tb), 0)

    def starts_of(blk, r):
        out = []
        for e in range(E):
            base = jnp.left_shift(jnp.right_shift(off_sm[blk * E + e], 4), 4)
            st = jnp.minimum(e * cap + base + r * WIN, y_rows - WIN)
            out.append(pl.multiple_of(st, BF16_ROWS))
        return out

    def copies(starts, dst, dsem):
        return [pltpu.make_async_copy(y_hbm.at[pl.ds(starts[e], WIN), :], dst.at[pl.ds(e * WIN, WIN), :], dsem)
                for e in range(E)]

    def onehot(starts):
        ps = []
        for e in range(E):
            pe = pm[e:e + 1, :]
            rel = jnp.where(pe >= 0, pe + (e * cap - starts[e]), -1)
            ps.append((wrow == rel).astype(BF16))
        return jnp.concatenate(ps, axis=0)

    @pl.when(b == 0)
    def _():
        for cp in copies(starts_of(b, 0), ycat.at[0], sem.at[0]):
            cp.start()

    @pl.when(b + 1 < nblk)
    def _():
        for cp in copies(starts_of(b + 1, 0), ycat.at[1 - slot], sem.at[1 - slot]):
            cp.start()

    starts0 = starts_of(b, 0)
    p0 = onehot(starts0)
    for cp in copies(starts0, ycat.at[slot], sem.at[slot]):
        cp.wait()
    acc0 = lax.dot_general(p0, ycat[slot], _TN, preferred_element_type=F32)

    nrounds = jnp.int32(1)
    for e in range(E):
        base = jnp.left_shift(jnp.right_shift(off_sm[b * E + e], 4), 4)
        nrounds = jnp.maximum(nrounds, jnp.right_shift(off_sm[(b + 1) * E + e] - base + (WIN - 1), 6))

    def round_body(r, acc):
        starts = starts_of(b, r)
        cps = copies(starts, yext, sem_ext)
        for cp in cps:
            cp.start()
        p = onehot(starts)
        for cp in cps:
            cp.wait()
        return acc + lax.dot_general(p, yext[...], _TN, preferred_element_type=F32)

    acc = lax.fori_loop(1, nrounds, round_body, acc0)
    o_ref[...] = _rms(x1_ref[...] + acc, fnw_ref[...])


def _combine(off_flat, posm3, x1, y, fnw, cap, tb):
    N, D = x1.shape
    nb, E, _ = posm3.shape
    nsub = tb // LANES
    y_rows = y.shape[0]
    gs = pltpu.PrefetchScalarGridSpec(
        num_scalar_prefetch=1, grid=(N // tb,),
        in_specs=[pl.BlockSpec((nsub, E, LANES), lambda i, o: (i, 0, 0)),
                  pl.BlockSpec((tb, D), lambda i, o: (i, 0)),
                  pl.BlockSpec(memory_space=pl.ANY),
                  pl.BlockSpec(fnw.shape, lambda i, o: (0, 0))],
        out_specs=pl.BlockSpec((tb, D), lambda i, o: (i, 0)),
        scratch_shapes=[pltpu.VMEM((2, E * WIN, D), BF16), pltpu.VMEM((E * WIN, D), BF16),
                        pltpu.SemaphoreType.DMA((2,)), pltpu.SemaphoreType.DMA])
    return pl.pallas_call(
        functools.partial(_combine_kernel, cap=cap, y_rows=y_rows, nblk=N // tb),
        out_shape=jax.ShapeDtypeStruct((N, D), F32), grid_spec=gs,
        compiler_params=_cparams(("arbitrary",)), name="combine")(off_flat, posm3, x1, y, fnw)


def _prep_params(norm_mix_w, w_in, conv_w, conv_b, gate_b, mlstm_norm_w, sgu_norm_w, sgu_w, sgu_b,
                 w_out, norm_ffn_w, router_w, w_gate, w_up, w_down, norm_final_w):
    d_a = mlstm_norm_w.shape[1]
    d_b = sgu_norm_w.shape[1]
    ng = gate_b.shape[1]
    w = w_in[0]
    o0, o1, o2, o3, o4, o5 = 2 * d_a, 3 * d_a, 4 * d_a, 4 * d_a + ng, 4 * d_a + ng + d_b, 4 * d_a + ng + 2 * d_b
    wg = w[:, o2:o3]
    return dict(
        nw=norm_mix_w[0][None, :],
        wqk=w[:, :o0].astype(BF16), wvt=w[:, o0:o1].T.astype(BF16), wo=w[:, o1:o2].astype(BF16),
        wgt=wg.T.astype(BF16),
        wu=w[:, o3:o4].astype(BF16), ws=w[:, o4:o5].astype(BF16),
        cw=conv_w[0], cb=conv_b[0][None, :], gbt=gate_b[0][:, None],
        snw=sgu_norm_w[0][None, :], mnw=mlstm_norm_w[0][None, :],
        sw=sgu_w[0].astype(BF16),
        sbias=jnp.repeat(sgu_b[0].T, HEAD_DIM, axis=1),
        wout=w_out[0].astype(BF16), fnw=norm_ffn_w[0][None, :], rwt=router_w[0].T,
        nfw=norm_final_w[None, :],
    )


def _mixer(x, p, expert_weights):
    B, T, D = x.shape
    q, k, vt, og, gt, u, s = _in_proj(
        x, p["nw"], p["wqk"], p["wvt"], p["wo"], p["wgt"], p["wu"], p["ws"],
        p["cw"], p["cb"], p["gbt"], p["snw"], min(512, T))
    n_sub = MLSTM_CHUNKS_PER_STEP if (T // CHUNK) % MLSTM_CHUNKS_PER_STEP == 0 else 1
    nsteps = B * (T // (CHUNK * n_sub))
    slabs = [_cast_slabs(w, nsteps) for w in expert_weights]
    hf, hb, cast = _mlstm(q, k, vt, gt, n_sub, tuple(sl for sl in slabs if sl is not None))
    cast = list(cast)
    weights_bf16 = [w.astype(BF16) if sl is None else cast.pop(0).reshape(w.shape)
                    for w, sl in zip(expert_weights, slabs)]
    return (hf, hb, og, u, s), weights_bf16


def _moe_tail(x, mixed, p, wgate, wup, wdown):
    B, T, D = x.shape
    N = B * T
    E = N_EXPERTS
    cap = (N * CAPACITY_FACTOR) // E
    hf, hb, og, u, s = mixed
    flat = lambda a: a.reshape(N, a.shape[-1])
    x1, xn, aff3 = _mix_out(flat(hf), flat(hb), flat(og), flat(u), flat(s), flat(x),
                            p["mnw"], p["sw"], p["sbias"], p["wout"], p["fnw"], p["rwt"], min(512, N))
    posm3, off3 = _select(aff3, cap)
    nb = N // LANES
    off_flat = off3[:, :, 0].reshape(nb * E)
    cacc = _compact(off_flat, posm3, aff3, cap)
    nt = cap // LANES
    idx = (cacc[:, :nt, 0, :] * 256.0 + cacc[:, :nt, 1, :]).astype(I32).reshape(E * cap)
    ts = min(512, cap // 2)
    y = _ffn(idx, xn, cacc, wgate, wup, wdown, cap, ts)
    tb = min(256, N)
    sub = tb // LANES
    off_tb = jnp.concatenate([off3[::sub, :, 0], jnp.full((1, E), cap, I32)], axis=0).reshape(-1)
    out = _combine(off_tb, posm3, x1, y, p["nfw"], cap, tb)
    return out.reshape(B, T, D)


def kernel(x_prompt, x_sample, norm_mix_w, w_in, conv_w, conv_b, gate_b, mlstm_norm_w, sgu_norm_w,
           sgu_w, sgu_b, w_out, norm_ffn_w, router_w, w_gate, w_up, w_down, norm_final_w):
    p = _prep_params(norm_mix_w, w_in, conv_w, conv_b, gate_b, mlstm_norm_w, sgu_norm_w, sgu_w,
                     sgu_b, w_out, norm_ffn_w, router_w, w_gate, w_up, w_down, norm_final_w)
    mixed_p, (wgate,) = _mixer(x_prompt, p, [w_gate[0]])
    mixed_s, (wup, wdown) = _mixer(x_sample, p, [w_up[0], w_down[0]])
    return (_moe_tail(x_prompt, mixed_p, p, wgate, wup, wdown),
            _moe_tail(x_sample, mixed_s, p, wgate, wup, wdown))
```

```python
import functools
import math

import jax
import jax.numpy as jnp
from jax import lax
from jax.experimental import pallas as pl
from jax.experimental.pallas import tpu as pltpu

F32 = jnp.float32
BF16 = jnp.bfloat16
I32 = jnp.int32

EPS = 1e-6
N_HEADS = 4
HEAD_DIM = 128
CHUNK = 128
N_EXPERTS = 16
CAPACITY_FACTOR = 2
LANES = 128
SUBLANES = 8
BF16_ROWS = 16
NEG_BIG = -1e30
VMEM_LIMIT = 48 * 1024 * 1024

_NT = (((1,), (1,)), ((), ()))
_TN = (((0,), (0,)), ((), ()))


def _cparams(sem, vmem=VMEM_LIMIT):
    return pltpu.CompilerParams(dimension_semantics=sem, vmem_limit_bytes=vmem)


def _dot(a, b):
    return jnp.dot(a, b, preferred_element_type=F32)


def _sigmoid(x):
    return 1.0 / (1.0 + jnp.exp(-x))


def _gelu(x):
    return 0.5 * x * (1.0 + lax.erf(x * (1.0 / math.sqrt(2.0))))


def _rms(x, w):
    ms = jnp.mean(x * x, axis=-1, keepdims=True)
    return x * lax.rsqrt(ms + EPS) * w


PROJ_COLS = 256


def _in_proj_kernel(x_ref, xp_ref, xn_ref, nw_ref, wqk_ref, wvt_ref, wo_ref, wgt_ref,
                    wu_ref, ws_ref, cw_ref, cb_ref, gbt_ref, snw_ref,
                    q_ref, k_ref, vt_ref, og_ref, gt_ref, u_ref, s_ref):
    i = pl.program_id(1)
    n_i = pl.num_programs(1)
    tm = x_ref.shape[1]
    d_a = q_ref.shape[2]
    nw = nw_ref[...]
    hb = _rms(x_ref[0], nw).astype(BF16)
    hp = jnp.where(i == 0, 0.0, _rms(xp_ref[0], nw)).astype(BF16)
    hn = jnp.where(i == n_i - 1, 0.0, _rms(xn_ref[0], nw)).astype(BF16)
    h_ext = jnp.concatenate([hb, hp, hn], axis=0)
    cw = cw_ref[...]
    cb = cb_ref[...]
    snw = snw_ref[...]
    row = lax.broadcasted_iota(I32, (tm, PROJ_COLS), 0)

    def tile_cols(j):
        return slice(j * PROJ_COLS, (j + 1) * PROJ_COLS)

    def qk_mm(j):
        return _dot(h_ext, wqk_ref[:, tile_cols(j)])

    def qk_ep(j, ze):
        cs = tile_cols(j)
        z = ze[:tm]
        zp = ze[tm + SUBLANES - 1:tm + SUBLANES]
        zn = ze[tm + SUBLANES:tm + SUBLANES + 1]
        z_prev = jnp.where(row == 0, zp, pltpu.roll(z, 1, axis=0))
        z_next = jnp.where(row == tm - 1, zn, pltpu.roll(z, tm - 1, axis=0))
        conv = cb[:, cs] + cw[0:1, cs] * z_prev + cw[1:2, cs] * z + cw[2:3, cs] * z_next
        qk = conv * _sigmoid(conv)
        if cs.start < d_a:
            q_ref[0, :, cs] = qk.astype(BF16)
        else:
            ks = slice(cs.start - d_a, cs.stop - d_a)
            k_ref[0, :, ks] = (qk * (1.0 / math.sqrt(HEAD_DIM))).astype(BF16)

    def u_mm(j):
        return _dot(hb, wu_ref[:, tile_cols(j)])

    def u_ep(j, r):
        u_ref[0, :, tile_cols(j)] = _gelu(r).astype(BF16)

    def s_mm(j):
        return _dot(hb, ws_ref[:, tile_cols(j)])

    def s_ep(j, r):
        cs = tile_cols(j)
        sv = _gelu(r)
        for g in range(PROJ_COLS // HEAD_DIM):
            gs = slice(g * HEAD_DIM, (g + 1) * HEAD_DIM)
            og_cols = slice(cs.start + gs.start, cs.start + gs.stop)
            s_ref[0, :, og_cols] = _rms(sv[:, gs], snw[:, og_cols]).astype(BF16)

    def o_mm(j):
        return _dot(hb, wo_ref[:, tile_cols(j)])

    def o_ep(j, r):
        og_ref[0, :, tile_cols(j)] = _sigmoid(r).astype(BF16)

    def vt_mm(j):
        return lax.dot_general(wvt_ref[tile_cols(j), :], hb, _NT, preferred_element_type=F32)

    def vt_ep(j, r):
        vt_ref[0, tile_cols(j), :] = r.astype(BF16)

    n_qk = wqk_ref.shape[1] // PROJ_COLS
    n_b = wu_ref.shape[1] // PROJ_COLS
    n_a = wo_ref.shape[1] // PROJ_COLS
    light = [(u_mm, u_ep, j) for j in range(n_b)] + [(s_mm, s_ep, j) for j in range(n_b)]
    light = [light[(k // 2) + (k % 2) * n_b] for k in range(2 * n_b)]
    tiles = []
    for j in range(n_qk):
        tiles.append((qk_mm, qk_ep, j))
        if j < len(light):
            tiles.append(light[j])
    tiles += light[n_qk:]
    for j in range(n_a):
        tiles += [(o_mm, o_ep, j), (vt_mm, vt_ep, j)]
    pending = None
    for mm, ep, j in tiles:
        res = mm(j)
        if pending is not None:
            pending[0](pending[1], pending[2])
        pending = (ep, j, res)
    pending[0](pending[1], pending[2])

    zgt = lax.dot_general(wgt_ref[...], hb, _NT, preferred_element_type=F32) + gbt_ref[...]
    rowt = lax.broadcasted_iota(I32, zgt.shape, 0)
    gt_ref[0] = jnp.where(rowt < 2 * N_HEADS, zgt, jax.nn.log_sigmoid(zgt))


def _in_proj(x, nw, wqk, wvt, wo, wgt, wu, ws, cw, cb, gbt, snw, tm):
    B, T, D = x.shape
    d_a = wvt.shape[0]
    d_b = wu.shape[1]
    ng = wgt.shape[0]
    nt = T // tm
    hb8 = tm // SUBLANES
    last8 = T // SUBLANES - 1
    full = lambda a: pl.BlockSpec(a.shape, lambda b, i: (0,) * a.ndim)
    tok = lambda w: pl.BlockSpec((1, tm, w), lambda b, i: (b, i, 0))
    in_specs = [
        pl.BlockSpec((1, tm, D), lambda b, i: (b, i, 0)),
        pl.BlockSpec((1, SUBLANES, D), lambda b, i: (b, jnp.maximum(i * hb8 - 1, 0), 0)),
        pl.BlockSpec((1, SUBLANES, D), lambda b, i: (b, jnp.minimum((i + 1) * hb8, last8), 0)),
    ] + [full(a) for a in (nw, wqk, wvt, wo, wgt, wu, ws, cw, cb, gbt, snw)]
    out_shape = (
        jax.ShapeDtypeStruct((B, T, d_a), BF16), jax.ShapeDtypeStruct((B, T, d_a), BF16),
        jax.ShapeDtypeStruct((B, d_a, T), BF16), jax.ShapeDtypeStruct((B, T, d_a), BF16),
        jax.ShapeDtypeStruct((B, ng, T), F32),
        jax.ShapeDtypeStruct((B, T, d_b), BF16), jax.ShapeDtypeStruct((B, T, d_b), BF16),
    )
    out_specs = (tok(d_a), tok(d_a), pl.BlockSpec((1, d_a, tm), lambda b, i: (b, 0, i)), tok(d_a),
                 pl.BlockSpec((1, ng, tm), lambda b, i: (b, 0, i)), tok(d_b), tok(d_b))
    return pl.pallas_call(
        _in_proj_kernel, out_shape=out_shape, grid=(B, nt), in_specs=in_specs,
        out_specs=out_specs, compiler_params=_cparams(("parallel", "arbitrary")),
        name="in_proj")(x, x, x, nw, wqk, wvt, wo, wgt, wu, ws, cw, cb, gbt, snw)


def _split3(x):
    x0 = x.astype(BF16)
    r1 = x - x0.astype(F32)
    x1 = r1.astype(BF16)
    x2 = (r1 - x1.astype(F32)).astype(BF16)
    return x0, x1, x2


def _mlstm_kernel(*refs, n_cast):
    qf_ref, kf_ref, vtf_ref, gtf_ref, qb_ref, kb_ref, vtb_ref, gtb_ref = refs[:8]
    cast_in = refs[8:8 + n_cast]
    hf_ref, hb_ref = refs[8 + n_cast:10 + n_cast]
    cast_out = refs[10 + n_cast:10 + 2 * n_cast]
    c_st, m_st = refs[10 + 2 * n_cast:]
    for w_ref, o_ref in zip(cast_in, cast_out):
        o_ref[...] = w_ref[...].astype(BF16)

    c = pl.program_id(1)
    L = CHUNK
    n_sub = qf_ref.shape[1] // L
    d = HEAD_DIM

    @pl.when(c == 0)
    def _():
        c_st[...] = jnp.zeros_like(c_st)
        m_st[...] = jnp.zeros_like(m_st)

    r0 = lax.broadcasted_iota(I32, (L, L), 0)
    r1 = lax.broadcasted_iota(I32, (L, L), 1)
    ones8 = jnp.ones((SUBLANES, L), BF16)
    ng = 2 * N_HEADS
    dirs = ((qf_ref, kf_ref, vtf_ref, gtf_ref, r0 <= r1, hf_ref, lambda i: i),
            (qb_ref, kb_ref, vtb_ref, gtb_ref, r0 >= r1, hb_ref, lambda i: n_sub - 1 - i))

    subs = []
    for i in range(n_sub):
        chains = []
        for dr, (q_ref, k_ref, vt_ref, gt_ref, mask_st, h_ref, order) in enumerate(dirs):
            rows = slice(order(i) * L, (order(i) + 1) * L)
            gt = gt_ref[0, :, rows]
            g3 = jnp.concatenate(_split3(gt), axis=0)
            b3 = _dot(g3, mask_st.astype(BF16))
            nr = gt.shape[0]
            br_all = b3[:nr] + b3[nr:2 * nr] + b3[2 * nr:]
            a_rows = gt[:ng] - br_all[ng:]
            a_cols = jnp.concatenate([a_rows, jnp.zeros((L - ng, L), F32)], axis=0).T
            for hd in range(N_HEADS):
                j = dr * N_HEADS + hd
                hs = slice(hd * d, (hd + 1) * d)
                qb = q_ref[0, rows, hs]
                kb = k_ref[0, rows, hs]
                b_row = br_all[ng + j:ng + j + 1, :]
                chains.append(dict(
                    j=j, hs=hs, rows=rows, h_ref=h_ref, qb=qb, kb=kb,
                    vt_aug=jnp.concatenate([vt_ref[0, hs, rows], ones8], axis=0),
                    dmat=jnp.where(mask_st, a_cols[:, j:j + 1] + b_row, NEG_BIG),
                    i_row=gt[j:j + 1, :], b_row=b_row,
                    b_tot=jnp.sum(gt[ng + j:ng + j + 1, :], axis=1, keepdims=True),
                    s_raw=lax.dot_general(kb, qb, _NT, preferred_element_type=F32)))
        subs.append(chains)

    state = [(c_st[j], m_st[j]) for j in range(2 * N_HEADS)]
    for chains in subs:
        for ch in chains:
            ch["caug"], ch["m_prev"] = state[ch["j"]]
            ch["ia"] = lax.dot_general(ch["caug"].astype(BF16), ch["qb"], _NT, preferred_element_type=F32)
        for ch in chains:
            g_row = ch["b_tot"] - ch["b_row"] + ch["i_row"]
            m_new = jnp.maximum(ch["b_tot"] + ch["m_prev"], jnp.max(g_row, axis=1, keepdims=True))
            wk = jnp.exp(g_row - m_new)
            decay = jnp.exp(ch["b_tot"] + ch["m_prev"] - m_new)
            vw = (ch["vt_aug"].astype(F32) * wk).astype(BF16)
            state[ch["j"]] = (decay * ch["caug"] + _dot(vw, ch["kb"]), m_new)
        for ch in chains:
            inter = ch["b_row"] + ch["m_prev"]
            m_t = jnp.maximum(jnp.max(ch["dmat"], axis=0, keepdims=True), inter)
            st = ch["s_raw"] * jnp.exp(ch["dmat"] - m_t)
            w_inter = jnp.exp(inter - m_t)
            den = jnp.sum(st, axis=0, keepdims=True) + w_inter * ch["ia"][d:d + 1]
            ch["st"] = st.astype(BF16)
            ch["w_inter"] = w_inter
            ch["rden"] = 1.0 / jnp.maximum(jnp.abs(den), jnp.exp(-m_t))
        for ch in chains:
            num = _dot(ch["vt_aug"][:d], ch["st"]) + ch["w_inter"] * ch["ia"][:d]
            ch["h_ref"][0, ch["rows"], ch["hs"]] = (num * ch["rden"]).T.astype(ch["h_ref"].dtype)
    for j, (caug, m) in enumerate(state):
        c_st[j] = caug
        m_st[j] = m


CAST_SLAB_BYTES = 4 * 1024 * 1024
MLSTM_CHUNKS_PER_STEP = 4


def _cast_slabs(w, nsteps):
    E, R, C = w.shape
    if (E * R) % nsteps:
        return None
    rows = (E * R) // nsteps
    if rows % BF16_ROWS or R % rows or rows * C * 4 > CAST_SLAB_BYTES:
        return None
    return w.reshape(nsteps, rows, C)


def _mlstm(q, k, vt, gt, n_sub, cast_slabs=()):
    B, T, d_a = q.shape
    L = CHUNK * n_sub
    nc = T // L
    ng = gt.shape[1]
    fwd = lambda w: pl.BlockSpec((1, L, w), lambda b, c: (b, c, 0))
    bwd = lambda w: pl.BlockSpec((1, L, w), lambda b, c: (b, nc - 1 - c, 0))
    fwd_t = lambda r: pl.BlockSpec((1, r, L), lambda b, c: (b, 0, c))
    bwd_t = lambda r: pl.BlockSpec((1, r, L), lambda b, c: (b, 0, nc - 1 - c))
    slab = lambda a: pl.BlockSpec((1,) + a.shape[1:], lambda b, c: (b * nc + c, 0, 0))
    in_specs = [fwd(d_a), fwd(d_a), fwd_t(d_a), fwd_t(ng), bwd(d_a), bwd(d_a), bwd_t(d_a), bwd_t(ng)]
    in_specs += [slab(a) for a in cast_slabs]
    out_shape = [jax.ShapeDtypeStruct((B, T, d_a), BF16), jax.ShapeDtypeStruct((B, T, d_a), BF16)]
    out_shape += [jax.ShapeDtypeStruct(a.shape, BF16) for a in cast_slabs]
    out_specs = [fwd(d_a), bwd(d_a)] + [slab(a) for a in cast_slabs]
    nch = 2 * N_HEADS
    outs = pl.pallas_call(
        functools.partial(_mlstm_kernel, n_cast=len(cast_slabs)),
        out_shape=tuple(out_shape), grid=(B, nc), in_specs=in_specs, out_specs=tuple(out_specs),
        scratch_shapes=[pltpu.VMEM((nch, HEAD_DIM + SUBLANES, HEAD_DIM), F32),
                        pltpu.VMEM((nch, 1, 1), F32)],
        compiler_params=_cparams(("parallel", "arbitrary")),
        name="mlstm")(q, k, vt, gt, q, k, vt, gt, *cast_slabs)
    return outs[0], outs[1], tuple(outs[2:])


MIX_SUB_ROWS = 256


def _mix_out_kernel(hf_ref, hb_ref, og_ref, u_ref, s_ref, x_ref, mnw_ref, sw_ref, sb_ref,
                    wout_ref, fnw_ref, rwt_ref, x1_ref, xn_ref, aff_ref):
    tm = x_ref.shape[0]
    d_a = og_ref.shape[1]
    d_b = u_ref.shape[1]
    sub = min(tm, MIX_SUB_ROWS)
    subs = [slice(i * sub, (i + 1) * sub) for i in range(tm // sub)]
    mnw = mnw_ref[...]
    sbias = sb_ref[...]

    gates = []
    for rs in subs:
        rows = []
        for cc in range(sub // CHUNK):
            r0 = rs.start + cc * CHUNK
            cols = [_dot(sw_ref[g], s_ref[r0:r0 + CHUNK, g * HEAD_DIM:(g + 1) * HEAD_DIM])
                    for g in range(d_b // HEAD_DIM)]
            rows.append(jnp.concatenate(cols, axis=1) + sbias)
        gates.append(jnp.concatenate(rows, axis=0))

    mixes = []
    for rs, gate in zip(subs, gates):
        h = hf_ref[rs, :].astype(F32) + hb_ref[rs, :].astype(F32)
        parts = [_rms(h[:, hd * HEAD_DIM:(hd + 1) * HEAD_DIM], mnw[:, hd * HEAD_DIM:(hd + 1) * HEAD_DIM])
                 for hd in range(d_a // HEAD_DIM)]
        a_out = (og_ref[rs, :].astype(F32) * jnp.concatenate(parts, axis=1)).astype(BF16)
        b_out = (u_ref[rs, :].astype(F32) * gate).astype(BF16)
        mixes.append(jnp.concatenate([a_out, b_out], axis=1))

    x1s = [x_ref[rs, :] + _dot(mix, wout_ref[...]) for rs, mix in zip(subs, mixes)]

    xns = []
    for rs, x1 in zip(subs, x1s):
        x1_ref[rs, :] = x1
        xn = _rms(x1, fnw_ref[...])
        xns.append(xn)
        chunks = xn.shape[1] // LANES
        for j in range(chunks):
            xn_ref[pl.ds(rs.start * chunks + j, sub, stride=chunks), :] = xn[:, j * LANES:(j + 1) * LANES]

    E = rwt_ref.shape[0]
    r0 = rwt_ref[...].astype(BF16)
    r1 = (rwt_ref[...] - r0.astype(F32)).astype(BF16)
    r01 = jnp.concatenate([r0, r1], axis=0)
    logits = []
    for xn in xns:
        x0 = xn.astype(BF16)
        x1 = (xn - x0.astype(F32)).astype(BF16)
        a = lax.dot_general(r01, x0, _NT, preferred_element_type=F32)
        b = lax.dot_general(r0, x1, _NT, preferred_element_type=F32)
        logits.append(a[:E] + a[E:] + b)
    for rs, lg in zip(subs, logits):
        ex = jnp.exp(lg - jnp.max(lg, axis=0, keepdims=True))
        aff = ex / jnp.sum(ex, axis=0, keepdims=True)
        for j in range(sub // LANES):
            aff_ref[rs.start // LANES + j] = aff[:, j * LANES:(j + 1) * LANES]


def _mix_out(hf, hb, og, u, s, x, mnw, sw, sbias, wout, fnw, rwt, tm):
    N, D = x.shape
    d_a = og.shape[1]
    d_b = u.shape[1]
    E = rwt.shape[0]
    nt = N // tm
    full = lambda a: pl.BlockSpec(a.shape, lambda i: (0,) * a.ndim)
    tok = lambda w: pl.BlockSpec((tm, w), lambda i: (i, 0))
    in_specs = [tok(d_a), tok(d_a), tok(d_a), tok(d_b), tok(d_b), tok(D)] + [
        full(a) for a in (mnw, sw, sbias, wout, fnw, rwt)]
    chunks = D // LANES
    out_shape = (jax.ShapeDtypeStruct((N, D), F32), jax.ShapeDtypeStruct((N * chunks, LANES), F32),
                 jax.ShapeDtypeStruct((N // LANES, E, LANES), F32))
    out_specs = (tok(D), pl.BlockSpec((tm * chunks, LANES), lambda i: (i, 0)),
                 pl.BlockSpec((tm // LANES, E, LANES), lambda i: (i, 0, 0)))
    return pl.pallas_call(
        _mix_out_kernel, out_shape=out_shape, grid=(nt,), in_specs=in_specs, out_specs=out_specs,
        compiler_params=_cparams(("parallel",)), name="mix_out")(
            hf, hb, og, u, s, x, mnw, sw, sbias, wout, fnw, rwt)


def _select_kernel(aff_ref, posm_ref, off_ref, cnt_s, wi_s, *, cap):
    nb, E, _ = aff_ref.shape
    aff = aff_ref[...]

    def count_ge(cand):
        c = jnp.sum((aff >= cand).astype(I32), axis=0, keepdims=True)
        return jnp.sum(c, axis=2, keepdims=True)

    def bit_step(i, thr_bits):
        cand = thr_bits | jnp.left_shift(jnp.int32(1), 30 - i)
        return jnp.where(count_ge(pltpu.bitcast(cand, F32)) >= cap, cand, thr_bits)

    thr = pltpu.bitcast(lax.fori_loop(0, 31, bit_step, jnp.zeros((1, E, 1), I32)), F32)
    gt = aff > thr
    eq = aff == thr
    n_gt = jnp.sum(jnp.sum(gt.astype(I32), axis=0, keepdims=True), axis=2, keepdims=True)
    need = cap - n_gt

    li = lax.broadcasted_iota(I32, (LANES, LANES), 0)
    lj = lax.broadcasted_iota(I32, (LANES, LANES), 1)
    upper = (li < lj).astype(BF16)

    def excl_cumsum(flag):
        fb = flag.astype(BF16).reshape(nb * E, LANES)
        wi_s[...] = _dot(fb, upper).astype(I32).reshape(nb, E, LANES)
        cnt_s[...] = jnp.sum(flag.astype(I32), axis=2, keepdims=True)

        def blk(b, run):
            wi_s[b] = wi_s[b] + run
            return run + cnt_s[b]

        lax.fori_loop(0, nb, blk, jnp.zeros((E, 1), I32))
        return wi_s[...]

    eq_rank = excl_cumsum(eq)
    sel = gt | (eq & (eq_rank < need))
    pos = excl_cumsum(sel)
    posm_ref[...] = jnp.where(sel, pos, -1)
    off_ref[...] = jnp.broadcast_to(pos[:, :, 0:1], off_ref.shape)


def _select(aff3, cap):
    nb, E, _ = aff3.shape
    return pl.pallas_call(
        functools.partial(_select_kernel, cap=cap),
        out_shape=(jax.ShapeDtypeStruct((nb, E, LANES), I32), jax.ShapeDtypeStruct((nb, E, LANES), I32)),
        scratch_shapes=[pltpu.VMEM((nb, E, 1), I32), pltpu.VMEM((nb, E, LANES), I32)],
        compiler_params=_cparams(None), name="select")(aff3)


def _compact_kernel(off_sm, posm_ref, aff_ref, acc_ref):
    nb, E, _ = posm_ref.shape
    acc_ref[...] = jnp.zeros_like(acc_ref)
    srow = lax.broadcasted_iota(I32, (2 * LANES, LANES), 0)
    r8 = lax.broadcasted_iota(I32, (SUBLANES, LANES), 0)
    lane8 = lax.broadcasted_iota(I32, (SUBLANES, LANES), 1)

    def blk(b, carry):
        tok = b * LANES + lane8
        t_hi = jnp.right_shift(tok, 8).astype(F32)
        t_lo = jnp.bitwise_and(tok, 255).astype(F32)
        pm = posm_ref[b]
        af = aff_ref[b]
        for e in range(E):
            off = off_sm[b * E + e]
            j0 = jnp.right_shift(off, 7)
            rel = pm[e:e + 1, :] - j0 * LANES
            onehot = (srow == rel).astype(BF16)
            a = af[e:e + 1, :]
            a0 = a.astype(BF16)
            r1 = a - a0.astype(F32)
            a1 = r1.astype(BF16)
            a2 = (r1 - a1.astype(F32)).astype(BF16)
            lhs = jnp.where(r8 == 0, t_hi, jnp.where(r8 == 1, t_lo, 0.0))
            lhs = jnp.where(r8 == 2, a0.astype(F32), lhs)
            lhs = jnp.where(r8 == 3, a1.astype(F32), lhs)
            lhs = jnp.where(r8 == 4, a2.astype(F32), lhs).astype(BF16)
            out = lax.dot_general(lhs, onehot, _NT, preferred_element_type=F32)
            acc_ref[e, j0] = acc_ref[e, j0] + out[:, :LANES]
            acc_ref[e, j0 + 1] = acc_ref[e, j0 + 1] + out[:, LANES:]
        return carry

    lax.fori_loop(0, nb, blk, 0)


def _compact(off_flat, posm3, aff3, cap):
    nb, E, _ = posm3.shape
    nt_pad = cap // LANES + 2
    gs = pltpu.PrefetchScalarGridSpec(
        num_scalar_prefetch=1, grid=(1,),
        in_specs=[pl.BlockSpec(posm3.shape, lambda i, o: (0, 0, 0)),
                  pl.BlockSpec(aff3.shape, lambda i, o: (0, 0, 0))],
        out_specs=pl.BlockSpec((E, nt_pad, SUBLANES, LANES), lambda i, o: (0, 0, 0, 0)))
    return pl.pallas_call(
        _compact_kernel, out_shape=jax.ShapeDtypeStruct((E, nt_pad, SUBLANES, LANES), F32),
        grid_spec=gs, compiler_params=_cparams(("arbitrary",)), name="compact")(off_flat, posm3, aff3)


def _ffn_kernel(idc_sm, idn_sm, xn_hbm, cacc_ref, wg_ref, wu_ref, wd_ref, y_ref, xbuf, sem,
                *, n_pairs, fc):
    g = pl.program_id(0)
    ts = y_ref.shape[0] // 2
    chunks = xbuf.shape[1] // ts
    D = wd_ref.shape[2]
    F = wg_ref.shape[2]
    nchunk = F // fc
    rows_per_chunk = ts // nchunk
    scale_rows = min(ts, LANES)

    def row_copy(idx_sm, i, r, dst_slot):
        src = pl.multiple_of(idx_sm[0, 0, i] * chunks, chunks)
        return pltpu.make_async_copy(xn_hbm.at[pl.ds(src, chunks), :],
                                     xbuf.at[dst_slot, pl.ds(r * chunks, chunks), :], sem.at[dst_slot])

    def wait_tile(slot):
        pltpu.make_async_copy(xn_hbm.at[pl.ds(0, ts * chunks), :], xbuf.at[slot], sem.at[slot]).wait()

    def run_tile(slot, gathers):
        xs = xbuf.at[slot]
        x = jnp.concatenate([xs[pl.ds(j, ts, stride=chunks), :] for j in range(chunks)],
                            axis=1).astype(BF16)
        acc = jnp.zeros((ts, D), F32)
        for c in range(nchunk):
            for idx_sm, base, dst in gathers:
                for r in range(c * rows_per_chunk, (c + 1) * rows_per_chunk):
                    row_copy(idx_sm, base + r, r, dst).start(priority=r % 2)
            cs = slice(c * fc, (c + 1) * fc)
            gte = _dot(x, wg_ref[0, :, cs])
            up = _dot(x, wu_ref[0, :, cs])
            hid = (gte * _sigmoid(gte) * up).astype(BF16)
            acc = acc + _dot(hid, wd_ref[0, cs, :])
        for k in range(ts // scale_rows):
            r0 = slot * ts + k * scale_rows
            t = cacc_ref[0, r0 // LANES]
            vrow = t[2:3, :] + t[3:4, :] + t[4:5, :]
            vmat = jnp.broadcast_to(vrow, (LANES, LANES)).T[r0 % LANES:r0 % LANES + scale_rows]
            vfull = jnp.concatenate([vmat] * (D // LANES), axis=1)
            y_ref[r0:r0 + scale_rows, :] = (
                acc[k * scale_rows:(k + 1) * scale_rows, :] * vfull).astype(y_ref.dtype)

    @pl.when(g == 0)
    def _():
        def body(r, carry):
            row_copy(idc_sm, r, r, 0).start()
            return carry
        lax.fori_loop(0, ts, body, 0, unroll=8)

    @pl.when(g < n_pairs)
    def _():
        wait_tile(0)
        run_tile(0, [(idc_sm, ts, 1), (idn_sm, 0, 0)])
        wait_tile(1)
        run_tile(1, [])

    @pl.when(g >= n_pairs)
    def _():
        wait_tile(0)
        y_ref[...] = jnp.zeros_like(y_ref)


def _ffn(idx, xn, cacc, wg, wu, wd, cap, ts):
    E, D, F = wg.shape
    pairs_per_expert = cap // (2 * ts)
    n_pairs = E * pairs_per_expert
    last = n_pairs - 1
    idx3 = idx.reshape(n_pairs, 1, 2 * ts)
    eidx = lambda g: jnp.minimum(g // pairs_per_expert, E - 1)
    in_specs = [
        pl.BlockSpec((1, 1, 2 * ts), lambda g: (jnp.minimum(g, last), 0, 0), memory_space=pltpu.SMEM),
        pl.BlockSpec((1, 1, 2 * ts), lambda g: (jnp.minimum(g + 1, last), 0, 0), memory_space=pltpu.SMEM),
        pl.BlockSpec(memory_space=pl.ANY),
        pl.BlockSpec((1, 2 * ts // LANES, SUBLANES, LANES),
                     lambda g: (eidx(g), lax.rem(jnp.minimum(g, last), pairs_per_expert), 0, 0)),
        pl.BlockSpec((1, D, F), lambda g: (eidx(g), 0, 0)),
        pl.BlockSpec((1, D, F), lambda g: (eidx(g), 0, 0)),
        pl.BlockSpec((1, F, D), lambda g: (eidx(g), 0, 0)),
    ]
    return pl.pallas_call(
        functools.partial(_ffn_kernel, n_pairs=n_pairs, fc=min(512, F)),
        out_shape=jax.ShapeDtypeStruct(((n_pairs + 1) * 2 * ts, D), BF16),
        grid=(n_pairs + 1,), in_specs=in_specs,
        out_specs=pl.BlockSpec((2 * ts, D), lambda g: (g, 0)),
        scratch_shapes=[pltpu.VMEM((2, ts * (D // LANES), LANES), F32), pltpu.SemaphoreType.DMA((2,))],
        compiler_params=_cparams(("arbitrary",)), name="ffn")(idx3, idx3, xn, cacc, wg, wu, wd)


WIN = 64


def _combine_kernel(off_sm, posm_ref, x1_ref, y_hbm, fnw_ref, o_ref, ycat, yext, sem, sem_ext,
                    *, cap, y_rows, nblk):
    b = pl.program_id(0)
    slot = lax.rem(b, 2)
    nsub, E, _ = posm_ref.shape
    tb = x1_ref.shape[0]
    pm = jnp.concatenate([posm_ref[j] for j in range(nsub)], axis=1)
    wrow = lax.broadcasted_iota(I32, (WIN, tb), 0)

    def starts_of(blk, r):
        out = []
        for e in range(E):
            base = jnp.left_shift(jnp.right_shift(off_sm[blk * E + e], 4), 4)
            st = jnp.minimum(e * cap + base + r * WIN, y_rows - WIN)
            out.append(pl.multiple_of(st, BF16_ROWS))
        return out

    def copies(starts, dst, dsem):
        return [pltpu.make_async_copy(y_hbm.at[pl.ds(starts[e], WIN), :], dst.at[pl.ds(e * WIN, WIN), :], dsem)
                for e in range(E)]

    def onehot(starts):
        ps = []
        for e in range(E):
            pe = pm[e:e + 1, :]
            rel = jnp.where(pe >= 0, pe + (e * cap - starts[e]), -1)
            ps.append((wrow == rel).astype(BF16))
        return jnp.concatenate(ps, axis=0)

    @pl.when(b == 0)
    def _():
        for cp in copies(starts_of(b, 0), ycat.at[0], sem.at[0]):
            cp.start()

    @pl.when(b + 1 < nblk)
    def _():
        for cp in copies(starts_of(b + 1, 0), ycat.at[1 - slot], sem.at[1 - slot]):
            cp.start()

    starts0 = starts_of(b, 0)
    p0 = onehot(starts0)
    for cp in copies(starts0, ycat.at[slot], sem.at[slot]):
        cp.wait()
    acc0 = lax.dot_general(p0, ycat[slot], _TN, preferred_element_type=F32)

    nrounds = jnp.int32(1)
    for e in range(E):
        base = jnp.left_shift(jnp.right_shift(off_sm[b * E + e], 4), 4)
        nrounds = jnp.maximum(nrounds, jnp.right_shift(off_sm[(b + 1) * E + e] - base + (WIN - 1), 6))

    def round_body(r, acc):
        starts = starts_of(b, r)
        cps = copies(starts, yext, sem_ext)
        for cp in cps:
            cp.start()
        p = onehot(starts)
        for cp in cps:
            cp.wait()
        return acc + lax.dot_general(p, yext[...], _TN, preferred_element_type=F32)

    acc = lax.fori_loop(1, nrounds, round_body, acc0)
    o_ref[...] = _rms(x1_ref[...] + acc, fnw_ref[...])


def _combine(off_flat, posm3, x1, y, fnw, cap, tb):
    N, D = x1.shape
    nb, E, _ = posm3.shape
    nsub = tb // LANES
    y_rows = y.shape[0]
    gs = pltpu.PrefetchScalarGridSpec(
        num_scalar_prefetch=1, grid=(N // tb,),
        in_specs=[pl.BlockSpec((nsub, E, LANES), lambda i, o: (i, 0, 0)),
                  pl.BlockSpec((tb, D), lambda i, o: (i, 0)),
                  pl.BlockSpec(memory_space=pl.ANY),
                  pl.BlockSpec(fnw.shape, lambda i, o: (0, 0))],
        out_specs=pl.BlockSpec((tb, D), lambda i, o: (i, 0)),
        scratch_shapes=[pltpu.VMEM((2, E * WIN, D), BF16), pltpu.VMEM((E * WIN, D), BF16),
                        pltpu.SemaphoreType.DMA((2,)), pltpu.SemaphoreType.DMA])
    return pl.pallas_call(
        functools.partial(_combine_kernel, cap=cap, y_rows=y_rows, nblk=N // tb),
        out_shape=jax.ShapeDtypeStruct((N, D), F32), grid_spec=gs,
        compiler_params=_cparams(("arbitrary",)), name="combine")(off_flat, posm3, x1, y, fnw)


def _prep_params(norm_mix_w, w_in, conv_w, conv_b, gate_b, mlstm_norm_w, sgu_norm_w, sgu_w, sgu_b,
                 w_out, norm_ffn_w, router_w, w_gate, w_up, w_down, norm_final_w):
    d_a = mlstm_norm_w.shape[1]
    d_b = sgu_norm_w.shape[1]
    ng = gate_b.shape[1]
    w = w_in[0]
    o0, o1, o2, o3, o4, o5 = 2 * d_a, 3 * d_a, 4 * d_a, 4 * d_a + ng, 4 * d_a + ng + d_b, 4 * d_a + ng + 2 * d_b
    wg = w[:, o2:o3]
    return dict(
        nw=norm_mix_w[0][None, :],
        wqk=w[:, :o0].astype(BF16), wvt=w[:, o0:o1].T.astype(BF16), wo=w[:, o1:o2].astype(BF16),
        wgt=wg.T.astype(BF16),
        wu=w[:, o3:o4].astype(BF16), ws=w[:, o4:o5].astype(BF16),
        cw=conv_w[0], cb=conv_b[0][None, :], gbt=gate_b[0][:, None],
        snw=sgu_norm_w[0][None, :], mnw=mlstm_norm_w[0][None, :],
        sw=sgu_w[0].astype(BF16),
        sbias=jnp.repeat(sgu_b[0].T, HEAD_DIM, axis=1),
        wout=w_out[0].astype(BF16), fnw=norm_ffn_w[0][None, :], rwt=router_w[0].T,
        nfw=norm_final_w[None, :],
    )


def _mixer(x, p, expert_weights):
    B, T, D = x.shape
    q, k, vt, og, gt, u, s = _in_proj(
        x, p["nw"], p["wqk"], p["wvt"], p["wo"], p["wgt"], p["wu"], p["ws"],
        p["cw"], p["cb"], p["gbt"], p["snw"], min(1024, T))
    n_sub = MLSTM_CHUNKS_PER_STEP if (T // CHUNK) % MLSTM_CHUNKS_PER_STEP == 0 else 1
    nsteps = B * (T // (CHUNK * n_sub))
    slabs = [_cast_slabs(w, nsteps) for w in expert_weights]
    hf, hb, cast = _mlstm(q, k, vt, gt, n_sub, tuple(sl for sl in slabs if sl is not None))
    cast = list(cast)
    weights_bf16 = [w.astype(BF16) if sl is None else cast.pop(0).reshape(w.shape)
                    for w, sl in zip(expert_weights, slabs)]
    return (hf, hb, og, u, s), weights_bf16


def _moe_tail(x, mixed, p, wgate, wup, wdown):
    B, T, D = x.shape
    N = B * T
    E = N_EXPERTS
    cap = (N * CAPACITY_FACTOR) // E
    hf, hb, og, u, s = mixed
    flat = lambda a: a.reshape(N, a.shape[-1])
    x1, xn, aff3 = _mix_out(flat(hf), flat(hb), flat(og), flat(u), flat(s), flat(x),
                            p["mnw"], p["sw"], p["sbias"], p["wout"], p["fnw"], p["rwt"], min(1024, N))
    posm3, off3 = _select(aff3, cap)
    nb = N // LANES
    off_flat = off3[:, :, 0].reshape(nb * E)
    cacc = _compact(off_flat, posm3, aff3, cap)
    nt = cap // LANES
    idx = (cacc[:, :nt, 0, :] * 256.0 + cacc[:, :nt, 1, :]).astype(I32).reshape(E * cap)
    ts = min(512, cap // 2)
    y = _ffn(idx, xn, cacc, wgate, wup, wdown, cap, ts)
    tb = min(256, N)
    sub = tb // LANES
    off_tb = jnp.concatenate([off3[::sub, :, 0], jnp.full((1, E), cap, I32)], axis=0).reshape(-1)
    out = _combine(off_tb, posm3, x1, y, p["nfw"], cap, tb)
    return out.reshape(B, T, D)


def kernel(x_prompt, x_sample, norm_mix_w, w_in, conv_w, conv_b, gate_b, mlstm_norm_w, sgu_norm_w,
           sgu_w, sgu_b, w_out, norm_ffn_w, router_w, w_gate, w_up, w_down, norm_final_w):
    p = _prep_params(norm_mix_w, w_in, conv_w, conv_b, gate_b, mlstm_norm_w, sgu_norm_w, sgu_w,
                     sgu_b, w_out, norm_ffn_w, router_w, w_gate, w_up, w_down, norm_final_w)
    mixed_p, (wgate,) = _mixer(x_prompt, p, [w_gate[0]])
    mixed_s, (wup, wdown) = _mixer(x_sample, p, [w_up[0], w_down[0]])
    return (_moe_tail(x_prompt, mixed_p, p, wgate, wup, wdown),
            _moe_tail(x_sample, mixed_s, p, wgate, wup, wdown))
```

```python
import functools
import math

import jax
import jax.numpy as jnp
from jax import lax
from jax.experimental import pallas as pl
from jax.experimental.pallas import tpu as pltpu

F32 = jnp.float32
BF16 = jnp.bfloat16
I32 = jnp.int32

EPS = 1e-6
N_HEADS = 4
HEAD_DIM = 128
CHUNK = 128
N_EXPERTS = 16
CAPACITY_FACTOR = 2
LANES = 128
SUBLANES = 8
BF16_ROWS = 16
NEG_BIG = -1e30
VMEM_LIMIT = 48 * 1024 * 1024

_NT = (((1,), (1,)), ((), ()))
_TN = (((0,), (0,)), ((), ()))


def _cparams(sem, vmem=VMEM_LIMIT):
    return pltpu.CompilerParams(dimension_semantics=sem, vmem_limit_bytes=vmem)


def _dot(a, b):
    return jnp.dot(a, b, preferred_element_type=F32)


def _sigmoid(x):
    return 1.0 / (1.0 + jnp.exp(-x))


def _gelu(x):
    return 0.5 * x * (1.0 + lax.erf(x * (1.0 / math.sqrt(2.0))))


def _rms(x, w):
    ms = jnp.mean(x * x, axis=-1, keepdims=True)
    return x * lax.rsqrt(ms + EPS) * w


PROJ_COLS = 256


def _in_proj_kernel(x_ref, xp_ref, xn_ref, nw_ref, wqk_ref, wvt_ref, wo_ref, wgt_ref,
                    wu_ref, ws_ref, cw_ref, cb_ref, gbt_ref, snw_ref,
                    q_ref, k_ref, vt_ref, og_ref, gt_ref, u_ref, s_ref):
    i = pl.program_id(1)
    n_i = pl.num_programs(1)
    tm = x_ref.shape[1]
    d_a = q_ref.shape[2]
    nw = nw_ref[...]
    hb = _rms(x_ref[0], nw).astype(BF16)
    hp = jnp.where(i == 0, 0.0, _rms(xp_ref[0], nw)).astype(BF16)
    hn = jnp.where(i == n_i - 1, 0.0, _rms(xn_ref[0], nw)).astype(BF16)
    h_ext = jnp.concatenate([hb, hp, hn], axis=0)
    cw = cw_ref[...]
    cb = cb_ref[...]
    snw = snw_ref[...]
    row = lax.broadcasted_iota(I32, (tm, PROJ_COLS), 0)

    def tile_cols(j):
        return slice(j * PROJ_COLS, (j + 1) * PROJ_COLS)

    def qk_mm(j):
        return _dot(h_ext, wqk_ref[:, tile_cols(j)])

    def qk_ep(j, ze):
        cs = tile_cols(j)
        z = ze[:tm]
        zp = ze[tm + SUBLANES - 1:tm + SUBLANES]
        zn = ze[tm + SUBLANES:tm + SUBLANES + 1]
        z_prev = jnp.where(row == 0, zp, pltpu.roll(z, 1, axis=0))
        z_next = jnp.where(row == tm - 1, zn, pltpu.roll(z, tm - 1, axis=0))
        conv = cb[:, cs] + cw[0:1, cs] * z_prev + cw[1:2, cs] * z + cw[2:3, cs] * z_next
        qk = conv * _sigmoid(conv)
        if cs.start < d_a:
            q_ref[0, :, cs] = qk.astype(BF16)
        else:
            ks = slice(cs.start - d_a, cs.stop - d_a)
            k_ref[0, :, ks] = (qk * (1.0 / math.sqrt(HEAD_DIM))).astype(BF16)

    def u_mm(j):
        return _dot(hb, wu_ref[:, tile_cols(j)])

    def u_ep(j, r):
        u_ref[0, :, tile_cols(j)] = _gelu(r).astype(BF16)

    def s_mm(j):
        return _dot(hb, ws_ref[:, tile_cols(j)])

    def s_ep(j, r):
        cs = tile_cols(j)
        sv = _gelu(r)
        for g in range(PROJ_COLS // HEAD_DIM):
            gs = slice(g * HEAD_DIM, (g + 1) * HEAD_DIM)
            og_cols = slice(cs.start + gs.start, cs.start + gs.stop)
            s_ref[0, :, og_cols] = _rms(sv[:, gs], snw[:, og_cols]).astype(BF16)

    def o_mm(j):
        return _dot(hb, wo_ref[:, tile_cols(j)])

    def o_ep(j, r):
        og_ref[0, :, tile_cols(j)] = _sigmoid(r).astype(BF16)

    def vt_mm(j):
        return lax.dot_general(wvt_ref[tile_cols(j), :], hb, _NT, preferred_element_type=F32)

    def vt_ep(j, r):
        vt_ref[0, tile_cols(j), :] = r.astype(BF16)

    n_qk = wqk_ref.shape[1] // PROJ_COLS
    n_b = wu_ref.shape[1] // PROJ_COLS
    n_a = wo_ref.shape[1] // PROJ_COLS
    light = [(u_mm, u_ep, j) for j in range(n_b)] + [(s_mm, s_ep, j) for j in range(n_b)]
    light = [light[(k // 2) + (k % 2) * n_b] for k in range(2 * n_b)]
    tiles = []
    for j in range(n_qk):
        tiles.append((qk_mm, qk_ep, j))
        if j < len(light):
            tiles.append(light[j])
    tiles += light[n_qk:]
    for j in range(n_a):
        tiles += [(o_mm, o_ep, j), (vt_mm, vt_ep, j)]
    pending = None
    for mm, ep, j in tiles:
        res = mm(j)
        if pending is not None:
            pending[0](pending[1], pending[2])
        pending = (ep, j, res)
    pending[0](pending[1], pending[2])

    zgt = lax.dot_general(wgt_ref[...], hb, _NT, preferred_element_type=F32) + gbt_ref[...]
    rowt = lax.broadcasted_iota(I32, zgt.shape, 0)
    gt_ref[0] = jnp.where(rowt < 2 * N_HEADS, zgt, jax.nn.log_sigmoid(zgt))


def _in_proj(x, nw, wqk, wvt, wo, wgt, wu, ws, cw, cb, gbt, snw, tm):
    B, T, D = x.shape
    d_a = wvt.shape[0]
    d_b = wu.shape[1]
    ng = wgt.shape[0]
    nt = T // tm
    hb8 = tm // SUBLANES
    last8 = T // SUBLANES - 1
    full = lambda a: pl.BlockSpec(a.shape, lambda b, i: (0,) * a.ndim)
    tok = lambda w: pl.BlockSpec((1, tm, w), lambda b, i: (b, i, 0))
    in_specs = [
        pl.BlockSpec((1, tm, D), lambda b, i: (b, i, 0)),
        pl.BlockSpec((1, SUBLANES, D), lambda b, i: (b, jnp.maximum(i * hb8 - 1, 0), 0)),
        pl.BlockSpec((1, SUBLANES, D), lambda b, i: (b, jnp.minimum((i + 1) * hb8, last8), 0)),
    ] + [full(a) for a in (nw, wqk, wvt, wo, wgt, wu, ws, cw, cb, gbt, snw)]
    out_shape = (
        jax.ShapeDtypeStruct((B, T, d_a), BF16), jax.ShapeDtypeStruct((B, T, d_a), BF16),
        jax.ShapeDtypeStruct((B, d_a, T), BF16), jax.ShapeDtypeStruct((B, T, d_a), BF16),
        jax.ShapeDtypeStruct((B, ng, T), F32),
        jax.ShapeDtypeStruct((B, T, d_b), BF16), jax.ShapeDtypeStruct((B, T, d_b), BF16),
    )
    out_specs = (tok(d_a), tok(d_a), pl.BlockSpec((1, d_a, tm), lambda b, i: (b, 0, i)), tok(d_a),
                 pl.BlockSpec((1, ng, tm), lambda b, i: (b, 0, i)), tok(d_b), tok(d_b))
    return pl.pallas_call(
        _in_proj_kernel, out_shape=out_shape, grid=(B, nt), in_specs=in_specs,
        out_specs=out_specs, compiler_params=_cparams(("parallel", "arbitrary")),
        name="in_proj")(x, x, x, nw, wqk, wvt, wo, wgt, wu, ws, cw, cb, gbt, snw)


def _split3(x):
    x0 = x.astype(BF16)
    r1 = x - x0.astype(F32)
    x1 = r1.astype(BF16)
    x2 = (r1 - x1.astype(F32)).astype(BF16)
    return x0, x1, x2


def _mlstm_kernel(*refs, n_cast):
    qf_ref, kf_ref, vtf_ref, gtf_ref, qb_ref, kb_ref, vtb_ref, gtb_ref = refs[:8]
    cast_in = refs[8:8 + n_cast]
    hf_ref, hb_ref = refs[8 + n_cast:10 + n_cast]
    cast_out = refs[10 + n_cast:10 + 2 * n_cast]
    c_st, m_st = refs[10 + 2 * n_cast:]
    for w_ref, o_ref in zip(cast_in, cast_out):
        o_ref[...] = w_ref[...].astype(BF16)

    c = pl.program_id(1)
    L = CHUNK
    n_sub = qf_ref.shape[1] // L
    d = HEAD_DIM

    @pl.when(c == 0)
    def _():
        c_st[...] = jnp.zeros_like(c_st)
        m_st[...] = jnp.zeros_like(m_st)

    r0 = lax.broadcasted_iota(I32, (L, L), 0)
    r1 = lax.broadcasted_iota(I32, (L, L), 1)
    ones8 = jnp.ones((SUBLANES, L), BF16)
    ng = 2 * N_HEADS
    dirs = ((qf_ref, kf_ref, vtf_ref, gtf_ref, r0 <= r1, hf_ref, lambda i: i),
            (qb_ref, kb_ref, vtb_ref, gtb_ref, r0 >= r1, hb_ref, lambda i: n_sub - 1 - i))

    subs = []
    for i in range(n_sub):
        chains = []
        for dr, (q_ref, k_ref, vt_ref, gt_ref, mask_st, h_ref, order) in enumerate(dirs):
            rows = slice(order(i) * L, (order(i) + 1) * L)
            gt = gt_ref[0, :, rows]
            g3 = jnp.concatenate(_split3(gt), axis=0)
            b3 = _dot(g3, mask_st.astype(BF16))
            nr = gt.shape[0]
            br_all = b3[:nr] + b3[nr:2 * nr] + b3[2 * nr:]
            a_rows = gt[:ng] - br_all[ng:]
            a_cols = jnp.concatenate([a_rows, jnp.zeros((L - ng, L), F32)], axis=0).T
            for hd in range(N_HEADS):
                j = dr * N_HEADS + hd
                hs = slice(hd * d, (hd + 1) * d)
                qb = q_ref[0, rows, hs]
                kb = k_ref[0, rows, hs]
                b_row = br_all[ng + j:ng + j + 1, :]
                chains.append(dict(
                    j=j, hs=hs, rows=rows, h_ref=h_ref, qb=qb, kb=kb,
                    vt_aug=jnp.concatenate([vt_ref[0, hs, rows], ones8], axis=0),
                    dmat=jnp.where(mask_st, a_cols[:, j:j + 1] + b_row, NEG_BIG),
                    i_row=gt[j:j + 1, :], b_row=b_row,
                    b_tot=jnp.sum(gt[ng + j:ng + j + 1, :], axis=1, keepdims=True),
                    s_raw=lax.dot_general(kb, qb, _NT, preferred_element_type=F32)))
        subs.append(chains)

    state = [(c_st[j], m_st[j]) for j in range(2 * N_HEADS)]
    for chains in subs:
        for ch in chains:
            ch["caug"], ch["m_prev"] = state[ch["j"]]
            ch["ia"] = lax.dot_general(ch["caug"].astype(BF16), ch["qb"], _NT, preferred_element_type=F32)
        for ch in chains:
            g_row = ch["b_tot"] - ch["b_row"] + ch["i_row"]
            m_new = jnp.maximum(ch["b_tot"] + ch["m_prev"], jnp.max(g_row, axis=1, keepdims=True))
            wk = jnp.exp(g_row - m_new)
            decay = jnp.exp(ch["b_tot"] + ch["m_prev"] - m_new)
            vw = (ch["vt_aug"].astype(F32) * wk).astype(BF16)
            state[ch["j"]] = (decay * ch["caug"] + _dot(vw, ch["kb"]), m_new)
        for ch in chains:
            inter = ch["b_row"] + ch["m_prev"]
            m_t = jnp.maximum(jnp.max(ch["dmat"], axis=0, keepdims=True), inter)
            st = ch["s_raw"] * jnp.exp(ch["dmat"] - m_t)
            w_inter = jnp.exp(inter - m_t)
            den = jnp.sum(st, axis=0, keepdims=True) + w_inter * ch["ia"][d:d + 1]
            ch["st"] = st.astype(BF16)
            ch["w_inter"] = w_inter
            ch["rden"] = 1.0 / jnp.maximum(jnp.abs(den), jnp.exp(-m_t))
        for ch in chains:
            num = _dot(ch["vt_aug"][:d], ch["st"]) + ch["w_inter"] * ch["ia"][:d]
            ch["h_ref"][0, ch["rows"], ch["hs"]] = (num * ch["rden"]).T.astype(ch["h_ref"].dtype)
    for j, (caug, m) in enumerate(state):
        c_st[j] = caug
        m_st[j] = m


CAST_SLAB_BYTES = 4 * 1024 * 1024
MLSTM_CHUNKS_PER_STEP = 4


def _cast_slabs(w, nsteps):
    E, R, C = w.shape
    if (E * R) % nsteps:
        return None
    rows = (E * R) // nsteps
    if rows % BF16_ROWS or R % rows or rows * C * 4 > CAST_SLAB_BYTES:
        return None
    return w.reshape(nsteps, rows, C)


def _mlstm(q, k, vt, gt, n_sub, cast_slabs=()):
    B, T, d_a = q.shape
    L = CHUNK * n_sub
    nc = T // L
    ng = gt.shape[1]
    fwd = lambda w: pl.BlockSpec((1, L, w), lambda b, c: (b, c, 0))
    bwd = lambda w: pl.BlockSpec((1, L, w), lambda b, c: (b, nc - 1 - c, 0))
    fwd_t = lambda r: pl.BlockSpec((1, r, L), lambda b, c: (b, 0, c))
    bwd_t = lambda r: pl.BlockSpec((1, r, L), lambda b, c: (b, 0, nc - 1 - c))
    slab = lambda a: pl.BlockSpec((1,) + a.shape[1:], lambda b, c: (b * nc + c, 0, 0))
    in_specs = [fwd(d_a), fwd(d_a), fwd_t(d_a), fwd_t(ng), bwd(d_a), bwd(d_a), bwd_t(d_a), bwd_t(ng)]
    in_specs += [slab(a) for a in cast_slabs]
    out_shape = [jax.ShapeDtypeStruct((B, T, d_a), BF16), jax.ShapeDtypeStruct((B, T, d_a), BF16)]
    out_shape += [jax.ShapeDtypeStruct(a.shape, BF16) for a in cast_slabs]
    out_specs = [fwd(d_a), bwd(d_a)] + [slab(a) for a in cast_slabs]
    nch = 2 * N_HEADS
    outs = pl.pallas_call(
        functools.partial(_mlstm_kernel, n_cast=len(cast_slabs)),
        out_shape=tuple(out_shape), grid=(B, nc), in_specs=in_specs, out_specs=tuple(out_specs),
        scratch_shapes=[pltpu.VMEM((nch, HEAD_DIM + SUBLANES, HEAD_DIM), F32),
                        pltpu.VMEM((nch, 1, 1), F32)],
        compiler_params=_cparams(("parallel", "arbitrary")),
        name="mlstm")(q, k, vt, gt, q, k, vt, gt, *cast_slabs)
    return outs[0], outs[1], tuple(outs[2:])


MIX_SUB_ROWS = 256


def _mix_out_kernel(hf_ref, hb_ref, og_ref, u_ref, s_ref, x_ref, mnw_ref, sw_ref, sb_ref,
                    wout_ref, fnw_ref, rwt_ref, x1_ref, xn_ref, aff_ref):
    tm = x_ref.shape[0]
    d_a = og_ref.shape[1]
    d_b = u_ref.shape[1]
    sub = min(tm, MIX_SUB_ROWS)
    subs = [slice(i * sub, (i + 1) * sub) for i in range(tm // sub)]
    mnw = mnw_ref[...]
    sbias = sb_ref[...]

    gates = []
    for rs in subs:
        rows = []
        for cc in range(sub // CHUNK):
            r0 = rs.start + cc * CHUNK
            cols = [_dot(sw_ref[g], s_ref[r0:r0 + CHUNK, g * HEAD_DIM:(g + 1) * HEAD_DIM])
                    for g in range(d_b // HEAD_DIM)]
            rows.append(jnp.concatenate(cols, axis=1) + sbias)
        gates.append(jnp.concatenate(rows, axis=0))

    mixes = []
    for rs, gate in zip(subs, gates):
        h = hf_ref[rs, :].astype(F32) + hb_ref[rs, :].astype(F32)
        parts = [_rms(h[:, hd * HEAD_DIM:(hd + 1) * HEAD_DIM], mnw[:, hd * HEAD_DIM:(hd + 1) * HEAD_DIM])
                 for hd in range(d_a // HEAD_DIM)]
        a_out = (og_ref[rs, :].astype(F32) * jnp.concatenate(parts, axis=1)).astype(BF16)
        b_out = (u_ref[rs, :].astype(F32) * gate).astype(BF16)
        mixes.append(jnp.concatenate([a_out, b_out], axis=1))

    x1s = [x_ref[rs, :] + _dot(mix, wout_ref[...]) for rs, mix in zip(subs, mixes)]

    xns = []
    for rs, x1 in zip(subs, x1s):
        x1_ref[rs, :] = x1
        xn = _rms(x1, fnw_ref[...])
        xns.append(xn)
        chunks = xn.shape[1] // LANES
        for j in range(chunks):
            xn_ref[pl.ds(rs.start * chunks + j, sub, stride=chunks), :] = xn[:, j * LANES:(j + 1) * LANES]

    E = rwt_ref.shape[0]
    r0 = rwt_ref[...].astype(BF16)
    r1 = (rwt_ref[...] - r0.astype(F32)).astype(BF16)
    r01 = jnp.concatenate([r0, r1], axis=0)
    logits = []
    for xn in xns:
        x0 = xn.astype(BF16)
        x1 = (xn - x0.astype(F32)).astype(BF16)
        a = lax.dot_general(r01, x0, _NT, preferred_element_type=F32)
        b = lax.dot_general(r0, x1, _NT, preferred_element_type=F32)
        logits.append(a[:E] + a[E:] + b)
    for rs, lg in zip(subs, logits):
        ex = jnp.exp(lg - jnp.max(lg, axis=0, keepdims=True))
        aff = ex / jnp.sum(ex, axis=0, keepdims=True)
        for j in range(sub // LANES):
            aff_ref[rs.start // LANES + j] = aff[:, j * LANES:(j + 1) * LANES]


def _mix_out(hf, hb, og, u, s, x, mnw, sw, sbias, wout, fnw, rwt, tm):
    N, D = x.shape
    d_a = og.shape[1]
    d_b = u.shape[1]
    E = rwt.shape[0]
    nt = N // tm
    full = lambda a: pl.BlockSpec(a.shape, lambda i: (0,) * a.ndim)
    tok = lambda w: pl.BlockSpec((tm, w), lambda i: (i, 0))
    in_specs = [tok(d_a), tok(d_a), tok(d_a), tok(d_b), tok(d_b), tok(D)] + [
        full(a) for a in (mnw, sw, sbias, wout, fnw, rwt)]
    chunks = D // LANES
    out_shape = (jax.ShapeDtypeStruct((N, D), F32), jax.ShapeDtypeStruct((N * chunks, LANES), F32),
                 jax.ShapeDtypeStruct((N // LANES, E, LANES), F32))
    out_specs = (tok(D), pl.BlockSpec((tm * chunks, LANES), lambda i: (i, 0)),
                 pl.BlockSpec((tm // LANES, E, LANES), lambda i: (i, 0, 0)))
    return pl.pallas_call(
        _mix_out_kernel, out_shape=out_shape, grid=(nt,), in_specs=in_specs, out_specs=out_specs,
        compiler_params=_cparams(("parallel",)), name="mix_out")(
            hf, hb, og, u, s, x, mnw, sw, sbias, wout, fnw, rwt)


def _select_kernel(aff_ref, posm_ref, off_ref, cnt_s, wi_s, *, cap):
    nb, E, _ = aff_ref.shape
    aff = aff_ref[...]

    def count_ge(cand):
        c = jnp.sum((aff >= cand).astype(I32), axis=0, keepdims=True)
        return jnp.sum(c, axis=2, keepdims=True)

    def bit_step(i, thr_bits):
        cand = thr_bits | jnp.left_shift(jnp.int32(1), 30 - i)
        return jnp.where(count_ge(pltpu.bitcast(cand, F32)) >= cap, cand, thr_bits)

    thr = pltpu.bitcast(lax.fori_loop(0, 31, bit_step, jnp.zeros((1, E, 1), I32)), F32)
    gt = aff > thr
    eq = aff == thr
    n_gt = jnp.sum(jnp.sum(gt.astype(I32), axis=0, keepdims=True), axis=2, keepdims=True)
    need = cap - n_gt

    li = lax.broadcasted_iota(I32, (LANES, LANES), 0)
    lj = lax.broadcasted_iota(I32, (LANES, LANES), 1)
    upper = (li < lj).astype(BF16)

    def excl_cumsum(flag):
        fb = flag.astype(BF16).reshape(nb * E, LANES)
        wi_s[...] = _dot(fb, upper).astype(I32).reshape(nb, E, LANES)
        cnt_s[...] = jnp.sum(flag.astype(I32), axis=2, keepdims=True)

        def blk(b, run):
            wi_s[b] = wi_s[b] + run
            return run + cnt_s[b]

        lax.fori_loop(0, nb, blk, jnp.zeros((E, 1), I32))
        return wi_s[...]

    eq_rank = excl_cumsum(eq)
    sel = gt | (eq & (eq_rank < need))
    pos = excl_cumsum(sel)
    posm_ref[...] = jnp.where(sel, pos, -1)
    off_ref[...] = jnp.broadcast_to(pos[:, :, 0:1], off_ref.shape)


def _select(aff3, cap):
    nb, E, _ = aff3.shape
    return pl.pallas_call(
        functools.partial(_select_kernel, cap=cap),
        out_shape=(jax.ShapeDtypeStruct((nb, E, LANES), I32), jax.ShapeDtypeStruct((nb, E, LANES), I32)),
        scratch_shapes=[pltpu.VMEM((nb, E, 1), I32), pltpu.VMEM((nb, E, LANES), I32)],
        compiler_params=_cparams(None), name="select")(aff3)


def _compact_kernel(off_sm, posm_ref, aff_ref, acc_ref):
    nb, E, _ = posm_ref.shape
    acc_ref[...] = jnp.zeros_like(acc_ref)
    srow = lax.broadcasted_iota(I32, (2 * LANES, LANES), 0)
    r8 = lax.broadcasted_iota(I32, (SUBLANES, LANES), 0)
    lane8 = lax.broadcasted_iota(I32, (SUBLANES, LANES), 1)

    def blk(b, carry):
        tok = b * LANES + lane8
        t_hi = jnp.right_shift(tok, 8).astype(F32)
        t_lo = jnp.bitwise_and(tok, 255).astype(F32)
        pm = posm_ref[b]
        af = aff_ref[b]
        for e in range(E):
            off = off_sm[b * E + e]
            j0 = jnp.right_shift(off, 7)
            rel = pm[e:e + 1, :] - j0 * LANES
            onehot = (srow == rel).astype(BF16)
            a = af[e:e + 1, :]
            a0 = a.astype(BF16)
            r1 = a - a0.astype(F32)
            a1 = r1.astype(BF16)
            a2 = (r1 - a1.astype(F32)).astype(BF16)
            lhs = jnp.where(r8 == 0, t_hi, jnp.where(r8 == 1, t_lo, 0.0))
            lhs = jnp.where(r8 == 2, a0.astype(F32), lhs)
            lhs = jnp.where(r8 == 3, a1.astype(F32), lhs)
            lhs = jnp.where(r8 == 4, a2.astype(F32), lhs).astype(BF16)
            out = lax.dot_general(lhs, onehot, _NT, preferred_element_type=F32)
            acc_ref[e, j0] = acc_ref[e, j0] + out[:, :LANES]
            acc_ref[e, j0 + 1] = acc_ref[e, j0 + 1] + out[:, LANES:]
        return carry

    lax.fori_loop(0, nb, blk, 0, unroll=4)


def _compact(off_flat, posm3, aff3, cap):
    nb, E, _ = posm3.shape
    nt_pad = cap // LANES + 2
    gs = pltpu.PrefetchScalarGridSpec(
        num_scalar_prefetch=1, grid=(1,),
        in_specs=[pl.BlockSpec(posm3.shape, lambda i, o: (0, 0, 0)),
                  pl.BlockSpec(aff3.shape, lambda i, o: (0, 0, 0))],
        out_specs=pl.BlockSpec((E, nt_pad, SUBLANES, LANES), lambda i, o: (0, 0, 0, 0)))
    return pl.pallas_call(
        _compact_kernel, out_shape=jax.ShapeDtypeStruct((E, nt_pad, SUBLANES, LANES), F32),
        grid_spec=gs, compiler_params=_cparams(("arbitrary",)), name="compact")(off_flat, posm3, aff3)


def _ffn_kernel(idc_sm, idn_sm, xn_hbm, cacc_ref, wg_ref, wu_ref, wd_ref, y_ref, xbuf, sem,
                *, n_pairs, fc):
    g = pl.program_id(0)
    ts = y_ref.shape[0] // 2
    chunks = xbuf.shape[1] // ts
    D = wd_ref.shape[2]
    F = wg_ref.shape[2]
    nchunk = F // fc
    rows_per_chunk = ts // nchunk
    scale_rows = min(ts, LANES)

    def row_copy(idx_sm, i, r, dst_slot):
        src = pl.multiple_of(idx_sm[0, 0, i] * chunks, chunks)
        return pltpu.make_async_copy(xn_hbm.at[pl.ds(src, chunks), :],
                                     xbuf.at[dst_slot, pl.ds(r * chunks, chunks), :], sem.at[dst_slot])

    def wait_tile(slot):
        pltpu.make_async_copy(xn_hbm.at[pl.ds(0, ts * chunks), :], xbuf.at[slot], sem.at[slot]).wait()

    def run_tile(slot, gathers):
        xs = xbuf.at[slot]
        x = jnp.concatenate([xs[pl.ds(j, ts, stride=chunks), :] for j in range(chunks)],
                            axis=1).astype(BF16)
        acc = jnp.zeros((ts, D), F32)
        for c in range(nchunk):
            for idx_sm, base, dst in gathers:
                for r in range(c * rows_per_chunk, (c + 1) * rows_per_chunk):
                    row_copy(idx_sm, base + r, r, dst).start(priority=r % 2)
            cs = slice(c * fc, (c + 1) * fc)
            gte = _dot(x, wg_ref[0, :, cs])
            up = _dot(x, wu_ref[0, :, cs])
            hid = (gte * _sigmoid(gte) * up).astype(BF16)
            acc = acc + _dot(hid, wd_ref[0, cs, :])
        for k in range(ts // scale_rows):
            r0 = slot * ts + k * scale_rows
            t = cacc_ref[0, r0 // LANES]
            vrow = t[2:3, :] + t[3:4, :] + t[4:5, :]
            vmat = jnp.broadcast_to(vrow, (LANES, LANES)).T[r0 % LANES:r0 % LANES + scale_rows]
            vfull = jnp.concatenate([vmat] * (D // LANES), axis=1)
            y_ref[r0:r0 + scale_rows, :] = (
                acc[k * scale_rows:(k + 1) * scale_rows, :] * vfull).astype(y_ref.dtype)

    @pl.when(g == 0)
    def _():
        def body(r, carry):
            row_copy(idc_sm, r, r, 0).start()
            return carry
        lax.fori_loop(0, ts, body, 0, unroll=8)

    @pl.when(g < n_pairs)
    def _():
        wait_tile(0)
        run_tile(0, [(idc_sm, ts, 1), (idn_sm, 0, 0)])
        wait_tile(1)
        run_tile(1, [])

    @pl.when(g >= n_pairs)
    def _():
        wait_tile(0)
        y_ref[...] = jnp.zeros_like(y_ref)


def _ffn(idx, xn, cacc, wg, wu, wd, cap, ts):
    E, D, F = wg.shape
    pairs_per_expert = cap // (2 * ts)
    n_pairs = E * pairs_per_expert
    last = n_pairs - 1
    idx3 = idx.reshape(n_pairs, 1, 2 * ts)
    eidx = lambda g: jnp.minimum(g // pairs_per_expert, E - 1)
    in_specs = [
        pl.BlockSpec((1, 1, 2 * ts), lambda g: (jnp.minimum(g, last), 0, 0), memory_space=pltpu.SMEM),
        pl.BlockSpec((1, 1, 2 * ts), lambda g: (jnp.minimum(g + 1, last), 0, 0), memory_space=pltpu.SMEM),
        pl.BlockSpec(memory_space=pl.ANY),
        pl.BlockSpec((1, 2 * ts // LANES, SUBLANES, LANES),
                     lambda g: (eidx(g), lax.rem(jnp.minimum(g, last), pairs_per_expert), 0, 0)),
        pl.BlockSpec((1, D, F), lambda g: (eidx(g), 0, 0)),
        pl.BlockSpec((1, D, F), lambda g: (eidx(g), 0, 0)),
        pl.BlockSpec((1, F, D), lambda g: (eidx(g), 0, 0)),
    ]
    return pl.pallas_call(
        functools.partial(_ffn_kernel, n_pairs=n_pairs, fc=min(512, F)),
        out_shape=jax.ShapeDtypeStruct(((n_pairs + 1) * 2 * ts, D), BF16),
        grid=(n_pairs + 1,), in_specs=in_specs,
        out_specs=pl.BlockSpec((2 * ts, D), lambda g: (g, 0)),
        scratch_shapes=[pltpu.VMEM((2, ts * (D // LANES), LANES), F32), pltpu.SemaphoreType.DMA((2,))],
        compiler_params=_cparams(("arbitrary",)), name="ffn")(idx3, idx3, xn, cacc, wg, wu, wd)


WIN = 64


def _combine_kernel(off_sm, posm_ref, x1_ref, y_hbm, fnw_ref, o_ref, ycat, yext, sem, sem_ext,
                    *, cap, y_rows, nblk):
    b = pl.program_id(0)
    slot = lax.rem(b, 2)
    nsub, E, _ = posm_ref.shape
    tb = x1_ref.shape[0]
    pm = jnp.concatenate([posm_ref[j] for j in range(nsub)], axis=1)
    wrow = lax.broadcasted_iota(I32, (WIN, tb), 0)

    def starts_of(blk, r):
        out = []
        for e in range(E):
            base = jnp.left_shift(jnp.right_shift(off_sm[blk * E + e], 4), 4)
            st = jnp.minimum(e * cap + base + r * WIN, y_rows - WIN)
            out.append(pl.multiple_of(st, BF16_ROWS))
        return out

    def copies(starts, dst, dsem):
        return [pltpu.make_async_copy(y_hbm.at[pl.ds(starts[e], WIN), :], dst.at[pl.ds(e * WIN, WIN), :], dsem)
                for e in range(E)]

    def onehot(starts):
        ps = []
        for e in range(E):
            pe = pm[e:e + 1, :]
            rel = jnp.where(pe >= 0, pe + (e * cap - starts[e]), -1)
            ps.append((wrow == rel).astype(BF16))
        return jnp.concatenate(ps, axis=0)

    @pl.when(b == 0)
    def _():
        for cp in copies(starts_of(b, 0), ycat.at[0], sem.at[0]):
            cp.start()

    @pl.when(b + 1 < nblk)
    def _():
        for cp in copies(starts_of(b + 1, 0), ycat.at[1 - slot], sem.at[1 - slot]):
            cp.start()

    starts0 = starts_of(b, 0)
    p0 = onehot(starts0)
    for cp in copies(starts0, ycat.at[slot], sem.at[slot]):
        cp.wait()
    acc0 = lax.dot_general(p0, ycat[slot], _TN, preferred_element_type=F32)

    nrounds = jnp.int32(1)
    for e in range(E):
        base = jnp.left_shift(jnp.right_shift(off_sm[b * E + e], 4), 4)
        nrounds = jnp.maximum(nrounds, jnp.right_shift(off_sm[(b + 1) * E + e] - base + (WIN - 1), 6))

    def round_body(r, acc):
        starts = starts_of(b, r)
        cps = copies(starts, yext, sem_ext)
        for cp in cps:
            cp.start()
        p = onehot(starts)
        for cp in cps:
            cp.wait()
        return acc + lax.dot_general(p, yext[...], _TN, preferred_element_type=F32)

    acc = lax.fori_loop(1, nrounds, round_body, acc0)
    o_ref[...] = _rms(x1_ref[...] + acc, fnw_ref[...])


def _combine(off_flat, posm3, x1, y, fnw, cap, tb):
    N, D = x1.shape
    nb, E, _ = posm3.shape
    nsub = tb // LANES
    y_rows = y.shape[0]
    gs = pltpu.PrefetchScalarGridSpec(
        num_scalar_prefetch=1, grid=(N // tb,),
        in_specs=[pl.BlockSpec((nsub, E, LANES), lambda i, o: (i, 0, 0)),
                  pl.BlockSpec((tb, D), lambda i, o: (i, 0)),
                  pl.BlockSpec(memory_space=pl.ANY),
                  pl.BlockSpec(fnw.shape, lambda i, o: (0, 0))],
        out_specs=pl.BlockSpec((tb, D), lambda i, o: (i, 0)),
        scratch_shapes=[pltpu.VMEM((2, E * WIN, D), BF16), pltpu.VMEM((E * WIN, D), BF16),
                        pltpu.SemaphoreType.DMA((2,)), pltpu.SemaphoreType.DMA])
    return pl.pallas_call(
        functools.partial(_combine_kernel, cap=cap, y_rows=y_rows, nblk=N // tb),
        out_shape=jax.ShapeDtypeStruct((N, D), F32), grid_spec=gs,
        compiler_params=_cparams(("arbitrary",)), name="combine")(off_flat, posm3, x1, y, fnw)


def _prep_params(norm_mix_w, w_in, conv_w, conv_b, gate_b, mlstm_norm_w, sgu_norm_w, sgu_w, sgu_b,
                 w_out, norm_ffn_w, router_w, w_gate, w_up, w_down, norm_final_w):
    d_a = mlstm_norm_w.shape[1]
    d_b = sgu_norm_w.shape[1]
    ng = gate_b.shape[1]
    w = w_in[0]
    o0, o1, o2, o3, o4, o5 = 2 * d_a, 3 * d_a, 4 * d_a, 4 * d_a + ng, 4 * d_a + ng + d_b, 4 * d_a + ng + 2 * d_b
    wg = w[:, o2:o3]
    return dict(
        nw=norm_mix_w[0][None, :],
        wqk=w[:, :o0].astype(BF16), wvt=w[:, o0:o1].T.astype(BF16), wo=w[:, o1:o2].astype(BF16),
        wgt=wg.T.astype(BF16),
        wu=w[:, o3:o4].astype(BF16), ws=w[:, o4:o5].astype(BF16),
        cw=conv_w[0], cb=conv_b[0][None, :], gbt=gate_b[0][:, None],
        snw=sgu_norm_w[0][None, :], mnw=mlstm_norm_w[0][None, :],
        sw=sgu_w[0].astype(BF16),
        sbias=jnp.repeat(sgu_b[0].T, HEAD_DIM, axis=1),
        wout=w_out[0].astype(BF16), fnw=norm_ffn_w[0][None, :], rwt=router_w[0].T,
        nfw=norm_final_w[None, :],
    )


def _mixer(x, p, expert_weights):
    B, T, D = x.shape
    q, k, vt, og, gt, u, s = _in_proj(
        x, p["nw"], p["wqk"], p["wvt"], p["wo"], p["wgt"], p["wu"], p["ws"],
        p["cw"], p["cb"], p["gbt"], p["snw"], min(1024, T))
    n_sub = MLSTM_CHUNKS_PER_STEP if (T // CHUNK) % MLSTM_CHUNKS_PER_STEP == 0 else 1
    nsteps = B * (T // (CHUNK * n_sub))
    slabs = [_cast_slabs(w, nsteps) for w in expert_weights]
    hf, hb, cast = _mlstm(q, k, vt, gt, n_sub, tuple(sl for sl in slabs if sl is not None))
    cast = list(cast)
    weights_bf16 = [w.astype(BF16) if sl is None else cast.pop(0).reshape(w.shape)
                    for w, sl in zip(expert_weights, slabs)]
    return (hf, hb, og, u, s), weights_bf16


def _moe_tail(x, mixed, p, wgate, wup, wdown):
    B, T, D = x.shape
    N = B * T
    E = N_EXPERTS
    cap = (N * CAPACITY_FACTOR) // E
    hf, hb, og, u, s = mixed
    flat = lambda a: a.reshape(N, a.shape[-1])
    x1, xn, aff3 = _mix_out(flat(hf), flat(hb), flat(og), flat(u), flat(s), flat(x),
                            p["mnw"], p["sw"], p["sbias"], p["wout"], p["fnw"], p["rwt"], min(1024, N))
    posm3, off3 = _select(aff3, cap)
    nb = N // LANES
    off_flat = off3[:, :, 0].reshape(nb * E)
    cacc = _compact(off_flat, posm3, aff3, cap)
    nt = cap // LANES
    idx = (cacc[:, :nt, 0, :] * 256.0 + cacc[:, :nt, 1, :]).astype(I32).reshape(E * cap)
    ts = min(512, cap // 2)
    y = _ffn(idx, xn, cacc, wgate, wup, wdown, cap, ts)
    tb = min(256, N)
    sub = tb // LANES
    off_tb = jnp.concatenate([off3[::sub, :, 0], jnp.full((1, E), cap, I32)], axis=0).reshape(-1)
    out = _combine(off_tb, posm3, x1, y, p["nfw"], cap, tb)
    return out.reshape(B, T, D)


def kernel(x_prompt, x_sample, norm_mix_w, w_in, conv_w, conv_b, gate_b, mlstm_norm_w, sgu_norm_w,
           sgu_w, sgu_b, w_out, norm_ffn_w, router_w, w_gate, w_up, w_down, norm_final_w):
    p = _prep_params(norm_mix_w, w_in, conv_w, conv_b, gate_b, mlstm_norm_w, sgu_norm_w, sgu_w,
                     sgu_b, w_out, norm_ffn_w, router_w, w_gate, w_up, w_down, norm_final_w)
    mixed_p, (wgate,) = _mixer(x_prompt, p, [w_gate[0]])
    mixed_s, (wup, wdown) = _mixer(x_sample, p, [w_up[0], w_down[0]])
    return (_moe_tail(x_prompt, mixed_p, p, wgate, wup, wdown),
            _moe_tail(x_sample, mixed_s, p, wgate, wup, wdown))
```
